```python
import jax
import jax.numpy as jnp
from jax import lax
import numpy as np


D_MODEL = 1024
BATCH = 8
SEQ = 4096
DEPTH = 1

CTX_LEN = 256
GRID_W = 64

RET_HEADS = 4
RET_DK = 128
RET_DV = 128
RET_CHUNK = 128
RET_W = RET_HEADS * RET_DV
RET_SCALE = RET_DK ** -0.5

ATT_HEADS = 8
ATT_KV_HEADS = 2
ATT_GROUP = ATT_HEADS // ATT_KV_HEADS
ATT_HD = 64
ATT_W = ATT_HEADS * ATT_HD
ATT_SCALE = ATT_HD ** -0.5
Q_BLOCK = 128

MIX_W = RET_W + ATT_W
D_FF = 4 * D_MODEL
ROPE_BASE = 10000.0
EPS = 1e-6

RQ_OFF = 0
RK_OFF = RQ_OFF + RET_HEADS * RET_DK
RV_OFF = RK_OFF + RET_HEADS * RET_DK
RG_OFF = RV_OFF + RET_W
AQ_OFF = RG_OFF + RET_W
AK_OFF = AQ_OFF + ATT_W
AV_OFF = AK_OFF + ATT_KV_HEADS * ATT_HD
D_IN = AV_OFF + ATT_KV_HEADS * ATT_HD
SPLIT_IDX = [RK_OFF, RV_OFF, RG_OFF, AQ_OFF, AK_OFF, AV_OFF]

kernel_name = "hybrid_retention_gqa_dit_layer"


def _rmsnorm(x, g):
    xf = x.astype(jnp.float32)
    y = xf * lax.rsqrt(jnp.mean(xf * xf, axis=-1, keepdims=True) + EPS)
    return (y * g.astype(jnp.float32)).astype(x.dtype)


def _modulate(h, shift, scale):
    return h * (1 + scale) + shift


def _freqs(n_pairs):
    return ROPE_BASE ** (-jnp.arange(n_pairs, dtype=jnp.float32) / n_pairs)


def _apply_rope(x, cos, sin):
    x1, x2 = jnp.split(x, 2, axis=-1)
    return jnp.concatenate([x1 * cos - x2 * sin, x1 * sin + x2 * cos], axis=-1).astype(x.dtype)


def _heads(t, n_heads):
    B, L, W = t.shape
    return t.reshape(B, L, n_heads, W // n_heads).transpose(0, 2, 1, 3)


def _retention_chunkwise(q, k, v, log_gamma, s0):
    B, H, L, dk = q.shape
    dv = v.shape[-1]
    n_chunks = L // RET_CHUNK

    def chunks(t):
        return jnp.moveaxis(t.reshape(B, H, n_chunks, RET_CHUNK, t.shape[-1]), 2, 0)

    idx = jnp.arange(RET_CHUNK, dtype=jnp.float32)
    lg = log_gamma.astype(jnp.float32)[:, None]
    rel = idx[:, None] - idx[None, :]
    intra = jnp.where(rel >= 0, jnp.exp(lg[:, :, None] * jnp.maximum(rel, 0.0)), 0.0)
    q_dec = jnp.exp(lg * (idx + 1.0))[:, :, None]
    k_dec = jnp.exp(lg * (RET_CHUNK - 1.0 - idx))[:, :, None]
    c_dec = jnp.exp(lg * RET_CHUNK)[:, :, None]

    def step(state, qkv):
        qf, kf, vf = (t.astype(jnp.float32) for t in qkv)
        scores = jnp.einsum('bhid,bhjd->bhij', qf, kf) * intra
        o = (jnp.einsum('bhij,bhjv->bhiv', scores, vf)
             + jnp.einsum('bhid,bhdv->bhiv', qf, state) * q_dec)
        state = state * c_dec + jnp.einsum('bhjd,bhjv->bhdv', kf * k_dec, vf)
        return state, o

    s_fin, o = lax.scan(step, s0, (chunks(q), chunks(k), chunks(v)))
    o = jnp.moveaxis(o, 0, 2).reshape(B, H, L, dv)
    return o, s_fin


def _retention_final_state(k, v, log_gamma):
    L = k.shape[2]
    pos = jnp.arange(L, dtype=jnp.float32)
    w = jnp.exp(log_gamma.astype(jnp.float32)[:, None] * (L - 1.0 - pos))
    return jnp.einsum('bhld,bhlv->bhdv', k.astype(jnp.float32) * w[:, :, None], v.astype(jnp.float32))


def _flip(t):
    return jnp.flip(t, axis=2)


def _bidir_retention(q, k, v, log_gamma, s0_fwd, s0_bwd):
    o_f, s_f = _retention_chunkwise(q, k, v, log_gamma[0], s0_fwd)
    o_b, s_b = _retention_chunkwise(_flip(q), _flip(k), _flip(v), log_gamma[1], s0_bwd)
    return o_f + _flip(o_b), s_f, s_b


def _ret_out(o, gate, gn_g):
    B, H, L, dv = o.shape
    mu = jnp.mean(o, axis=-1, keepdims=True)
    var = jnp.mean(jnp.square(o - mu), axis=-1, keepdims=True)
    o = ((o - mu) * lax.rsqrt(var + EPS)).transpose(0, 2, 1, 3).reshape(B, L, H * dv)
    o = o * gn_g.astype(jnp.float32)
    return (o * jax.nn.silu(gate.astype(jnp.float32))).astype(gate.dtype)


def _dense_attention(q, k, v):
    s = jnp.einsum('bqkgd,bskd->bkgqs', q, k).astype(jnp.float32) * ATT_SCALE
    p = jax.nn.softmax(s, axis=-1).astype(v.dtype)
    return jnp.einsum('bkgqs,bskd->bqkgd', p, v)


def _latent_attention(q, k_lat, v_lat, k_ctx, v_ctx):
    B, L = q.shape[:2]
    k_all = jnp.concatenate([k_ctx, k_lat], axis=1)
    v_all = jnp.concatenate([v_ctx, v_lat], axis=1)
    qb = jnp.moveaxis(q.reshape(B, L // Q_BLOCK, Q_BLOCK, ATT_KV_HEADS, ATT_GROUP, ATT_HD), 1, 0)
    o = lax.map(lambda qblk: _dense_attention(qblk, k_all, v_all), qb)
    return jnp.moveaxis(o, 0, 1).reshape(B, L, ATT_W)


def _sqrelu_mlp(h, w1, w2):
    return jnp.square(jax.nn.relu(h @ w1)) @ w2


def _layer(x, xc, c_act, cc_act, w_mod, b_mod, norm1_g, norm2_g, w_in, w_out,
           ret_log_rate, ret_gn_g, q_norm_g, k_norm_g, w_ff1, w_ff2,
           ret_cs, att_cs, ctx_out):
    B, L, _ = x.shape
    Lc = xc.shape[1]
    mod = (c_act @ w_mod + b_mod)[:, None, :]
    mod_c = (cc_act @ w_mod + b_mod)[None, None, :]
    sh1, sc1, g1, sh2, sc2, g2 = jnp.split(mod, 6, axis=-1)
    csh1, csc1, cg1, csh2, csc2, cg2 = jnp.split(mod_c, 6, axis=-1)
    log_gamma = jnp.log1p(-jnp.exp(ret_log_rate.astype(jnp.float32)))

    h = _modulate(_rmsnorm(x, norm1_g), sh1, sc1)
    hc = _modulate(_rmsnorm(xc, norm1_g), csh1, csc1)
    rq, rk, rv, rg, aq, ak, av = jnp.split(h @ w_in, SPLIT_IDX, axis=-1)
    if ctx_out:
        crq, crk, crv, crg, caq, cak, cav = jnp.split(hc @ w_in, SPLIT_IDX, axis=-1)
    else:
        crk, crv = jnp.split(hc @ w_in[:, RK_OFF:RG_OFF], 2, axis=-1)
        cak, cav = jnp.split(hc @ w_in[:, AK_OFF:D_IN], 2, axis=-1)

    crk_h = _heads(crk, RET_HEADS) * RET_SCALE
    crv_h = _heads(crv, RET_HEADS)
    if ctx_out:
        zeros = jnp.zeros((B, RET_HEADS, RET_DK, RET_DV), jnp.float32)
        ret_c, s_f, s_b = _bidir_retention(_heads(crq, RET_HEADS), crk_h, crv_h, log_gamma, zeros, zeros)
    else:
        s_f = _retention_final_state(crk_h, crv_h, log_gamma[0])
        s_b = _retention_final_state(_flip(crk_h), _flip(crv_h), log_gamma[1])
    rq_h = _apply_rope(_heads(rq, RET_HEADS), *ret_cs)
    rk_h = _apply_rope(_heads(rk, RET_HEADS), *ret_cs) * RET_SCALE
    ret, _, _ = _bidir_retention(rq_h, rk_h, _heads(rv, RET_HEADS), log_gamma, s_f, s_b)
    ret = _ret_out(ret, rg, ret_gn_g)

    q = _apply_rope(_rmsnorm(aq.reshape(B, L, ATT_HEADS, ATT_HD), q_norm_g), *att_cs)
    k = _apply_rope(_rmsnorm(ak.reshape(B, L, ATT_KV_HEADS, ATT_HD), k_norm_g), *att_cs)
    v = av.reshape(B, L, ATT_KV_HEADS, ATT_HD)
    kc = _rmsnorm(cak.reshape(B, Lc, ATT_KV_HEADS, ATT_HD), k_norm_g)
    vc = cav.reshape(B, Lc, ATT_KV_HEADS, ATT_HD)
    att = _latent_attention(q.reshape(B, L, ATT_KV_HEADS, ATT_GROUP, ATT_HD), k, v, kc, vc)

    x = x + g1 * (jnp.concatenate([ret, att], axis=-1) @ w_out)

    h2 = _modulate(_rmsnorm(x, norm2_g), sh2, sc2)
    x = x + g2 * _sqrelu_mlp(h2, w_ff1, w_ff2)

    if ctx_out:
        ret_c = _ret_out(ret_c, crg, ret_gn_g)
        qc = _rmsnorm(caq.reshape(B, Lc, ATT_KV_HEADS, ATT_GROUP, ATT_HD), q_norm_g)
        att_c = _dense_attention(qc, kc, vc).reshape(B, Lc, ATT_W)
        xc = xc + cg1 * (jnp.concatenate([ret_c, att_c], axis=-1) @ w_out)
        hc2 = _modulate(_rmsnorm(xc, norm2_g), csh2, csc2)
        xc = xc + cg2 * _sqrelu_mlp(hc2, w_ff1, w_ff2)
    return x, xc


def setup_inputs(seed: int = 0) -> dict:
    key = jax.random.key(seed)
    ks = jax.random.split(key, 17)
    f32 = jnp.float32
    nrm = lambda k, shape, s: jax.random.normal(k, shape, f32) * s
    base_rate = -(5.0 + jnp.arange(RET_HEADS, dtype=f32)) * np.float32(np.log(2.0))
    return {
        "x": nrm(ks[0], (BATCH, SEQ, D_MODEL), 1.0),
        "c": nrm(ks[1], (BATCH, D_MODEL), 1.0),
        "ctx": nrm(ks[2], (BATCH, CTX_LEN, D_MODEL), 1.0),
        "c_ctx": nrm(ks[3], (D_MODEL,), 1.0),
        "w_mod": nrm(ks[4], (DEPTH, D_MODEL, 6 * D_MODEL), 0.01),
        "b_mod": nrm(ks[5], (DEPTH, 6 * D_MODEL), 0.02),
        "norm1_g": 1.0 + nrm(ks[6], (DEPTH, D_MODEL), 0.02),
        "norm2_g": 1.0 + nrm(ks[7], (DEPTH, D_MODEL), 0.02),
        "w_in": nrm(ks[8], (DEPTH, D_MODEL, D_IN), D_MODEL ** -0.5),
        "w_out": nrm(ks[9], (DEPTH, MIX_W, D_MODEL), MIX_W ** -0.5),
        "ret_log_rate": base_rate[None, None, :] + nrm(ks[10], (DEPTH, 2, RET_HEADS), 0.05),
        "ret_gn_g": 1.0 + nrm(ks[11], (DEPTH, RET_W), 0.02),
        "q_norm_g": 1.0 + nrm(ks[12], (DEPTH, ATT_HD), 0.02),
        "k_norm_g": 1.0 + nrm(ks[13], (DEPTH, ATT_HD), 0.02),
        "w_ff1": nrm(ks[14], (DEPTH, D_MODEL, D_FF), D_MODEL ** -0.5),
        "w_ff2": nrm(ks[15], (DEPTH, D_FF, D_MODEL), D_FF ** -0.5),
        "final_norm_g": 1.0 + nrm(ks[16], (D_MODEL,), 0.02),
    }


def reference(x, c, ctx, c_ctx, w_mod, b_mod, norm1_g, norm2_g, w_in, w_out,
              ret_log_rate, ret_gn_g, q_norm_g, k_norm_g, w_ff1, w_ff2, final_norm_g):
    L = x.shape[1]
    ROWS = L // GRID_W
    t = jnp.arange(L, dtype=jnp.float32)
    ret_ang = t[:, None] * _freqs(RET_DK // 2)
    ret_cs = (jnp.cos(ret_ang), jnp.sin(ret_ang))
    row = jnp.repeat(jnp.arange(ROWS, dtype=jnp.float32), GRID_W)
    col = jnp.tile(jnp.arange(GRID_W, dtype=jnp.float32), ROWS)
    af = _freqs(ATT_HD // 4)
    att_ang = jnp.concatenate([row[:, None] * af, col[:, None] * af], axis=-1)[:, None, :]
    att_cs = (jnp.cos(att_ang), jnp.sin(att_ang))

    c_act = jax.nn.silu(c)
    cc_act = jax.nn.silu(c_ctx)
    xc = ctx
    for l in range(DEPTH):
        x, xc = _layer(x, xc, c_act, cc_act, w_mod[l], b_mod[l], norm1_g[l], norm2_g[l],
                       w_in[l], w_out[l], ret_log_rate[l], ret_gn_g[l], q_norm_g[l], k_norm_g[l],
                       w_ff1[l], w_ff2[l], ret_cs, att_cs, l < DEPTH - 1)
    return _rmsnorm(x, final_norm_g)
```

```python
import functools

import jax
import jax.numpy as jnp
from jax import lax
from jax.experimental import pallas as pl
from jax.experimental.pallas import tpu as pltpu

D_MODEL = 1024
CTX_LEN = 256
GRID_W = 64

RET_HEADS = 4
RET_DK = 128
RET_DV = 128
RET_W = RET_HEADS * RET_DV
RET_SCALE = RET_DK ** -0.5

ATT_HEADS = 8
ATT_KV_HEADS = 2
ATT_GROUP = ATT_HEADS // ATT_KV_HEADS
ATT_HD = 64
ATT_W = ATT_HEADS * ATT_HD
ATT_SCALE = ATT_HD ** -0.5

MIX_W = RET_W + ATT_W
D_FF = 4 * D_MODEL
ROPE_BASE = 10000.0
EPS = 1e-6

RQ_OFF = 0
RK_OFF = RQ_OFF + RET_HEADS * RET_DK
RV_OFF = RK_OFF + RET_HEADS * RET_DK
RG_OFF = RV_OFF + RET_W
AQ_OFF = RG_OFF + RET_W
AK_OFF = AQ_OFF + ATT_W
AV_OFF = AK_OFF + ATT_KV_HEADS * ATT_HD
D_IN = AV_OFF + ATT_KV_HEADS * ATT_HD

LANES = 128
MXU_DIM = 256
VMEM_LIMIT = 56 * 1024 * 1024

MOD_ROWS = 16
TM_PROJ = 512
TQ_ATT = 256
RET_CHUNK = 256
FF_CHUNK = 1024

F32 = jnp.float32
BF16 = jnp.bfloat16


def _dot(a, b):
    return jnp.dot(a, b, preferred_element_type=F32)


def _dot_nt(a, b):
    return lax.dot_general(a, b, (((1,), (1,)), ((), ())), preferred_element_type=F32)


def _dot_tn(a, b):
    return lax.dot_general(a, b, (((0,), (0,)), ((), ())), preferred_element_type=F32)


def _silu(x):
    return x * jax.nn.sigmoid(x)


def _head_mean_matrix(n):
    r = lax.broadcasted_iota(jnp.int32, (n, n), 0) // ATT_HD
    c = lax.broadcasted_iota(jnp.int32, (n, n), 1) // ATT_HD
    return jnp.where(r == c, 1.0 / ATT_HD, 0.0).astype(BF16)


def _head_rms_scale(blk, bd):
    ms = _dot((blk * blk).astype(BF16), bd)
    return lax.rsqrt(ms + EPS)


def _rope_ret(blk, c2, s2):
    return blk * c2 + pltpu.roll(blk, RET_DK // 2, 1) * s2


def _rope_att(blk, c2, s2, first_half):
    up = pltpu.roll(blk, ATT_HD // 2, 1)
    dn = pltpu.roll(blk, LANES - ATT_HD // 2, 1)
    return blk * c2 + jnp.where(first_half, dn, up) * s2


def _mod_kernel(c_ref, w_ref, b_ref, o_ref):
    a = _silu(c_ref[...])
    o_ref[...] = jnp.dot(a, w_ref[...], preferred_element_type=F32,
                         precision=lax.Precision.HIGHEST) + b_ref[...]


def _mod_call(c_rows, w_mod, b_mod):
    n = w_mod.shape[1]
    bn = 1024
    return pl.pallas_call(
        _mod_kernel,
        grid=(n // bn,),
        in_specs=[
            pl.BlockSpec((MOD_ROWS, D_MODEL), lambda j: (0, 0)),
            pl.BlockSpec((D_MODEL, bn), lambda j: (0, j)),
            pl.BlockSpec((1, bn), lambda j: (0, j)),
        ],
        out_specs=pl.BlockSpec((MOD_ROWS, bn), lambda j: (0, j)),
        out_shape=jax.ShapeDtypeStruct((MOD_ROWS, n), F32),
        compiler_params=pltpu.CompilerParams(dimension_semantics=("arbitrary",),
                                             vmem_limit_bytes=VMEM_LIMIT),
        name="mod",
    )(c_rows, w_mod, b_mod)


def _norm_modulate(x, g, sh, sc):
    ms = jnp.mean(x * x, axis=-1, keepdims=True)
    y = x * lax.rsqrt(ms + EPS) * g
    return y * (1.0 + sc) + sh


def _ctx_proj_kernel(x_ref, sh_ref, sc_ref, g_ref, wr_ref, wa_ref, gk_ref, ret_ref, akv_ref):
    h = _norm_modulate(x_ref[...], g_ref[...], sh_ref[...], sc_ref[...]).astype(BF16)
    pr = _dot(h, wr_ref[...])
    ret_ref[:, :RET_W] = (pr[:, :RET_W] * RET_SCALE).astype(BF16)
    ret_ref[:, RET_W:] = pr[:, RET_W:].astype(BF16)
    pa = _dot(h, wa_ref[...])
    ak = pa[:, :LANES]
    r = _head_rms_scale(ak, _head_mean_matrix(LANES))
    akv_ref[:, :LANES] = (ak * r * gk_ref[...]).astype(BF16)
    akv_ref[:, LANES:] = pa[:, LANES:].astype(BF16)


def _ctx_proj_call(ctx, mod3, norm1_g, w_rkv, w_akv, gk2):
    B, Lc, D = ctx.shape
    ctx_row = B
    const = lambda b: (0, 0)
    return pl.pallas_call(
        _ctx_proj_kernel,
        grid=(B,),
        in_specs=[
            pl.BlockSpec((None, Lc, D), lambda b: (b, 0, 0)),
            pl.BlockSpec((None, 1, D), lambda b: (ctx_row, 0, 0)),
            pl.BlockSpec((None, 1, D), lambda b: (ctx_row, 0, 1)),
            pl.BlockSpec((1, D), const),
            pl.BlockSpec(w_rkv.shape, const),
            pl.BlockSpec(w_akv.shape, const),
            pl.BlockSpec((1, LANES), const),
        ],
        out_specs=[
            pl.BlockSpec((None, Lc, 2 * RET_W), lambda b: (b, 0, 0)),
            pl.BlockSpec((None, Lc, 2 * LANES), lambda b: (b, 0, 0)),
        ],
        out_shape=[
            jax.ShapeDtypeStruct((B, Lc, 2 * RET_W), BF16),
            jax.ShapeDtypeStruct((B, Lc, 2 * LANES), BF16),
        ],
        compiler_params=pltpu.CompilerParams(dimension_semantics=("arbitrary",),
                                             vmem_limit_bytes=VMEM_LIMIT),
        name="ctx_proj",
    )(ctx, mod3, mod3, norm1_g, w_rkv, w_akv, gk2)


def _in_proj_kernel(x_ref, sh_ref, sc_ref, g_ref, w_ref, cr_ref, sr_ref, ca_ref, sa_ref,
                    gq_ref, gk_ref, ret_ref, aq_ref, akv_ref):
    tm = x_ref.shape[0]
    h = _norm_modulate(x_ref[...], g_ref[...], sh_ref[...], sc_ref[...]).astype(BF16)
    cr, sr = cr_ref[...], sr_ref[...]
    ca, sa = ca_ref[...], sa_ref[...]
    lane = lax.broadcasted_iota(jnp.int32, (tm, LANES), 1)
    first_half = (lane & (ATT_HD // 2)) == 0

    pq = _dot(h, w_ref[:, RQ_OFF:RK_OFF])
    for hh in range(RET_HEADS):
        sl = slice(hh * RET_DK, (hh + 1) * RET_DK)
        ret_ref[:, RQ_OFF + hh * RET_DK:RQ_OFF + (hh + 1) * RET_DK] = _rope_ret(pq[:, sl], cr, sr).astype(BF16)
    pk = _dot(h, w_ref[:, RK_OFF:RV_OFF])
    for hh in range(RET_HEADS):
        sl = slice(hh * RET_DK, (hh + 1) * RET_DK)
        ret_ref[:, RK_OFF + hh * RET_DK:RK_OFF + (hh + 1) * RET_DK] = (
            _rope_ret(pk[:, sl], cr, sr) * RET_SCALE).astype(BF16)
    ret_ref[:, RV_OFF:AQ_OFF] = _dot(h, w_ref[:, RV_OFF:AQ_OFF]).astype(BF16)

    bd = _head_mean_matrix(MXU_DIM)
    gq, gk = gq_ref[...], gk_ref[...]
    pa = _dot(h, w_ref[:, AQ_OFF:AK_OFF])
    for j in range(ATT_W // MXU_DIM):
        blk = pa[:, j * MXU_DIM:(j + 1) * MXU_DIM]
        r = _head_rms_scale(blk, bd) * ATT_SCALE
        for i in range(MXU_DIM // LANES):
            sl = slice(i * LANES, (i + 1) * LANES)
            o = _rope_att(blk[:, sl] * gq, ca, sa, first_half) * r[:, sl]
            aq_ref[:, j * MXU_DIM + i * LANES:j * MXU_DIM + (i + 1) * LANES] = o.astype(BF16)

    pkv = _dot(h, w_ref[:, AK_OFF:D_IN])
    ak = pkv[:, :LANES]
    rk = _head_rms_scale(ak, bd[:LANES, :LANES])
    akv_ref[:, :LANES] = (_rope_att(ak * gk, ca, sa, first_half) * rk).astype(BF16)
    akv_ref[:, LANES:] = pkv[:, LANES:].astype(BF16)


def _in_proj_call(x, mod3, norm1_g, w_in, cr, sr, ca, sa, gq2, gk2):
    B, L, D = x.shape
    tm = TM_PROJ
    const = lambda b, i: (0, 0)
    tok = lambda b, i: (i, 0)
    return pl.pallas_call(
        _in_proj_kernel,
        grid=(B, L // tm),
        in_specs=[
            pl.BlockSpec((None, tm, D), lambda b, i: (b, i, 0)),
            pl.BlockSpec((None, 1, D), lambda b, i: (b, 0, 0)),
            pl.BlockSpec((None, 1, D), lambda b, i: (b, 0, 1)),
            pl.BlockSpec((1, D), const),
            pl.BlockSpec(w_in.shape, const),
            pl.BlockSpec((tm, LANES), tok),
            pl.BlockSpec((tm, LANES), tok),
            pl.BlockSpec((tm, LANES), tok),
            pl.BlockSpec((tm, LANES), tok),
            pl.BlockSpec((1, LANES), const),
            pl.BlockSpec((1, LANES), const),
        ],
        out_specs=[
            pl.BlockSpec((None, tm, AQ_OFF), lambda b, i: (b, i, 0)),
            pl.BlockSpec((None, tm, ATT_W), lambda b, i: (b, i, 0)),
            pl.BlockSpec((None, tm, 2 * LANES), lambda b, i: (b, i, 0)),
        ],
        out_shape=[
            jax.ShapeDtypeStruct((B, L, AQ_OFF), BF16),
            jax.ShapeDtypeStruct((B, L, ATT_W), BF16),
            jax.ShapeDtypeStruct((B, L, 2 * LANES), BF16),
        ],
        compiler_params=pltpu.CompilerParams(dimension_semantics=("arbitrary", "arbitrary"),
                                             vmem_limit_bytes=VMEM_LIMIT),
        name="in_proj",
    )(x, mod3, mod3, norm1_g, w_in, cr, sr, ca, sa, gq2, gk2)


def _retention_kernel(q_ref, k_ref, v_ref, g_ref, kc_ref, vc_ref, rate_ref, gn_ref, o_ref, sb_ref):
    L = q_ref.shape[0]
    Lc = kc_ref.shape[0]
    C = RET_CHUNK
    T = L // C

    lg = jnp.log1p(-jnp.exp(rate_ref[...]))
    lgf, lgb = lg[0], lg[1]
    lgf1, lgb1 = lgf[:, :RET_DK], lgb[:, :RET_DK]

    ri = lax.broadcasted_iota(jnp.int32, (C, C), 0)
    ci = lax.broadcasted_iota(jnp.int32, (C, C), 1)
    rel = (ri - ci).astype(F32)
    decay = (jnp.where(rel >= 0, jnp.exp(lgf * jnp.maximum(rel, 0.0)), 0.0)
             + jnp.where(rel <= 0, jnp.exp(lgb * jnp.maximum(-rel, 0.0)), 0.0))

    pos = lax.broadcasted_iota(jnp.int32, (C, RET_DK), 0).astype(F32)
    q_dec_f = jnp.exp(lgf1 * (pos + 1.0))
    q_dec_b = jnp.exp(lgb1 * (C - pos))
    k_dec_f = jnp.exp(lgf1 * (C - 1.0 - pos))
    k_dec_b = jnp.exp(lgb1 * pos)
    c_dec_f = jnp.exp(lgf1 * C)
    c_dec_b = jnp.exp(lgb1 * C)

    cpos = lax.broadcasted_iota(jnp.int32, (Lc, RET_DK), 0).astype(F32)
    kc = kc_ref[...].astype(F32)
    vc = vc_ref[...]
    s_f0 = _dot_tn((kc * jnp.exp(lgf1 * (Lc - 1.0 - cpos))).astype(BF16), vc)
    s_b0 = _dot_tn((kc * jnp.exp(lgb1 * cpos)).astype(BF16), vc)

    def bwd_body(i, s_b):
        t = T - 1 - i
        sb_ref[t] = s_b
        rows = pl.ds(pl.multiple_of(t * C, C), C)
        kd = (k_ref[rows, :].astype(F32) * k_dec_b).astype(BF16)
        return s_b * c_dec_b + _dot_tn(kd, v_ref[rows, :])

    lax.fori_loop(0, T, bwd_body, s_b0)

    gn = gn_ref[...]

    def fwd_body(t, s_f):
        rows = pl.ds(pl.multiple_of(t * C, C), C)
        q = q_ref[rows, :]
        k = k_ref[rows, :]
        v = v_ref[rows, :]
        scores = _dot_nt(q, k) * decay
        qf = q.astype(F32)
        q_cat = jnp.concatenate([(qf * q_dec_f).astype(BF16), (qf * q_dec_b).astype(BF16)], axis=1)
        s_cat = jnp.concatenate([s_f, sb_ref[t]], axis=0).astype(BF16)
        o = _dot(scores.astype(BF16), v) + _dot(q_cat, s_cat)
        mu = jnp.mean(o, axis=-1, keepdims=True)
        d = o - mu
        var = jnp.mean(d * d, axis=-1, keepdims=True)
        on = d * lax.rsqrt(var + EPS) * gn
        o_ref[rows, :] = (on * _silu(g_ref[rows, :].astype(F32))).astype(o_ref.dtype)
        kd = (k.astype(F32) * k_dec_f).astype(BF16)
        return s_f * c_dec_f + _dot_tn(kd, v)

    lax.fori_loop(0, T, fwd_body, s_f0)


def _retention_call(ret_in, ctx_ret, rate, gn_g):
    B, L, _ = ret_in.shape
    Lc = ctx_ret.shape[1]
    H = RET_HEADS
    T = L // RET_CHUNK

    def col(off):
        return lambda b, h: (b, 0, off + h)

    return pl.pallas_call(
        _retention_kernel,
        grid=(B, H),
        in_specs=[
            pl.BlockSpec((None, L, RET_DK), col(RQ_OFF // RET_DK)),
            pl.BlockSpec((None, L, RET_DK), col(RK_OFF // RET_DK)),
            pl.BlockSpec((None, L, RET_DV), col(RV_OFF // RET_DV)),
            pl.BlockSpec((None, L, RET_DV), col(RG_OFF // RET_DV)),
            pl.BlockSpec((None, Lc, RET_DK), col(0)),
            pl.BlockSpec((None, Lc, RET_DV), col(RET_HEADS)),
            pl.BlockSpec((2, None, 1, RET_CHUNK), lambda b, h: (0, h, 0, 0)),
            pl.BlockSpec((1, RET_DV), lambda b, h: (0, h)),
        ],
        out_specs=pl.BlockSpec((None, L, RET_DV), col(0)),
        out_shape=jax.ShapeDtypeStruct((B, L, RET_W), BF16),
        scratch_shapes=[pltpu.VMEM((T, RET_DK, RET_DV), F32)],
        compiler_params=pltpu.CompilerParams(dimension_semantics=("arbitrary", "arbitrary"),
                                             vmem_limit_bytes=VMEM_LIMIT),
        name="retention",
    )(ret_in, ret_in, ret_in, ret_in, ctx_ret, ctx_ret, rate, gn_g)


def _attention_kernel(q_ref, k_ref, v_ref, kc_ref, vc_ref, o_ref, kt_ref, vt_ref):
    kv = pl.program_id(1)
    Lc = kc_ref.shape[0]
    L = k_ref.shape[0]

    @pl.when(pl.program_id(2) == 0)
    def _():
        def tile4(x):
            xf = x.astype(F32)
            lane = lax.broadcasted_iota(jnp.int32, xf.shape, 1)
            own = (lane // ATT_HD) == kv
            two = jnp.where(own, xf, pltpu.roll(xf, ATT_HD, 1)).astype(BF16)
            return jnp.concatenate([two, two], axis=1)

        kt_ref[0:Lc, :] = tile4(kc_ref[...])
        kt_ref[Lc:Lc + L, :] = tile4(k_ref[...])
        vt_ref[0:Lc, :] = tile4(vc_ref[...])
        vt_ref[Lc:Lc + L, :] = tile4(v_ref[...])

    q = q_ref[...]
    group = lax.broadcasted_iota(jnp.int32, q.shape, 1) // ATT_HD
    out = jnp.zeros(q.shape, F32)
    for g in range(ATT_GROUP):
        qg = jnp.where(group == g, q, jnp.zeros_like(q))
        s = _dot_nt(qg, kt_ref[...])
        m = jnp.max(s, axis=-1, keepdims=True)
        p = jnp.exp(s - m)
        l = jnp.sum(p, axis=-1, keepdims=True)
        pv = _dot(p.astype(BF16), vt_ref[...])
        out = jnp.where(group == g, pv / l, out)
    o_ref[...] = out.astype(o_ref.dtype)


def _attention_call(aq, akv, ctx_akv):
    B, L, _ = aq.shape
    Lc = ctx_akv.shape[1]
    tq = TQ_ATT
    S = Lc + L
    return pl.pallas_call(
        _attention_kernel,
        grid=(B, ATT_KV_HEADS, L // tq),
        in_specs=[
            pl.BlockSpec((None, tq, ATT_GROUP * ATT_HD), lambda b, kv, i: (b, i, kv)),
            pl.BlockSpec((None, L, LANES), lambda b, kv, i: (b, 0, 0)),
            pl.BlockSpec((None, L, LANES), lambda b, kv, i: (b, 0, 1)),
            pl.BlockSpec((None, Lc, LANES), lambda b, kv, i: (b, 0, 0)),
            pl.BlockSpec((None, Lc, LANES), lambda b, kv, i: (b, 0, 1)),
        ],
        out_specs=pl.BlockSpec((None, tq, ATT_GROUP * ATT_HD), lambda b, kv, i: (b, i, kv)),
        out_shape=jax.ShapeDtypeStruct((B, L, ATT_W), BF16),
        scratch_shapes=[pltpu.VMEM((S, MXU_DIM), BF16), pltpu.VMEM((S, MXU_DIM), BF16)],
        compiler_params=pltpu.CompilerParams(
            dimension_semantics=("arbitrary", "arbitrary", "arbitrary"),
            vmem_limit_bytes=VMEM_LIMIT),
        name="attention",
    )(aq, akv, akv, ctx_akv, ctx_akv)


def _out_ffn_kernel(x_ref, ret_ref, att_ref, g1_ref, sh2_ref, sc2_ref, g2_ref, n2_ref, nf_ref,
                    wo_r_ref, wo_a_ref, w1_ref, w2_ref, o_ref):
    mix = _dot(ret_ref[...], wo_r_ref[...]) + _dot(att_ref[...], wo_a_ref[...])
    x1 = x_ref[...] + g1_ref[...] * mix
    h2 = _norm_modulate(x1, n2_ref[...], sh2_ref[...], sc2_ref[...]).astype(BF16)
    ff = jnp.zeros(x1.shape, F32)
    for j in range(D_FF // FF_CHUNK):
        a = jnp.maximum(_dot(h2, w1_ref[:, j * FF_CHUNK:(j + 1) * FF_CHUNK]), 0.0)
        ff = ff + _dot((a * a).astype(BF16), w2_ref[j * FF_CHUNK:(j + 1) * FF_CHUNK, :])
    x2 = x1 + g2_ref[...] * ff
    ms = jnp.mean(x2 * x2, axis=-1, keepdims=True)
    o_ref[...] = x2 * lax.rsqrt(ms + EPS) * nf_ref[...]


def _out_ffn_call(x, ret, att, mod3, norm2_g, final_g, wo_r, wo_a, w1, w2):
    B, L, D = x.shape
    tm = TM_PROJ
    const = lambda b, i: (0, 0)
    once = pl.Buffered(1)

    def modrow(j):
        return pl.BlockSpec((None, 1, D), lambda b, i: (b, 0, j))

    return pl.pallas_call(
        _out_ffn_kernel,
        grid=(B, L // tm),
        in_specs=[
            pl.BlockSpec((None, tm, D), lambda b, i: (b, i, 0)),
            pl.BlockSpec((None, tm, RET_W), lambda b, i: (b, i, 0)),
            pl.BlockSpec((None, tm, ATT_W), lambda b, i: (b, i, 0)),
            modrow(2), modrow(3), modrow(4), modrow(5),
            pl.BlockSpec((1, D), const),
            pl.BlockSpec((1, D), const),
            pl.BlockSpec(wo_r.shape, const, pipeline_mode=once),
            pl.BlockSpec(wo_a.shape, const, pipeline_mode=once),
            pl.BlockSpec(w1.shape, const, pipeline_mode=once),
            pl.BlockSpec(w2.shape, const, pipeline_mode=once),
        ],
        out_specs=pl.BlockSpec((None, tm, D), lambda b, i: (b, i, 0)),
        out_shape=jax.ShapeDtypeStruct((B, L, D), F32),
        compiler_params=pltpu.CompilerParams(dimension_semantics=("arbitrary", "arbitrary"),
                                             vmem_limit_bytes=VMEM_LIMIT),
        name="out_ffn",
    )(x, ret, att, mod3, mod3, mod3, mod3, norm2_g, final_g, wo_r, wo_a, w1, w2)


def _freqs(n_pairs):
    return ROPE_BASE ** (-jnp.arange(n_pairs, dtype=F32) / n_pairs)


def _rope_tables(L):
    t = jnp.arange(L, dtype=F32)
    ang = t[:, None] * _freqs(RET_DK // 2)
    cr = jnp.concatenate([jnp.cos(ang), jnp.cos(ang)], axis=-1)
    sr = jnp.concatenate([-jnp.sin(ang), jnp.sin(ang)], axis=-1)
    rows = L // GRID_W
    row = jnp.repeat(jnp.arange(rows, dtype=F32), GRID_W)
    col = jnp.tile(jnp.arange(GRID_W, dtype=F32), rows)
    af = _freqs(ATT_HD // 4)
    aang = jnp.concatenate([row[:, None] * af, col[:, None] * af], axis=-1)
    c, s = jnp.cos(aang), jnp.sin(aang)
    ca = jnp.concatenate([c, c, c, c], axis=-1)
    sa = jnp.concatenate([-s, s, -s, s], axis=-1)
    return cr, sr, ca, sa


def kernel(x, c, ctx, c_ctx, w_mod, b_mod, norm1_g, norm2_g, w_in, w_out, ret_log_rate, ret_gn_g,
           q_norm_g, k_norm_g, w_ff1, w_ff2, final_norm_g):
    B, L, D = x.shape
    assert w_mod.shape[0] == 1, "single-layer configuration"
    assert B + 1 <= MOD_ROWS and L % TM_PROJ == 0 and L % TQ_ATT == 0 and L % RET_CHUNK == 0

    c_rows = jnp.zeros((MOD_ROWS, D), F32).at[:B].set(c).at[B].set(c_ctx)
    mod = _mod_call(c_rows, w_mod[0], b_mod[0][None, :])
    mod3 = mod.reshape(MOD_ROWS, 1, 6 * D)

    w_in_b = w_in[0].astype(BF16)
    w_rkv = w_in_b[:, RK_OFF:RG_OFF]
    w_akv = w_in_b[:, AK_OFF:D_IN]
    wo = w_out[0].astype(BF16)
    w1 = w_ff1[0].astype(BF16)
    w2 = w_ff2[0].astype(BF16)
    n1 = norm1_g[0][None, :]
    n2 = norm2_g[0][None, :]
    nf = final_norm_g[None, :]
    gq2 = jnp.tile(q_norm_g[0], LANES // ATT_HD)[None, :]
    gk2 = jnp.tile(k_norm_g[0], LANES // ATT_HD)[None, :]
    gn = ret_gn_g[0][None, :]
    rate = jnp.broadcast_to(ret_log_rate[0].astype(F32)[:, :, None, None],
                            (2, RET_HEADS, 1, RET_CHUNK))
    cr, sr, ca, sa = _rope_tables(L)

    ctx_ret, ctx_akv = _ctx_proj_call(ctx, mod3, n1, w_rkv, w_akv, gk2)
    ret_in, aq, akv = _in_proj_call(x, mod3, n1, w_in_b, cr, sr, ca, sa, gq2, gk2)
    ret = _retention_call(ret_in, ctx_ret, rate, gn)
    att = _attention_call(aq, akv, ctx_akv)
    return _out_ffn_call(x, ret, att, mod3, n2, nf, wo[:RET_W], wo[RET_W:], w1, w2)
```

```python
import math

import jax
import jax.numpy as jnp
from jax import lax
from jax.experimental import pallas as pl
from jax.experimental.pallas import tpu as pltpu

D_MODEL = 1024
CTX_LEN = 256
GRID_W = 64

RET_HEADS = 4
RET_DK = 128
RET_DV = 128
RET_W = RET_HEADS * RET_DV
RET_SCALE = RET_DK ** -0.5

ATT_HEADS = 8
ATT_KV_HEADS = 2
ATT_GROUP = ATT_HEADS // ATT_KV_HEADS
ATT_HD = 64
ATT_W = ATT_HEADS * ATT_HD
ATT_SCALE = ATT_HD ** -0.5

MIX_W = RET_W + ATT_W
D_FF = 4 * D_MODEL
ROPE_BASE = 10000.0
EPS = 1e-6
LOG2E = math.log2(math.e)

RQ_OFF = 0
RK_OFF = RQ_OFF + RET_HEADS * RET_DK
RV_OFF = RK_OFF + RET_HEADS * RET_DK
RG_OFF = RV_OFF + RET_W
AQ_OFF = RG_OFF + RET_W
AK_OFF = AQ_OFF + ATT_W
AV_OFF = AK_OFF + ATT_KV_HEADS * ATT_HD
D_IN = AV_OFF + ATT_KV_HEADS * ATT_HD

LANES = 128
MXU_DIM = 256
VMEM_LIMIT = 56 * 1024 * 1024

MOD_ROWS = 16
TM_PROJ = 512
TQ_ATT = 256
ATT_KEYS = 128
RET_CHUNK = 256
FF_CHUNK = 1024

F32 = jnp.float32
BF16 = jnp.bfloat16


def _dot(a, b):
    return jnp.dot(a, b, preferred_element_type=F32)


def _dot_nt(a, b):
    return lax.dot_general(a, b, (((1,), (1,)), ((), ())), preferred_element_type=F32)


def _dot_tn(a, b):
    return lax.dot_general(a, b, (((0,), (0,)), ((), ())), preferred_element_type=F32)


def _silu(x):
    return x * jax.nn.sigmoid(x)


def _head_mean_matrix(n):
    r = lax.broadcasted_iota(jnp.int32, (n, n), 0) // ATT_HD
    c = lax.broadcasted_iota(jnp.int32, (n, n), 1) // ATT_HD
    return jnp.where(r == c, 1.0 / ATT_HD, 0.0).astype(BF16)


def _head_rms_scale(blk, bd):
    ms = _dot((blk * blk).astype(BF16), bd)
    return lax.rsqrt(ms + EPS)


def _rope_ret(blk, c2, s2):
    return blk * c2 + pltpu.roll(blk, RET_DK // 2, 1) * s2


def _rope_att(blk, c2, s2, first_half):
    up = pltpu.roll(blk, ATT_HD // 2, 1)
    dn = pltpu.roll(blk, LANES - ATT_HD // 2, 1)
    return blk * c2 + jnp.where(first_half, dn, up) * s2


def _mod_kernel(c_ref, w_ref, b_ref, o_ref):
    a = _silu(c_ref[...])
    o_ref[...] = jnp.dot(a, w_ref[...], preferred_element_type=F32,
                         precision=lax.Precision.HIGHEST) + b_ref[...]


def _mod_call(c_rows, w_mod, b_mod):
    n = w_mod.shape[1]
    bn = 1024
    return pl.pallas_call(
        _mod_kernel,
        grid=(n // bn,),
        in_specs=[
            pl.BlockSpec((MOD_ROWS, D_MODEL), lambda j: (0, 0)),
            pl.BlockSpec((D_MODEL, bn), lambda j: (0, j)),
            pl.BlockSpec((1, bn), lambda j: (0, j)),
        ],
        out_specs=pl.BlockSpec((MOD_ROWS, bn), lambda j: (0, j)),
        out_shape=jax.ShapeDtypeStruct((MOD_ROWS, n), F32),
        compiler_params=pltpu.CompilerParams(dimension_semantics=("arbitrary",),
                                             vmem_limit_bytes=VMEM_LIMIT),
        name="mod",
    )(c_rows, w_mod, b_mod)


def _norm_modulate(x, g, sh, sc):
    ms = jnp.mean(x * x, axis=-1, keepdims=True)
    y = x * lax.rsqrt(ms + EPS) * g
    return y * (1.0 + sc) + sh


def _ctx_proj_kernel(x_ref, sh_ref, sc_ref, g_ref, wr_ref, wa_ref, gk_ref, ret_ref, akv_ref):
    h = _norm_modulate(x_ref[...], g_ref[...], sh_ref[...], sc_ref[...]).astype(BF16)
    pr = _dot(h, wr_ref[...])
    ret_ref[:, :RET_W] = (pr[:, :RET_W] * RET_SCALE).astype(BF16)
    ret_ref[:, RET_W:] = pr[:, RET_W:].astype(BF16)
    pa = _dot(h, wa_ref[...])
    ak = pa[:, :LANES]
    r = _head_rms_scale(ak, _head_mean_matrix(LANES))
    akv_ref[:, :LANES] = (ak * r * gk_ref[...]).astype(BF16)
    akv_ref[:, LANES:] = pa[:, LANES:].astype(BF16)


def _ctx_proj_call(ctx, mod3, norm1_g, w_rkv, w_akv, gk2):
    B, Lc, D = ctx.shape
    ctx_row = B
    const = lambda b: (0, 0)
    return pl.pallas_call(
        _ctx_proj_kernel,
        grid=(B,),
        in_specs=[
            pl.BlockSpec((None, Lc, D), lambda b: (b, 0, 0)),
            pl.BlockSpec((None, 1, D), lambda b: (ctx_row, 0, 0)),
            pl.BlockSpec((None, 1, D), lambda b: (ctx_row, 0, 1)),
            pl.BlockSpec((1, D), const),
            pl.BlockSpec(w_rkv.shape, const),
            pl.BlockSpec(w_akv.shape, const),
            pl.BlockSpec((1, LANES), const),
        ],
        out_specs=[
            pl.BlockSpec((None, Lc, 2 * RET_W), lambda b: (b, 0, 0)),
            pl.BlockSpec((None, Lc, 2 * LANES), lambda b: (b, 0, 0)),
        ],
        out_shape=[
            jax.ShapeDtypeStruct((B, Lc, 2 * RET_W), BF16),
            jax.ShapeDtypeStruct((B, Lc, 2 * LANES), BF16),
        ],
        compiler_params=pltpu.CompilerParams(dimension_semantics=("arbitrary",),
                                             vmem_limit_bytes=VMEM_LIMIT),
        name="ctx_proj",
    )(ctx, mod3, mod3, norm1_g, w_rkv, w_akv, gk2)


def _in_proj_kernel(x_ref, sh_ref, sc_ref, g_ref, w_ref, cr_ref, sr_ref, ca_ref, sa_ref,
                    gq_ref, gk_ref, ret_ref, aq_ref, akv_ref):
    tm = x_ref.shape[0]
    h = _norm_modulate(x_ref[...], g_ref[...], sh_ref[...], sc_ref[...]).astype(BF16)
    cr, sr = cr_ref[...], sr_ref[...]
    ca, sa = ca_ref[...], sa_ref[...]
    lane = lax.broadcasted_iota(jnp.int32, (tm, LANES), 1)
    first_half = (lane & (ATT_HD // 2)) == 0

    pq = _dot(h, w_ref[:, RQ_OFF:RK_OFF])
    for hh in range(RET_HEADS):
        sl = slice(hh * RET_DK, (hh + 1) * RET_DK)
        ret_ref[:, RQ_OFF + hh * RET_DK:RQ_OFF + (hh + 1) * RET_DK] = _rope_ret(pq[:, sl], cr, sr).astype(BF16)
    pk = _dot(h, w_ref[:, RK_OFF:RV_OFF])
    for hh in range(RET_HEADS):
        sl = slice(hh * RET_DK, (hh + 1) * RET_DK)
        ret_ref[:, RK_OFF + hh * RET_DK:RK_OFF + (hh + 1) * RET_DK] = (
            _rope_ret(pk[:, sl], cr, sr) * RET_SCALE).astype(BF16)
    ret_ref[:, RV_OFF:AQ_OFF] = _dot(h, w_ref[:, RV_OFF:AQ_OFF]).astype(BF16)

    bd = _head_mean_matrix(MXU_DIM)
    gq, gk = gq_ref[...], gk_ref[...]
    pa = _dot(h, w_ref[:, AQ_OFF:AK_OFF])
    for j in range(ATT_W // MXU_DIM):
        blk = pa[:, j * MXU_DIM:(j + 1) * MXU_DIM]
        r = _head_rms_scale(blk, bd) * (ATT_SCALE * LOG2E)
        for i in range(MXU_DIM // LANES):
            sl = slice(i * LANES, (i + 1) * LANES)
            o = _rope_att(blk[:, sl] * gq, ca, sa, first_half) * r[:, sl]
            aq_ref[:, j * MXU_DIM + i * LANES:j * MXU_DIM + (i + 1) * LANES] = o.astype(BF16)

    pkv = _dot(h, w_ref[:, AK_OFF:D_IN])
    ak = pkv[:, :LANES]
    rk = _head_rms_scale(ak, bd[:LANES, :LANES])
    akv_ref[:, :LANES] = (_rope_att(ak * gk, ca, sa, first_half) * rk).astype(BF16)
    akv_ref[:, LANES:] = pkv[:, LANES:].astype(BF16)


def _in_proj_call(x, mod3, norm1_g, w_in, cr, sr, ca, sa, gq2, gk2):
    B, L, D = x.shape
    tm = TM_PROJ
    const = lambda b, i: (0, 0)
    tok = lambda b, i: (i, 0)
    return pl.pallas_call(
        _in_proj_kernel,
        grid=(B, L // tm),
        in_specs=[
            pl.BlockSpec((None, tm, D), lambda b, i: (b, i, 0)),
            pl.BlockSpec((None, 1, D), lambda b, i: (b, 0, 0)),
            pl.BlockSpec((None, 1, D), lambda b, i: (b, 0, 1)),
            pl.BlockSpec((1, D), const),
            pl.BlockSpec(w_in.shape, const),
            pl.BlockSpec((tm, LANES), tok),
            pl.BlockSpec((tm, LANES), tok),
            pl.BlockSpec((tm, LANES), tok),
            pl.BlockSpec((tm, LANES), tok),
            pl.BlockSpec((1, LANES), const),
            pl.BlockSpec((1, LANES), const),
        ],
        out_specs=[
            pl.BlockSpec((None, tm, AQ_OFF), lambda b, i: (b, i, 0)),
            pl.BlockSpec((None, tm, ATT_W), lambda b, i: (b, i, 0)),
            pl.BlockSpec((None, tm, 2 * LANES), lambda b, i: (b, i, 0)),
        ],
        out_shape=[
            jax.ShapeDtypeStruct((B, L, AQ_OFF), BF16),
            jax.ShapeDtypeStruct((B, L, ATT_W), BF16),
            jax.ShapeDtypeStruct((B, L, 2 * LANES), BF16),
        ],
        compiler_params=pltpu.CompilerParams(dimension_semantics=("arbitrary", "arbitrary"),
                                             vmem_limit_bytes=VMEM_LIMIT),
        name="in_proj",
    )(x, mod3, mod3, norm1_g, w_in, cr, sr, ca, sa, gq2, gk2)


def _retention_kernel(q_ref, k_ref, v_ref, g_ref, kc_ref, vc_ref, rate_ref, gn_ref, o_ref, sb_ref):
    L = q_ref.shape[0]
    Lc = kc_ref.shape[0]
    C = RET_CHUNK
    T = L // C

    lg = jnp.log1p(-jnp.exp(rate_ref[...]))
    lgf, lgb = lg[0], lg[1]
    lgf1, lgb1 = lgf[:, :RET_DK], lgb[:, :RET_DK]

    ri = lax.broadcasted_iota(jnp.int32, (C, C), 0)
    ci = lax.broadcasted_iota(jnp.int32, (C, C), 1)
    rel = (ri - ci).astype(F32)
    decay = (jnp.where(rel >= 0, jnp.exp(lgf * jnp.maximum(rel, 0.0)), 0.0)
             + jnp.where(rel <= 0, jnp.exp(lgb * jnp.maximum(-rel, 0.0)), 0.0))

    pos = lax.broadcasted_iota(jnp.int32, (C, RET_DK), 0).astype(F32)
    q_dec_f = jnp.exp(lgf1 * (pos + 1.0))
    q_dec_b = jnp.exp(lgb1 * (C - pos))
    k_dec_f = jnp.exp(lgf1 * (C - 1.0 - pos))
    k_dec_b = jnp.exp(lgb1 * pos)
    c_dec_f = jnp.exp(lgf1 * C)
    c_dec_b = jnp.exp(lgb1 * C)

    cpos = lax.broadcasted_iota(jnp.int32, (Lc, RET_DK), 0).astype(F32)
    kc = kc_ref[...].astype(F32)
    vc = vc_ref[...]
    s_f0 = _dot_tn((kc * jnp.exp(lgf1 * (Lc - 1.0 - cpos))).astype(BF16), vc)
    s_b0 = _dot_tn((kc * jnp.exp(lgb1 * cpos)).astype(BF16), vc)

    def bwd_body(i, s_b):
        t = T - 1 - i
        sb_ref[t] = s_b
        rows = pl.ds(pl.multiple_of(t * C, C), C)
        kd = (k_ref[rows, :].astype(F32) * k_dec_b).astype(BF16)
        return s_b * c_dec_b + _dot_tn(kd, v_ref[rows, :])

    lax.fori_loop(0, T, bwd_body, s_b0)

    gn = gn_ref[...]

    def fwd_body(t, s_f):
        rows = pl.ds(pl.multiple_of(t * C, C), C)
        q = q_ref[rows, :]
        k = k_ref[rows, :]
        v = v_ref[rows, :]
        scores = _dot_nt(q, k) * decay
        qf = q.astype(F32)
        q_cat = jnp.concatenate([(qf * q_dec_f).astype(BF16), (qf * q_dec_b).astype(BF16)], axis=1)
        s_cat = jnp.concatenate([s_f, sb_ref[t]], axis=0).astype(BF16)
        o = _dot(scores.astype(BF16), v) + _dot(q_cat, s_cat)
        mu = jnp.mean(o, axis=-1, keepdims=True)
        d = o - mu
        var = jnp.mean(d * d, axis=-1, keepdims=True)
        on = d * lax.rsqrt(var + EPS) * gn
        o_ref[rows, :] = (on * _silu(g_ref[rows, :].astype(F32))).astype(o_ref.dtype)
        kd = (k.astype(F32) * k_dec_f).astype(BF16)
        return s_f * c_dec_f + _dot_tn(kd, v)

    lax.fori_loop(0, T, fwd_body, s_f0)


def _retention_call(ret_in, ctx_ret, rate, gn_g):
    B, L, _ = ret_in.shape
    Lc = ctx_ret.shape[1]
    H = RET_HEADS
    T = L // RET_CHUNK

    def col(off):
        return lambda b, h: (b, 0, off + h)

    return pl.pallas_call(
        _retention_kernel,
        grid=(B, H),
        in_specs=[
            pl.BlockSpec((None, L, RET_DK), col(RQ_OFF // RET_DK)),
            pl.BlockSpec((None, L, RET_DK), col(RK_OFF // RET_DK)),
            pl.BlockSpec((None, L, RET_DV), col(RV_OFF // RET_DV)),
            pl.BlockSpec((None, L, RET_DV), col(RG_OFF // RET_DV)),
            pl.BlockSpec((None, Lc, RET_DK), col(0)),
            pl.BlockSpec((None, Lc, RET_DV), col(RET_HEADS)),
            pl.BlockSpec((2, None, 1, RET_CHUNK), lambda b, h: (0, h, 0, 0)),
            pl.BlockSpec((1, RET_DV), lambda b, h: (0, h)),
        ],
        out_specs=pl.BlockSpec((None, L, RET_DV), col(0)),
        out_shape=jax.ShapeDtypeStruct((B, L, RET_W), BF16),
        scratch_shapes=[pltpu.VMEM((T, RET_DK, RET_DV), F32)],
        compiler_params=pltpu.CompilerParams(dimension_semantics=("arbitrary", "arbitrary"),
                                             vmem_limit_bytes=VMEM_LIMIT),
        name="retention",
    )(ret_in, ret_in, ret_in, ret_in, ctx_ret, ctx_ret, rate, gn_g)


def _attention_kernel(q_ref, k_ref, v_ref, kc_ref, vc_ref, o_ref, ks_ref, vt_ref):
    kv = pl.program_id(1)
    Lc = kc_ref.shape[0]
    L = k_ref.shape[0]
    S = Lc + L
    tq = q_ref.shape[0]

    @pl.when(pl.program_id(2) == 0)
    def _():
        def own_twice(x):
            xf = x.astype(F32)
            lane = lax.broadcasted_iota(jnp.int32, xf.shape, 1)
            own = (lane // ATT_HD) == kv
            return jnp.where(own, xf, pltpu.roll(xf, ATT_HD, 1))

        ks_ref[0:Lc, :] = own_twice(kc_ref[...]).astype(BF16)
        ks_ref[Lc:S, :] = own_twice(k_ref[...]).astype(BF16)
        vt_ref[:, 0:Lc] = own_twice(vc_ref[...]).T[:ATT_HD].astype(BF16)
        vt_ref[:, Lc:S] = own_twice(v_ref[...]).T[:ATT_HD].astype(BF16)

    q_t = q_ref[...].astype(F32).T.astype(BF16)
    pad = jnp.zeros((LANES - ATT_HD, tq), BF16)
    wq = [jnp.concatenate([q_t[g * ATT_HD:(g + 1) * ATT_HD], pad], axis=0) for g in range(ATT_GROUP)]

    def scores(c):
        k_c = ks_ref[c * ATT_KEYS:(c + 1) * ATT_KEYS, :]
        return [_dot(k_c, wq[g]) for g in range(ATT_GROUP)]

    m = [None] * ATT_GROUP
    l = [None] * ATT_GROUP
    acc = [None] * ATT_GROUP
    n_tiles = S // ATT_KEYS
    s_next = scores(0)
    for c in range(n_tiles):
        s_cur = s_next
        if c + 1 < n_tiles:
            s_next = scores(c + 1)
        v_c = vt_ref[:, c * ATT_KEYS:(c + 1) * ATT_KEYS]
        for g in range(ATT_GROUP):
            s = s_cur[g]
            mt = jnp.max(s, axis=0, keepdims=True)
            if c == 0:
                m[g] = mt
                p = jnp.exp2(s - mt)
                l[g] = jnp.sum(p, axis=0, keepdims=True)
                acc[g] = _dot(v_c, p.astype(BF16))
            else:
                m_new = jnp.maximum(m[g], mt)
                alpha = jnp.exp2(m[g] - m_new)
                p = jnp.exp2(s - m_new)
                l[g] = l[g] * alpha + jnp.sum(p, axis=0, keepdims=True)
                acc[g] = acc[g] * alpha + _dot(v_c, p.astype(BF16))
                m[g] = m_new
    out_t = jnp.concatenate([acc[g] / l[g] for g in range(ATT_GROUP)], axis=0)
    o_ref[...] = out_t.T.astype(o_ref.dtype)


def _attention_call(aq, akv, ctx_akv):
    B, L, _ = aq.shape
    Lc = ctx_akv.shape[1]
    tq = TQ_ATT
    S = Lc + L
    assert S % ATT_KEYS == 0
    return pl.pallas_call(
        _attention_kernel,
        grid=(B, ATT_KV_HEADS, L // tq),
        in_specs=[
            pl.BlockSpec((None, tq, ATT_GROUP * ATT_HD), lambda b, kv, i: (b, i, kv)),
            pl.BlockSpec((None, L, LANES), lambda b, kv, i: (b, 0, 0)),
            pl.BlockSpec((None, L, LANES), lambda b, kv, i: (b, 0, 1)),
            pl.BlockSpec((None, Lc, LANES), lambda b, kv, i: (b, 0, 0)),
            pl.BlockSpec((None, Lc, LANES), lambda b, kv, i: (b, 0, 1)),
        ],
        out_specs=pl.BlockSpec((None, tq, ATT_GROUP * ATT_HD), lambda b, kv, i: (b, i, kv)),
        out_shape=jax.ShapeDtypeStruct((B, L, ATT_W), BF16),
        scratch_shapes=[pltpu.VMEM((S, LANES), BF16), pltpu.VMEM((ATT_HD, S), BF16)],
        compiler_params=pltpu.CompilerParams(
            dimension_semantics=("arbitrary", "arbitrary", "arbitrary"),
            vmem_limit_bytes=VMEM_LIMIT),
        name="attention",
    )(aq, akv, akv, ctx_akv, ctx_akv)


def _out_ffn_kernel(x_ref, ret_ref, att_ref, g1_ref, sh2_ref, sc2_ref, g2_ref, n2_ref, nf_ref,
                    wo_r_ref, wo_a_ref, w1_ref, w2_ref, o_ref):
    mix = _dot(ret_ref[...], wo_r_ref[...]) + _dot(att_ref[...], wo_a_ref[...])
    x1 = x_ref[...] + g1_ref[...] * mix
    h2 = _norm_modulate(x1, n2_ref[...], sh2_ref[...], sc2_ref[...]).astype(BF16)
    ff = jnp.zeros(x1.shape, F32)
    for j in range(D_FF // FF_CHUNK):
        a = jnp.maximum(_dot(h2, w1_ref[:, j * FF_CHUNK:(j + 1) * FF_CHUNK]), 0.0)
        ff = ff + _dot((a * a).astype(BF16), w2_ref[j * FF_CHUNK:(j + 1) * FF_CHUNK, :])
    x2 = x1 + g2_ref[...] * ff
    ms = jnp.mean(x2 * x2, axis=-1, keepdims=True)
    o_ref[...] = x2 * lax.rsqrt(ms + EPS) * nf_ref[...]


def _out_ffn_call(x, ret, att, mod3, norm2_g, final_g, wo_r, wo_a, w1, w2):
    B, L, D = x.shape
    tm = TM_PROJ
    const = lambda b, i: (0, 0)
    once = pl.Buffered(1)

    def modrow(j):
        return pl.BlockSpec((None, 1, D), lambda b, i: (b, 0, j))

    return pl.pallas_call(
        _out_ffn_kernel,
        grid=(B, L // tm),
        in_specs=[
            pl.BlockSpec((None, tm, D), lambda b, i: (b, i, 0)),
            pl.BlockSpec((None, tm, RET_W), lambda b, i: (b, i, 0)),
            pl.BlockSpec((None, tm, ATT_W), lambda b, i: (b, i, 0)),
            modrow(2), modrow(3), modrow(4), modrow(5),
            pl.BlockSpec((1, D), const),
            pl.BlockSpec((1, D), const),
            pl.BlockSpec(wo_r.shape, const, pipeline_mode=once),
            pl.BlockSpec(wo_a.shape, const, pipeline_mode=once),
            pl.BlockSpec(w1.shape, const, pipeline_mode=once),
            pl.BlockSpec(w2.shape, const, pipeline_mode=once),
        ],
        out_specs=pl.BlockSpec((None, tm, D), lambda b, i: (b, i, 0)),
        out_shape=jax.ShapeDtypeStruct((B, L, D), F32),
        compiler_params=pltpu.CompilerParams(dimension_semantics=("arbitrary", "arbitrary"),
                                             vmem_limit_bytes=VMEM_LIMIT),
        name="out_ffn",
    )(x, ret, att, mod3, mod3, mod3, mod3, norm2_g, final_g, wo_r, wo_a, w1, w2)


def _freqs(n_pairs):
    return ROPE_BASE ** (-jnp.arange(n_pairs, dtype=F32) / n_pairs)


def _rope_tables(L):
    t = jnp.arange(L, dtype=F32)
    ang = t[:, None] * _freqs(RET_DK // 2)
    cr = jnp.concatenate([jnp.cos(ang), jnp.cos(ang)], axis=-1)
    sr = jnp.concatenate([-jnp.sin(ang), jnp.sin(ang)], axis=-1)
    rows = L // GRID_W
    row = jnp.repeat(jnp.arange(rows, dtype=F32), GRID_W)
    col = jnp.tile(jnp.arange(GRID_W, dtype=F32), rows)
    af = _freqs(ATT_HD // 4)
    aang = jnp.concatenate([row[:, None] * af, col[:, None] * af], axis=-1)
    c, s = jnp.cos(aang), jnp.sin(aang)
    ca = jnp.concatenate([c, c, c, c], axis=-1)
    sa = jnp.concatenate([-s, s, -s, s], axis=-1)
    return cr, sr, ca, sa


def kernel(x, c, ctx, c_ctx, w_mod, b_mod, norm1_g, norm2_g, w_in, w_out, ret_log_rate, ret_gn_g,
           q_norm_g, k_norm_g, w_ff1, w_ff2, final_norm_g):
    B, L, D = x.shape
    assert w_mod.shape[0] == 1, "single-layer configuration"
    assert B + 1 <= MOD_ROWS and L % TM_PROJ == 0 and L % TQ_ATT == 0 and L % RET_CHUNK == 0

    c_rows = jnp.zeros((MOD_ROWS, D), F32).at[:B].set(c).at[B].set(c_ctx)
    mod = _mod_call(c_rows, w_mod[0], b_mod[0][None, :])
    mod3 = mod.reshape(MOD_ROWS, 1, 6 * D)

    w_in_b = w_in[0].astype(BF16)
    w_rkv = w_in_b[:, RK_OFF:RG_OFF]
    w_akv = w_in_b[:, AK_OFF:D_IN]
    wo = w_out[0].astype(BF16)
    w1 = w_ff1[0].astype(BF16)
    w2 = w_ff2[0].astype(BF16)
    n1 = norm1_g[0][None, :]
    n2 = norm2_g[0][None, :]
    nf = final_norm_g[None, :]
    gq2 = jnp.tile(q_norm_g[0], LANES // ATT_HD)[None, :]
    gk2 = jnp.tile(k_norm_g[0], LANES // ATT_HD)[None, :]
    gn = ret_gn_g[0][None, :]
    rate = jnp.broadcast_to(ret_log_rate[0].astype(F32)[:, :, None, None],
                            (2, RET_HEADS, 1, RET_CHUNK))
    cr, sr, ca, sa = _rope_tables(L)

    ctx_ret, ctx_akv = _ctx_proj_call(ctx, mod3, n1, w_rkv, w_akv, gk2)
    ret_in, aq, akv = _in_proj_call(x, mod3, n1, w_in_b, cr, sr, ca, sa, gq2, gk2)
    ret = _retention_call(ret_in, ctx_ret, rate, gn)
    att = _attention_call(aq, akv, ctx_akv)
    return _out_ffn_call(x, ret, att, mod3, n2, nf, wo[:RET_W], wo[RET_W:], w1, w2)
```

```python
import math

import jax
import jax.numpy as jnp
from jax import lax
from jax.experimental import pallas as pl
from jax.experimental.pallas import tpu as pltpu

D_MODEL = 1024
CTX_LEN = 256
GRID_W = 64

RET_HEADS = 4
RET_DK = 128
RET_DV = 128
RET_W = RET_HEADS * RET_DV
RET_SCALE = RET_DK ** -0.5

ATT_HEADS = 8
ATT_KV_HEADS = 2
ATT_GROUP = ATT_HEADS // ATT_KV_HEADS
ATT_HD = 64
ATT_W = ATT_HEADS * ATT_HD
ATT_SCALE = ATT_HD ** -0.5

MIX_W = RET_W + ATT_W
D_FF = 4 * D_MODEL
ROPE_BASE = 10000.0
EPS = 1e-6
LOG2E = math.log2(math.e)

RQ_OFF = 0
RK_OFF = RQ_OFF + RET_HEADS * RET_DK
RV_OFF = RK_OFF + RET_HEADS * RET_DK
RG_OFF = RV_OFF + RET_W
AQ_OFF = RG_OFF + RET_W
AK_OFF = AQ_OFF + ATT_W
AV_OFF = AK_OFF + ATT_KV_HEADS * ATT_HD
D_IN = AV_OFF + ATT_KV_HEADS * ATT_HD

LANES = 128
MXU_DIM = 256
VMEM_LIMIT = 56 * 1024 * 1024

MOD_ROWS = 16
TM_PROJ = 512
TQ_ATT = 256
ATT_KEYS = 128
RET_CHUNK = 256
FF_CHUNK = 1024

F32 = jnp.float32
BF16 = jnp.bfloat16


def _dot(a, b):
    return jnp.dot(a, b, preferred_element_type=F32)


def _dot_nt(a, b):
    return lax.dot_general(a, b, (((1,), (1,)), ((), ())), preferred_element_type=F32)


def _dot_tn(a, b):
    return lax.dot_general(a, b, (((0,), (0,)), ((), ())), preferred_element_type=F32)


def _silu(x):
    return x * jax.nn.sigmoid(x)


def _head_mean_matrix(n):
    r = lax.broadcasted_iota(jnp.int32, (n, n), 0) // ATT_HD
    c = lax.broadcasted_iota(jnp.int32, (n, n), 1) // ATT_HD
    return jnp.where(r == c, 1.0 / ATT_HD, 0.0).astype(BF16)


def _head_rms_scale(blk, bd):
    ms = _dot((blk * blk).astype(BF16), bd)
    return lax.rsqrt(ms + EPS)


def _rope_ret(blk, c2, s2):
    return blk * c2 + pltpu.roll(blk, RET_DK // 2, 1) * s2


def _rope_att(blk, c2, s2, first_half):
    up = pltpu.roll(blk, ATT_HD // 2, 1)
    dn = pltpu.roll(blk, LANES - ATT_HD // 2, 1)
    return blk * c2 + jnp.where(first_half, dn, up) * s2


def _mod_kernel(c_ref, w_ref, b_ref, o_ref):
    a = _silu(c_ref[...])
    o_ref[...] = jnp.dot(a, w_ref[...], preferred_element_type=F32,
                         precision=lax.Precision.HIGHEST) + b_ref[...]


def _mod_call(c_rows, w_mod, b_mod):
    n = w_mod.shape[1]
    bn = 1024
    return pl.pallas_call(
        _mod_kernel,
        grid=(n // bn,),
        in_specs=[
            pl.BlockSpec((MOD_ROWS, D_MODEL), lambda j: (0, 0)),
            pl.BlockSpec((D_MODEL, bn), lambda j: (0, j)),
            pl.BlockSpec((1, bn), lambda j: (0, j)),
        ],
        out_specs=pl.BlockSpec((MOD_ROWS, bn), lambda j: (0, j)),
        out_shape=jax.ShapeDtypeStruct((MOD_ROWS, n), F32),
        compiler_params=pltpu.CompilerParams(dimension_semantics=("arbitrary",),
                                             vmem_limit_bytes=VMEM_LIMIT),
        name="mod",
    )(c_rows, w_mod, b_mod)


def _norm_modulate(x, g, sh, sc):
    ms = jnp.mean(x * x, axis=-1, keepdims=True)
    y = x * lax.rsqrt(ms + EPS) * g
    return y * (1.0 + sc) + sh


def _ctx_proj_kernel(x_ref, sh_ref, sc_ref, g_ref, wr_ref, wa_ref, gk_ref, ret_ref, akv_ref):
    h = _norm_modulate(x_ref[...], g_ref[...], sh_ref[...], sc_ref[...]).astype(BF16)
    pr = _dot(h, wr_ref[...])
    ret_ref[:, :RET_W] = (pr[:, :RET_W] * RET_SCALE).astype(BF16)
    ret_ref[:, RET_W:] = pr[:, RET_W:].astype(BF16)
    pa = _dot(h, wa_ref[...])
    ak = pa[:, :LANES]
    r = _head_rms_scale(ak, _head_mean_matrix(LANES))
    akv_ref[:, :LANES] = (ak * r * gk_ref[...]).astype(BF16)
    akv_ref[:, LANES:] = pa[:, LANES:].astype(BF16)


def _ctx_proj_call(ctx, mod3, norm1_g, w_rkv, w_akv, gk2):
    B, Lc, D = ctx.shape
    ctx_row = B
    const = lambda b: (0, 0)
    return pl.pallas_call(
        _ctx_proj_kernel,
        grid=(B,),
        in_specs=[
            pl.BlockSpec((None, Lc, D), lambda b: (b, 0, 0)),
            pl.BlockSpec((None, 1, D), lambda b: (ctx_row, 0, 0)),
            pl.BlockSpec((None, 1, D), lambda b: (ctx_row, 0, 1)),
            pl.BlockSpec((1, D), const),
            pl.BlockSpec(w_rkv.shape, const),
            pl.BlockSpec(w_akv.shape, const),
            pl.BlockSpec((1, LANES), const),
        ],
        out_specs=[
            pl.BlockSpec((None, Lc, 2 * RET_W), lambda b: (b, 0, 0)),
            pl.BlockSpec((None, Lc, 2 * LANES), lambda b: (b, 0, 0)),
        ],
        out_shape=[
            jax.ShapeDtypeStruct((B, Lc, 2 * RET_W), BF16),
            jax.ShapeDtypeStruct((B, Lc, 2 * LANES), BF16),
        ],
        compiler_params=pltpu.CompilerParams(dimension_semantics=("arbitrary",),
                                             vmem_limit_bytes=VMEM_LIMIT),
        name="ctx_proj",
    )(ctx, mod3, mod3, norm1_g, w_rkv, w_akv, gk2)


def _in_proj_kernel(x_ref, sh_ref, sc_ref, g_ref, w_ref, cr_ref, sr_ref, ca_ref, sa_ref,
                    gq_ref, gk_ref, ret_ref, aq_ref, akv_ref):
    tm = x_ref.shape[0]
    h = _norm_modulate(x_ref[...], g_ref[...], sh_ref[...], sc_ref[...]).astype(BF16)
    cr, sr = cr_ref[...], sr_ref[...]
    ca, sa = ca_ref[...], sa_ref[...]
    lane = lax.broadcasted_iota(jnp.int32, (tm, LANES), 1)
    first_half = (lane & (ATT_HD // 2)) == 0

    pq = _dot(h, w_ref[:, RQ_OFF:RK_OFF])
    for hh in range(RET_HEADS):
        sl = slice(hh * RET_DK, (hh + 1) * RET_DK)
        ret_ref[:, RQ_OFF + hh * RET_DK:RQ_OFF + (hh + 1) * RET_DK] = _rope_ret(pq[:, sl], cr, sr).astype(BF16)
    pk = _dot(h, w_ref[:, RK_OFF:RV_OFF])
    for hh in range(RET_HEADS):
        sl = slice(hh * RET_DK, (hh + 1) * RET_DK)
        ret_ref[:, RK_OFF + hh * RET_DK:RK_OFF + (hh + 1) * RET_DK] = (
            _rope_ret(pk[:, sl], cr, sr) * RET_SCALE).astype(BF16)
    ret_ref[:, RV_OFF:AQ_OFF] = _dot(h, w_ref[:, RV_OFF:AQ_OFF]).astype(BF16)

    bd = _head_mean_matrix(MXU_DIM)
    gq, gk = gq_ref[...], gk_ref[...]
    pa = _dot(h, w_ref[:, AQ_OFF:AK_OFF])
    for j in range(ATT_W // MXU_DIM):
        blk = pa[:, j * MXU_DIM:(j + 1) * MXU_DIM]
        r = _head_rms_scale(blk, bd) * (ATT_SCALE * LOG2E)
        for i in range(MXU_DIM // LANES):
            sl = slice(i * LANES, (i + 1) * LANES)
            o = _rope_att(blk[:, sl] * gq, ca, sa, first_half) * r[:, sl]
            aq_ref[:, j * MXU_DIM + i * LANES:j * MXU_DIM + (i + 1) * LANES] = o.astype(BF16)

    pkv = _dot(h, w_ref[:, AK_OFF:D_IN])
    ak = pkv[:, :LANES]
    rk = _head_rms_scale(ak, bd[:LANES, :LANES])
    akv_ref[:, :LANES] = (_rope_att(ak * gk, ca, sa, first_half) * rk).astype(BF16)
    akv_ref[:, LANES:] = pkv[:, LANES:].astype(BF16)


def _in_proj_call(x, mod3, norm1_g, w_in, cr, sr, ca, sa, gq2, gk2):
    B, L, D = x.shape
    tm = TM_PROJ
    const = lambda b, i: (0, 0)
    tok = lambda b, i: (i, 0)
    return pl.pallas_call(
        _in_proj_kernel,
        grid=(B, L // tm),
        in_specs=[
            pl.BlockSpec((None, tm, D), lambda b, i: (b, i, 0)),
            pl.BlockSpec((None, 1, D), lambda b, i: (b, 0, 0)),
            pl.BlockSpec((None, 1, D), lambda b, i: (b, 0, 1)),
            pl.BlockSpec((1, D), const),
            pl.BlockSpec(w_in.shape, const),
            pl.BlockSpec((tm, LANES), tok),
            pl.BlockSpec((tm, LANES), tok),
            pl.BlockSpec((tm, LANES), tok),
            pl.BlockSpec((tm, LANES), tok),
            pl.BlockSpec((1, LANES), const),
            pl.BlockSpec((1, LANES), const),
        ],
        out_specs=[
            pl.BlockSpec((None, tm, AQ_OFF), lambda b, i: (b, i, 0)),
            pl.BlockSpec((None, tm, ATT_W), lambda b, i: (b, i, 0)),
            pl.BlockSpec((None, tm, 2 * LANES), lambda b, i: (b, i, 0)),
        ],
        out_shape=[
            jax.ShapeDtypeStruct((B, L, AQ_OFF), BF16),
            jax.ShapeDtypeStruct((B, L, ATT_W), BF16),
            jax.ShapeDtypeStruct((B, L, 2 * LANES), BF16),
        ],
        compiler_params=pltpu.CompilerParams(dimension_semantics=("arbitrary", "arbitrary"),
                                             vmem_limit_bytes=VMEM_LIMIT),
        name="in_proj",
    )(x, mod3, mod3, norm1_g, w_in, cr, sr, ca, sa, gq2, gk2)


def _retention_kernel(q_ref, k_ref, v_ref, g_ref, kc_ref, vc_ref, rate_ref, gn_ref, o_ref,
                      u_ref, s_ref):
    L = q_ref.shape[0]
    Lc = kc_ref.shape[0]
    C = RET_CHUNK
    T = L // C

    lg = jnp.log1p(-jnp.exp(rate_ref[...]))
    lgf, lgb = lg[0], lg[1]
    lgf1, lgb1 = lgf[:, :RET_DK], lgb[:, :RET_DK]

    ri = lax.broadcasted_iota(jnp.int32, (C, C), 0)
    ci = lax.broadcasted_iota(jnp.int32, (C, C), 1)
    rel = (ri - ci).astype(F32)
    decay = (jnp.where(rel >= 0, jnp.exp(lgf * jnp.maximum(rel, 0.0)), 0.0)
             + jnp.where(rel <= 0, jnp.exp(lgb * jnp.maximum(-rel, 0.0)), 0.0))

    pos = lax.broadcasted_iota(jnp.int32, (C, RET_DK), 0).astype(F32)
    q_dec_f = jnp.exp(lgf1 * (pos + 1.0))
    q_dec_b = jnp.exp(lgb1 * (C - pos))
    k_dec_f = jnp.exp(lgf1 * (C - 1.0 - pos))
    k_dec_b = jnp.exp(lgb1 * pos)
    c_dec_f = jnp.exp(lgf1 * C)
    c_dec_b = jnp.exp(lgb1 * C)

    cpos = lax.broadcasted_iota(jnp.int32, (Lc, RET_DK), 0).astype(F32)
    kc = kc_ref[...].astype(F32)
    vc = vc_ref[...]
    s_f0 = _dot_tn((kc * jnp.exp(lgf1 * (Lc - 1.0 - cpos))).astype(BF16), vc)
    s_b0 = _dot_tn((kc * jnp.exp(lgb1 * cpos)).astype(BF16), vc)

    def rows(t):
        return slice(t * C, (t + 1) * C)

    for t in range(T):
        kf = k_ref[rows(t), :].astype(F32)
        kd = jnp.concatenate([(kf * k_dec_f).astype(BF16), (kf * k_dec_b).astype(BF16)], axis=1)
        u_ref[t] = _dot_tn(kd, v_ref[rows(t), :])

    s_f, s_b = s_f0, s_b0
    for t in range(T):
        tb = T - 1 - t
        s_ref[t, :RET_DK, :] = s_f.astype(BF16)
        s_ref[tb, RET_DK:, :] = s_b.astype(BF16)
        s_f = s_f * c_dec_f + u_ref[t, :RET_DK, :]
        s_b = s_b * c_dec_b + u_ref[tb, RET_DK:, :]

    gn = gn_ref[...]

    def qk(t):
        return _dot_nt(q_ref[rows(t), :], k_ref[rows(t), :])

    sc_next = qk(0)
    for t in range(T):
        sc = sc_next
        if t + 1 < T:
            sc_next = qk(t + 1)
        qf = q_ref[rows(t), :].astype(F32)
        lhs = jnp.concatenate([(sc * decay).astype(BF16), (qf * q_dec_f).astype(BF16),
                               (qf * q_dec_b).astype(BF16)], axis=1)
        rhs = jnp.concatenate([v_ref[rows(t), :], s_ref[t]], axis=0)
        o = _dot(lhs, rhs)
        mu = jnp.mean(o, axis=-1, keepdims=True)
        d = o - mu
        var = jnp.mean(d * d, axis=-1, keepdims=True)
        on = d * lax.rsqrt(var + EPS) * gn
        o_ref[rows(t), :] = (on * _silu(g_ref[rows(t), :].astype(F32))).astype(o_ref.dtype)


def _retention_call(ret_in, ctx_ret, rate, gn_g):
    B, L, _ = ret_in.shape
    Lc = ctx_ret.shape[1]
    H = RET_HEADS
    T = L // RET_CHUNK

    def col(off):
        return lambda b, h: (b, 0, off + h)

    return pl.pallas_call(
        _retention_kernel,
        grid=(B, H),
        in_specs=[
            pl.BlockSpec((None, L, RET_DK), col(RQ_OFF // RET_DK)),
            pl.BlockSpec((None, L, RET_DK), col(RK_OFF // RET_DK)),
            pl.BlockSpec((None, L, RET_DV), col(RV_OFF // RET_DV)),
            pl.BlockSpec((None, L, RET_DV), col(RG_OFF // RET_DV)),
            pl.BlockSpec((None, Lc, RET_DK), col(0)),
            pl.BlockSpec((None, Lc, RET_DV), col(RET_HEADS)),
            pl.BlockSpec((2, None, 1, RET_CHUNK), lambda b, h: (0, h, 0, 0)),
            pl.BlockSpec((1, RET_DV), lambda b, h: (0, h)),
        ],
        out_specs=pl.BlockSpec((None, L, RET_DV), col(0)),
        out_shape=jax.ShapeDtypeStruct((B, L, RET_W), BF16),
        scratch_shapes=[pltpu.VMEM((T, 2 * RET_DK, RET_DV), F32),
                        pltpu.VMEM((T, 2 * RET_DK, RET_DV), BF16)],
        compiler_params=pltpu.CompilerParams(dimension_semantics=("arbitrary", "arbitrary"),
                                             vmem_limit_bytes=VMEM_LIMIT),
        name="retention",
    )(ret_in, ret_in, ret_in, ret_in, ctx_ret, ctx_ret, rate, gn_g)


def _attention_kernel(q_ref, k_ref, v_ref, kc_ref, vc_ref, o_ref, ks_ref, vt_ref):
    kv = pl.program_id(1)
    Lc = kc_ref.shape[0]
    L = k_ref.shape[0]
    S = Lc + L
    tq = q_ref.shape[0]

    @pl.when(pl.program_id(2) == 0)
    def _():
        def own_twice(x):
            xf = x.astype(F32)
            lane = lax.broadcasted_iota(jnp.int32, xf.shape, 1)
            own = (lane // ATT_HD) == kv
            return jnp.where(own, xf, pltpu.roll(xf, ATT_HD, 1))

        ks_ref[0:Lc, :] = own_twice(kc_ref[...]).astype(BF16)
        ks_ref[Lc:S, :] = own_twice(k_ref[...]).astype(BF16)
        vt_ref[:, 0:Lc] = own_twice(vc_ref[...]).T[:ATT_HD].astype(BF16)
        vt_ref[:, Lc:S] = own_twice(v_ref[...]).T[:ATT_HD].astype(BF16)

    q_t = q_ref[...].astype(F32).T.astype(BF16)
    pad = jnp.zeros((LANES - ATT_HD, tq), BF16)
    wq = [jnp.concatenate([q_t[g * ATT_HD:(g + 1) * ATT_HD], pad], axis=0) for g in range(ATT_GROUP)]

    def scores(c):
        k_c = ks_ref[c * ATT_KEYS:(c + 1) * ATT_KEYS, :]
        return [_dot(k_c, wq[g]) for g in range(ATT_GROUP)]

    m = [None] * ATT_GROUP
    l = [None] * ATT_GROUP
    acc = [None] * ATT_GROUP
    n_tiles = S // ATT_KEYS
    s_next = scores(0)
    for c in range(n_tiles):
        s_cur = s_next
        if c + 1 < n_tiles:
            s_next = scores(c + 1)
        v_c = vt_ref[:, c * ATT_KEYS:(c + 1) * ATT_KEYS]
        for g in range(ATT_GROUP):
            s = s_cur[g]
            mt = jnp.max(s, axis=0, keepdims=True)
            if c == 0:
                m[g] = mt
                p = jnp.exp2(s - mt)
                l[g] = jnp.sum(p, axis=0, keepdims=True)
                acc[g] = _dot(v_c, p.astype(BF16))
            else:
                m_new = jnp.maximum(m[g], mt)
                alpha = jnp.exp2(m[g] - m_new)
                p = jnp.exp2(s - m_new)
                l[g] = l[g] * alpha + jnp.sum(p, axis=0, keepdims=True)
                acc[g] = acc[g] * alpha + _dot(v_c, p.astype(BF16))
                m[g] = m_new
    out_t = jnp.concatenate([acc[g] / l[g] for g in range(ATT_GROUP)], axis=0)
    o_ref[...] = out_t.T.astype(o_ref.dtype)


def _attention_call(aq, akv, ctx_akv):
    B, L, _ = aq.shape
    Lc = ctx_akv.shape[1]
    tq = TQ_ATT
    S = Lc + L
    assert S % ATT_KEYS == 0
    return pl.pallas_call(
        _attention_kernel,
        grid=(B, ATT_KV_HEADS, L // tq),
        in_specs=[
            pl.BlockSpec((None, tq, ATT_GROUP * ATT_HD), lambda b, kv, i: (b, i, kv)),
            pl.BlockSpec((None, L, LANES), lambda b, kv, i: (b, 0, 0)),
            pl.BlockSpec((None, L, LANES), lambda b, kv, i: (b, 0, 1)),
            pl.BlockSpec((None, Lc, LANES), lambda b, kv, i: (b, 0, 0)),
            pl.BlockSpec((None, Lc, LANES), lambda b, kv, i: (b, 0, 1)),
        ],
        out_specs=pl.BlockSpec((None, tq, ATT_GROUP * ATT_HD), lambda b, kv, i: (b, i, kv)),
        out_shape=jax.ShapeDtypeStruct((B, L, ATT_W), BF16),
        scratch_shapes=[pltpu.VMEM((S, LANES), BF16), pltpu.VMEM((ATT_HD, S), BF16)],
        compiler_params=pltpu.CompilerParams(
            dimension_semantics=("arbitrary", "arbitrary", "arbitrary"),
            vmem_limit_bytes=VMEM_LIMIT),
        name="attention",
    )(aq, akv, akv, ctx_akv, ctx_akv)


def _out_ffn_kernel(x_ref, ret_ref, att_ref, g1_ref, sh2_ref, sc2_ref, g2_ref, n2_ref, nf_ref,
                    wo_r_ref, wo_a_ref, w1_ref, w2_ref, o_ref):
    mix = _dot(ret_ref[...], wo_r_ref[...]) + _dot(att_ref[...], wo_a_ref[...])
    x1 = x_ref[...] + g1_ref[...] * mix
    h2 = _norm_modulate(x1, n2_ref[...], sh2_ref[...], sc2_ref[...]).astype(BF16)
    ff = jnp.zeros(x1.shape, F32)
    for j in range(D_FF // FF_CHUNK):
        a = jnp.maximum(_dot(h2, w1_ref[:, j * FF_CHUNK:(j + 1) * FF_CHUNK]), 0.0)
        ff = ff + _dot((a * a).astype(BF16), w2_ref[j * FF_CHUNK:(j + 1) * FF_CHUNK, :])
    x2 = x1 + g2_ref[...] * ff
    ms = jnp.mean(x2 * x2, axis=-1, keepdims=True)
    o_ref[...] = x2 * lax.rsqrt(ms + EPS) * nf_ref[...]


def _out_ffn_call(x, ret, att, mod3, norm2_g, final_g, wo_r, wo_a, w1, w2):
    B, L, D = x.shape
    tm = TM_PROJ
    const = lambda b, i: (0, 0)
    once = pl.Buffered(1)

    def modrow(j):
        return pl.BlockSpec((None, 1, D), lambda b, i: (b, 0, j))

    return pl.pallas_call(
        _out_ffn_kernel,
        grid=(B, L // tm),
        in_specs=[
            pl.BlockSpec((None, tm, D), lambda b, i: (b, i, 0)),
            pl.BlockSpec((None, tm, RET_W), lambda b, i: (b, i, 0)),
            pl.BlockSpec((None, tm, ATT_W), lambda b, i: (b, i, 0)),
            modrow(2), modrow(3), modrow(4), modrow(5),
            pl.BlockSpec((1, D), const),
            pl.BlockSpec((1, D), const),
            pl.BlockSpec(wo_r.shape, const, pipeline_mode=once),
            pl.BlockSpec(wo_a.shape, const, pipeline_mode=once),
            pl.BlockSpec(w1.shape, const, pipeline_mode=once),
            pl.BlockSpec(w2.shape, const, pipeline_mode=once),
        ],
        out_specs=pl.BlockSpec((None, tm, D), lambda b, i: (b, i, 0)),
        out_shape=jax.ShapeDtypeStruct((B, L, D), F32),
        compiler_params=pltpu.CompilerParams(dimension_semantics=("arbitrary", "arbitrary"),
                                             vmem_limit_bytes=VMEM_LIMIT),
        name="out_ffn",
    )(x, ret, att, mod3, mod3, mod3, mod3, norm2_g, final_g, wo_r, wo_a, w1, w2)


def _freqs(n_pairs):
    return ROPE_BASE ** (-jnp.arange(n_pairs, dtype=F32) / n_pairs)


def _rope_tables(L):
    t = jnp.arange(L, dtype=F32)
    ang = t[:, None] * _freqs(RET_DK // 2)
    cr = jnp.concatenate([jnp.cos(ang), jnp.cos(ang)], axis=-1)
    sr = jnp.concatenate([-jnp.sin(ang), jnp.sin(ang)], axis=-1)
    rows = L // GRID_W
    row = jnp.repeat(jnp.arange(rows, dtype=F32), GRID_W)
    col = jnp.tile(jnp.arange(GRID_W, dtype=F32), rows)
    af = _freqs(ATT_HD // 4)
    aang = jnp.concatenate([row[:, None] * af, col[:, None] * af], axis=-1)
    c, s = jnp.cos(aang), jnp.sin(aang)
    ca = jnp.concatenate([c, c, c, c], axis=-1)
    sa = jnp.concatenate([-s, s, -s, s], axis=-1)
    return cr, sr, ca, sa


def kernel(x, c, ctx, c_ctx, w_mod, b_mod, norm1_g, norm2_g, w_in, w_out, ret_log_rate, ret_gn_g,
           q_norm_g, k_norm_g, w_ff1, w_ff2, final_norm_g):
    B, L, D = x.shape
    assert w_mod.shape[0] == 1, "single-layer configuration"
    assert B + 1 <= MOD_ROWS and L % TM_PROJ == 0 and L % TQ_ATT == 0 and L % RET_CHUNK == 0

    c_rows = jnp.zeros((MOD_ROWS, D), F32).at[:B].set(c).at[B].set(c_ctx)
    mod = _mod_call(c_rows, w_mod[0], b_mod[0][None, :])
    mod3 = mod.reshape(MOD_ROWS, 1, 6 * D)

    w_in_b = w_in[0].astype(BF16)
    w_rkv = w_in_b[:, RK_OFF:RG_OFF]
    w_akv = w_in_b[:, AK_OFF:D_IN]
    wo = w_out[0].astype(BF16)
    w1 = w_ff1[0].astype(BF16)
    w2 = w_ff2[0].astype(BF16)
    n1 = norm1_g[0][None, :]
    n2 = norm2_g[0][None, :]
    nf = final_norm_g[None, :]
    gq2 = jnp.tile(q_norm_g[0], LANES // ATT_HD)[None, :]
    gk2 = jnp.tile(k_norm_g[0], LANES // ATT_HD)[None, :]
    gn = ret_gn_g[0][None, :]
    rate = jnp.broadcast_to(ret_log_rate[0].astype(F32)[:, :, None, None],
                            (2, RET_HEADS, 1, RET_CHUNK))
    cr, sr, ca, sa = _rope_tables(L)

    ctx_ret, ctx_akv = _ctx_proj_call(ctx, mod3, n1, w_rkv, w_akv, gk2)
    ret_in, aq, akv = _in_proj_call(x, mod3, n1, w_in_b, cr, sr, ca, sa, gq2, gk2)
    ret = _retention_call(ret_in, ctx_ret, rate, gn)
    att = _attention_call(aq, akv, ctx_akv)
    return _out_ffn_call(x, ret, att, mod3, n2, nf, wo[:RET_W], wo[RET_W:], w1, w2)
```

```python
import math

import jax
import jax.numpy as jnp
from jax import lax
from jax.experimental import pallas as pl
from jax.experimental.pallas import tpu as pltpu

D_MODEL = 1024
CTX_LEN = 256
GRID_W = 64

RET_HEADS = 4
RET_DK = 128
RET_DV = 128
RET_W = RET_HEADS * RET_DV
RET_SCALE = RET_DK ** -0.5

ATT_HEADS = 8
ATT_KV_HEADS = 2
ATT_GROUP = ATT_HEADS // ATT_KV_HEADS
ATT_HD = 64
ATT_W = ATT_HEADS * ATT_HD
ATT_SCALE = ATT_HD ** -0.5

MIX_W = RET_W + ATT_W
D_FF = 4 * D_MODEL
ROPE_BASE = 10000.0
EPS = 1e-6
LOG2E = math.log2(math.e)

RQ_OFF = 0
RK_OFF = RQ_OFF + RET_HEADS * RET_DK
RV_OFF = RK_OFF + RET_HEADS * RET_DK
RG_OFF = RV_OFF + RET_W
AQ_OFF = RG_OFF + RET_W
AK_OFF = AQ_OFF + ATT_W
AV_OFF = AK_OFF + ATT_KV_HEADS * ATT_HD
D_IN = AV_OFF + ATT_KV_HEADS * ATT_HD

LANES = 128
BF16_SUBLANES = 16
MXU_DIM = 256
VMEM_LIMIT = 56 * 1024 * 1024

MOD_ROWS = 16
TM_PROJ = 512
TQ_ATT = 256
ATT_KEYS = 256
ATT_AHEAD = 6
RET_CHUNK = 256
FF_CHUNK = 1024

F32 = jnp.float32
BF16 = jnp.bfloat16


def _dot(a, b):
    return jnp.dot(a, b, preferred_element_type=F32)


def _dot_nt(a, b):
    return lax.dot_general(a, b, (((1,), (1,)), ((), ())), preferred_element_type=F32)


def _dot_tn(a, b):
    return lax.dot_general(a, b, (((0,), (0,)), ((), ())), preferred_element_type=F32)


def _silu(x):
    return x * jax.nn.sigmoid(x)


def _head_mean_matrix(n):
    r = lax.broadcasted_iota(jnp.int32, (n, n), 0) // ATT_HD
    c = lax.broadcasted_iota(jnp.int32, (n, n), 1) // ATT_HD
    return jnp.where(r == c, 1.0 / ATT_HD, 0.0).astype(BF16)


def _head_rms_scale(blk, bd):
    ms = _dot((blk * blk).astype(BF16), bd)
    return lax.rsqrt(ms + EPS)


def _rope_ret(blk, c2, s2):
    return blk * c2 + pltpu.roll(blk, RET_DK // 2, 1) * s2


def _rope_att(blk, c2, s2, first_half):
    up = pltpu.roll(blk, ATT_HD // 2, 1)
    dn = pltpu.roll(blk, LANES - ATT_HD // 2, 1)
    return blk * c2 + jnp.where(first_half, dn, up) * s2


def _mod_kernel(c_ref, w_ref, b_ref, o_ref):
    a = _silu(c_ref[...])
    o_ref[...] = jnp.dot(a, w_ref[...], preferred_element_type=F32,
                         precision=lax.Precision.HIGHEST) + b_ref[...]


def _mod_call(c_rows, w_mod, b_mod):
    n = w_mod.shape[1]
    bn = 1024
    return pl.pallas_call(
        _mod_kernel,
        grid=(n // bn,),
        in_specs=[
            pl.BlockSpec((MOD_ROWS, D_MODEL), lambda j: (0, 0)),
            pl.BlockSpec((D_MODEL, bn), lambda j: (0, j)),
            pl.BlockSpec((1, bn), lambda j: (0, j)),
        ],
        out_specs=pl.BlockSpec((MOD_ROWS, bn), lambda j: (0, j)),
        out_shape=jax.ShapeDtypeStruct((MOD_ROWS, n), F32),
        compiler_params=pltpu.CompilerParams(dimension_semantics=("arbitrary",),
                                             vmem_limit_bytes=VMEM_LIMIT),
        name="mod",
    )(c_rows, w_mod, b_mod)


def _norm_modulate(x, g, sh, sc):
    ms = jnp.mean(x * x, axis=-1, keepdims=True)
    y = x * lax.rsqrt(ms + EPS) * g
    return y * (1.0 + sc) + sh


def _ctx_proj_kernel(x_ref, sh_ref, sc_ref, g_ref, wr_ref, wa_ref, gk_ref, ret_ref, akv_ref):
    h = _norm_modulate(x_ref[...], g_ref[...], sh_ref[...], sc_ref[...]).astype(BF16)
    pr = _dot(h, wr_ref[...])
    ret_ref[:, :RET_W] = (pr[:, :RET_W] * RET_SCALE).astype(BF16)
    ret_ref[:, RET_W:] = pr[:, RET_W:].astype(BF16)
    pa = _dot(h, wa_ref[...])
    ak = pa[:, :LANES]
    r = _head_rms_scale(ak, _head_mean_matrix(LANES))
    akv_ref[:, :LANES] = (ak * r * gk_ref[...]).astype(BF16)
    akv_ref[:, LANES:] = pa[:, LANES:].astype(BF16)


def _ctx_proj_call(ctx, mod3, norm1_g, w_rkv, w_akv, gk2):
    B, Lc, D = ctx.shape
    ctx_row = B
    const = lambda b: (0, 0)
    return pl.pallas_call(
        _ctx_proj_kernel,
        grid=(B,),
        in_specs=[
            pl.BlockSpec((None, Lc, D), lambda b: (b, 0, 0)),
            pl.BlockSpec((None, 1, D), lambda b: (ctx_row, 0, 0)),
            pl.BlockSpec((None, 1, D), lambda b: (ctx_row, 0, 1)),
            pl.BlockSpec((1, D), const),
            pl.BlockSpec(w_rkv.shape, const),
            pl.BlockSpec(w_akv.shape, const),
            pl.BlockSpec((1, LANES), const),
        ],
        out_specs=[
            pl.BlockSpec((None, Lc, 2 * RET_W), lambda b: (b, 0, 0)),
            pl.BlockSpec((None, Lc, 2 * LANES), lambda b: (b, 0, 0)),
        ],
        out_shape=[
            jax.ShapeDtypeStruct((B, Lc, 2 * RET_W), BF16),
            jax.ShapeDtypeStruct((B, Lc, 2 * LANES), BF16),
        ],
        compiler_params=pltpu.CompilerParams(dimension_semantics=("arbitrary",),
                                             vmem_limit_bytes=VMEM_LIMIT),
        name="ctx_proj",
    )(ctx, mod3, mod3, norm1_g, w_rkv, w_akv, gk2)


def _in_proj_kernel(x_ref, sh_ref, sc_ref, g_ref, w_ref, cr_ref, sr_ref, ca_ref, sa_ref,
                    gq_ref, gk_ref, ret_ref, aq_ref, akv_ref):
    tm = x_ref.shape[0]
    h = _norm_modulate(x_ref[...], g_ref[...], sh_ref[...], sc_ref[...]).astype(BF16)
    cr, sr = cr_ref[...], sr_ref[...]
    ca, sa = ca_ref[...], sa_ref[...]
    lane = lax.broadcasted_iota(jnp.int32, (tm, LANES), 1)
    first_half = (lane & (ATT_HD // 2)) == 0

    pq = _dot(h, w_ref[:, RQ_OFF:RK_OFF])
    for hh in range(RET_HEADS):
        sl = slice(hh * RET_DK, (hh + 1) * RET_DK)
        ret_ref[:, RQ_OFF + hh * RET_DK:RQ_OFF + (hh + 1) * RET_DK] = _rope_ret(pq[:, sl], cr, sr).astype(BF16)
    pk = _dot(h, w_ref[:, RK_OFF:RV_OFF])
    for hh in range(RET_HEADS):
        sl = slice(hh * RET_DK, (hh + 1) * RET_DK)
        ret_ref[:, RK_OFF + hh * RET_DK:RK_OFF + (hh + 1) * RET_DK] = (
            _rope_ret(pk[:, sl], cr, sr) * RET_SCALE).astype(BF16)
    ret_ref[:, RV_OFF:AQ_OFF] = _dot(h, w_ref[:, RV_OFF:AQ_OFF]).astype(BF16)

    bd = _head_mean_matrix(MXU_DIM)
    gq, gk = gq_ref[...], gk_ref[...]
    pa = _dot(h, w_ref[:, AQ_OFF:AK_OFF])
    for j in range(ATT_W // MXU_DIM):
        blk = pa[:, j * MXU_DIM:(j + 1) * MXU_DIM]
        r = _head_rms_scale(blk, bd) * (ATT_SCALE * LOG2E)
        for i in range(MXU_DIM // LANES):
            sl = slice(i * LANES, (i + 1) * LANES)
            o = _rope_att(blk[:, sl] * gq, ca, sa, first_half) * r[:, sl]
            aq_ref[:, j * MXU_DIM + i * LANES:j * MXU_DIM + (i + 1) * LANES] = o.astype(BF16)

    pkv = _dot(h, w_ref[:, AK_OFF:D_IN])
    ak = pkv[:, :LANES]
    rk = _head_rms_scale(ak, bd[:LANES, :LANES])
    akv_ref[:, :LANES] = (_rope_att(ak * gk, ca, sa, first_half) * rk).astype(BF16)
    akv_ref[:, LANES:] = pkv[:, LANES:].astype(BF16)


def _in_proj_call(x, mod3, norm1_g, w_in, cr, sr, ca, sa, gq2, gk2):
    B, L, D = x.shape
    tm = TM_PROJ
    const = lambda b, i: (0, 0)
    tok = lambda b, i: (i, 0)
    return pl.pallas_call(
        _in_proj_kernel,
        grid=(B, L // tm),
        in_specs=[
            pl.BlockSpec((None, tm, D), lambda b, i: (b, i, 0)),
            pl.BlockSpec((None, 1, D), lambda b, i: (b, 0, 0)),
            pl.BlockSpec((None, 1, D), lambda b, i: (b, 0, 1)),
            pl.BlockSpec((1, D), const),
            pl.BlockSpec(w_in.shape, const),
            pl.BlockSpec((tm, LANES), tok),
            pl.BlockSpec((tm, LANES), tok),
            pl.BlockSpec((tm, LANES), tok),
            pl.BlockSpec((tm, LANES), tok),
            pl.BlockSpec((1, LANES), const),
            pl.BlockSpec((1, LANES), const),
        ],
        out_specs=[
            pl.BlockSpec((None, tm, AQ_OFF), lambda b, i: (b, i, 0)),
            pl.BlockSpec((None, tm, ATT_W), lambda b, i: (b, i, 0)),
            pl.BlockSpec((None, tm, 2 * LANES), lambda b, i: (b, i, 0)),
        ],
        out_shape=[
            jax.ShapeDtypeStruct((B, L, AQ_OFF), BF16),
            jax.ShapeDtypeStruct((B, L, ATT_W), BF16),
            jax.ShapeDtypeStruct((B, L, 2 * LANES), BF16),
        ],
        compiler_params=pltpu.CompilerParams(dimension_semantics=("arbitrary", "arbitrary"),
                                             vmem_limit_bytes=VMEM_LIMIT),
        name="in_proj",
    )(x, mod3, mod3, norm1_g, w_in, cr, sr, ca, sa, gq2, gk2)


def _retention_kernel(q_ref, k_ref, v_ref, g_ref, kc_ref, vc_ref, rate_ref, gn_ref, o_ref,
                      u_ref, s_ref):
    L = q_ref.shape[0]
    Lc = kc_ref.shape[0]
    C = RET_CHUNK
    T = L // C

    lg = jnp.log1p(-jnp.exp(rate_ref[...]))
    lgf, lgb = lg[0], lg[1]
    lgf1, lgb1 = lgf[:, :RET_DK], lgb[:, :RET_DK]

    ri = lax.broadcasted_iota(jnp.int32, (C, C), 0)
    ci = lax.broadcasted_iota(jnp.int32, (C, C), 1)
    rel = (ri - ci).astype(F32)
    decay = (jnp.where(rel >= 0, jnp.exp(lgf * jnp.maximum(rel, 0.0)), 0.0)
             + jnp.where(rel <= 0, jnp.exp(lgb * jnp.maximum(-rel, 0.0)), 0.0))

    pos = lax.broadcasted_iota(jnp.int32, (C, RET_DK), 0).astype(F32)
    q_dec_f = jnp.exp(lgf1 * (pos + 1.0))
    q_dec_b = jnp.exp(lgb1 * (C - pos))
    k_dec_f = jnp.exp(lgf1 * (C - 1.0 - pos))
    k_dec_b = jnp.exp(lgb1 * pos)
    c_dec_f = jnp.exp(lgf1 * C)
    c_dec_b = jnp.exp(lgb1 * C)

    cpos = lax.broadcasted_iota(jnp.int32, (Lc, RET_DK), 0).astype(F32)
    kc = kc_ref[...].astype(F32)
    vc = vc_ref[...]
    s_f0 = _dot_tn((kc * jnp.exp(lgf1 * (Lc - 1.0 - cpos))).astype(BF16), vc)
    s_b0 = _dot_tn((kc * jnp.exp(lgb1 * cpos)).astype(BF16), vc)

    def rows(t):
        return slice(t * C, (t + 1) * C)

    for t in range(T):
        kf = k_ref[rows(t), :].astype(F32)
        kd = jnp.concatenate([(kf * k_dec_f).astype(BF16), (kf * k_dec_b).astype(BF16)], axis=1)
        u_ref[t] = _dot_tn(kd, v_ref[rows(t), :])

    s_f, s_b = s_f0, s_b0
    for t in range(T):
        tb = T - 1 - t
        s_ref[t, :RET_DK, :] = s_f.astype(BF16)
        s_ref[tb, RET_DK:, :] = s_b.astype(BF16)
        s_f = s_f * c_dec_f + u_ref[t, :RET_DK, :]
        s_b = s_b * c_dec_b + u_ref[tb, RET_DK:, :]

    gn = gn_ref[...]

    def qk(t):
        return _dot_nt(q_ref[rows(t), :], k_ref[rows(t), :])

    sc_next = qk(0)
    for t in range(T):
        sc = sc_next
        if t + 1 < T:
            sc_next = qk(t + 1)
        qf = q_ref[rows(t), :].astype(F32)
        lhs = jnp.concatenate([(sc * decay).astype(BF16), (qf * q_dec_f).astype(BF16),
                               (qf * q_dec_b).astype(BF16)], axis=1)
        rhs = jnp.concatenate([v_ref[rows(t), :], s_ref[t]], axis=0)
        o = _dot(lhs, rhs)
        mu = jnp.mean(o, axis=-1, keepdims=True)
        d = o - mu
        var = jnp.mean(d * d, axis=-1, keepdims=True)
        on = d * lax.rsqrt(var + EPS) * gn
        o_ref[rows(t), :] = (on * _silu(g_ref[rows(t), :].astype(F32))).astype(o_ref.dtype)


def _retention_call(ret_in, ctx_ret, rate, gn_g):
    B, L, _ = ret_in.shape
    Lc = ctx_ret.shape[1]
    H = RET_HEADS
    T = L // RET_CHUNK

    def col(off):
        return lambda b, h: (b, 0, off + h)

    return pl.pallas_call(
        _retention_kernel,
        grid=(B, H),
        in_specs=[
            pl.BlockSpec((None, L, RET_DK), col(RQ_OFF // RET_DK)),
            pl.BlockSpec((None, L, RET_DK), col(RK_OFF // RET_DK)),
            pl.BlockSpec((None, L, RET_DV), col(RV_OFF // RET_DV)),
            pl.BlockSpec((None, L, RET_DV), col(RG_OFF // RET_DV)),
            pl.BlockSpec((None, Lc, RET_DK), col(0)),
            pl.BlockSpec((None, Lc, RET_DV), col(RET_HEADS)),
            pl.BlockSpec((2, None, 1, RET_CHUNK), lambda b, h: (0, h, 0, 0)),
            pl.BlockSpec((1, RET_DV), lambda b, h: (0, h)),
        ],
        out_specs=pl.BlockSpec((None, L, RET_DV), col(0)),
        out_shape=jax.ShapeDtypeStruct((B, L, RET_W), BF16),
        scratch_shapes=[pltpu.VMEM((T, 2 * RET_DK, RET_DV), F32),
                        pltpu.VMEM((T, 2 * RET_DK, RET_DV), BF16)],
        compiler_params=pltpu.CompilerParams(dimension_semantics=("arbitrary", "arbitrary"),
                                             vmem_limit_bytes=VMEM_LIMIT),
        name="retention",
    )(ret_in, ret_in, ret_in, ret_in, ctx_ret, ctx_ret, rate, gn_g)


def _attention_kernel(q_ref, k_ref, v_ref, kc_ref, vc_ref, o_ref, ks_ref, vt_ref):
    kv = pl.program_id(1)
    Lc = kc_ref.shape[0]
    L = k_ref.shape[0]
    S = Lc + L
    tq = q_ref.shape[0]

    @pl.when(pl.program_id(2) == 0)
    def _():
        def own_twice(x):
            xf = x.astype(F32)
            lane = lax.broadcasted_iota(jnp.int32, xf.shape, 1)
            own = (lane // ATT_HD) == kv
            return jnp.where(own, xf, pltpu.roll(xf, ATT_HD, 1))

        ks_ref[0:Lc, :] = own_twice(kc_ref[...]).astype(BF16)
        ks_ref[Lc:S, :] = own_twice(k_ref[...]).astype(BF16)
        vt_ref[:ATT_HD, 0:Lc] = own_twice(vc_ref[...]).T[:ATT_HD].astype(BF16)
        vt_ref[:ATT_HD, Lc:S] = own_twice(v_ref[...]).T[:ATT_HD].astype(BF16)
        vt_ref[ATT_HD:, :] = jnp.ones((BF16_SUBLANES, S), BF16)

    q_t = q_ref[...].astype(F32).T.astype(BF16)
    pad = jnp.zeros((LANES - ATT_HD, tq), BF16)
    wq = [jnp.concatenate([q_t[g * ATT_HD:(g + 1) * ATT_HD], pad], axis=0) for g in range(ATT_GROUP)]

    def scores(c, g):
        return _dot(ks_ref[c * ATT_KEYS:(c + 1) * ATT_KEYS, :], wq[g])

    m = [None] * ATT_GROUP
    acc = [None] * ATT_GROUP

    def fold(c, g, s):
        v_c = vt_ref[:, c * ATT_KEYS:(c + 1) * ATT_KEYS]
        mt = jnp.max(s, axis=0, keepdims=True)
        if c == 0:
            m[g] = mt
            acc[g] = _dot(v_c, jnp.exp2(s - mt).astype(BF16))
        else:
            m_new = jnp.maximum(m[g], mt)
            alpha = jnp.exp2(m[g] - m_new)
            acc[g] = acc[g] * alpha + _dot(v_c, jnp.exp2(s - m_new).astype(BF16))
            m[g] = m_new

    units = [(c, g) for c in range(S // ATT_KEYS) for g in range(ATT_GROUP)]
    pending = []
    for i in range(len(units) + ATT_AHEAD):
        if i < len(units):
            pending.append(scores(*units[i]))
        if i >= ATT_AHEAD:
            fold(*units[i - ATT_AHEAD], pending.pop(0))
    out_t = jnp.concatenate([acc[g][:ATT_HD] / acc[g][ATT_HD:ATT_HD + 1] for g in range(ATT_GROUP)],
                            axis=0)
    o_ref[...] = out_t.T.astype(o_ref.dtype)


def _attention_call(aq, akv, ctx_akv):
    B, L, _ = aq.shape
    Lc = ctx_akv.shape[1]
    tq = TQ_ATT
    S = Lc + L
    assert S % ATT_KEYS == 0
    return pl.pallas_call(
        _attention_kernel,
        grid=(B, ATT_KV_HEADS, L // tq),
        in_specs=[
            pl.BlockSpec((None, tq, ATT_GROUP * ATT_HD), lambda b, kv, i: (b, i, kv)),
            pl.BlockSpec((None, L, LANES), lambda b, kv, i: (b, 0, 0)),
            pl.BlockSpec((None, L, LANES), lambda b, kv, i: (b, 0, 1)),
            pl.BlockSpec((None, Lc, LANES), lambda b, kv, i: (b, 0, 0)),
            pl.BlockSpec((None, Lc, LANES), lambda b, kv, i: (b, 0, 1)),
        ],
        out_specs=pl.BlockSpec((None, tq, ATT_GROUP * ATT_HD), lambda b, kv, i: (b, i, kv)),
        out_shape=jax.ShapeDtypeStruct((B, L, ATT_W), BF16),
        scratch_shapes=[pltpu.VMEM((S, LANES), BF16), pltpu.VMEM((ATT_HD + BF16_SUBLANES, S), BF16)],
        compiler_params=pltpu.CompilerParams(
            dimension_semantics=("arbitrary", "arbitrary", "arbitrary"),
            vmem_limit_bytes=VMEM_LIMIT),
        name="attention",
    )(aq, akv, akv, ctx_akv, ctx_akv)


def _out_ffn_kernel(x_ref, ret_ref, att_ref, g1_ref, sh2_ref, sc2_ref, g2_ref, n2_ref, nf_ref,
                    wo_r_ref, wo_a_ref, w1_ref, w2_ref, o_ref):
    mix = _dot(ret_ref[...], wo_r_ref[...]) + _dot(att_ref[...], wo_a_ref[...])
    x1 = x_ref[...] + g1_ref[...] * mix
    h2 = _norm_modulate(x1, n2_ref[...], sh2_ref[...], sc2_ref[...]).astype(BF16)
    ff = jnp.zeros(x1.shape, F32)
    for j in range(D_FF // FF_CHUNK):
        a = jnp.maximum(_dot(h2, w1_ref[:, j * FF_CHUNK:(j + 1) * FF_CHUNK]), 0.0)
        ff = ff + _dot((a * a).astype(BF16), w2_ref[j * FF_CHUNK:(j + 1) * FF_CHUNK, :])
    x2 = x1 + g2_ref[...] * ff
    ms = jnp.mean(x2 * x2, axis=-1, keepdims=True)
    o_ref[...] = x2 * lax.rsqrt(ms + EPS) * nf_ref[...]


def _out_ffn_call(x, ret, att, mod3, norm2_g, final_g, wo_r, wo_a, w1, w2):
    B, L, D = x.shape
    tm = TM_PROJ
    const = lambda b, i: (0, 0)
    once = pl.Buffered(1)

    def modrow(j):
        return pl.BlockSpec((None, 1, D), lambda b, i: (b, 0, j))

    return pl.pallas_call(
        _out_ffn_kernel,
        grid=(B, L // tm),
        in_specs=[
            pl.BlockSpec((None, tm, D), lambda b, i: (b, i, 0)),
            pl.BlockSpec((None, tm, RET_W), lambda b, i: (b, i, 0)),
            pl.BlockSpec((None, tm, ATT_W), lambda b, i: (b, i, 0)),
            modrow(2), modrow(3), modrow(4), modrow(5),
            pl.BlockSpec((1, D), const),
            pl.BlockSpec((1, D), const),
            pl.BlockSpec(wo_r.shape, const, pipeline_mode=once),
            pl.BlockSpec(wo_a.shape, const, pipeline_mode=once),
            pl.BlockSpec(w1.shape, const, pipeline_mode=once),
            pl.BlockSpec(w2.shape, const, pipeline_mode=once),
        ],
        out_specs=pl.BlockSpec((None, tm, D), lambda b, i: (b, i, 0)),
        out_shape=jax.ShapeDtypeStruct((B, L, D), F32),
        compiler_params=pltpu.CompilerParams(dimension_semantics=("arbitrary", "arbitrary"),
                                             vmem_limit_bytes=VMEM_LIMIT),
        name="out_ffn",
    )(x, ret, att, mod3, mod3, mod3, mod3, norm2_g, final_g, wo_r, wo_a, w1, w2)


def _freqs(n_pairs):
    return ROPE_BASE ** (-jnp.arange(n_pairs, dtype=F32) / n_pairs)


def _rope_tables(L):
    t = jnp.arange(L, dtype=F32)
    ang = t[:, None] * _freqs(RET_DK // 2)
    cr = jnp.concatenate([jnp.cos(ang), jnp.cos(ang)], axis=-1)
    sr = jnp.concatenate([-jnp.sin(ang), jnp.sin(ang)], axis=-1)
    rows = L // GRID_W
    row = jnp.repeat(jnp.arange(rows, dtype=F32), GRID_W)
    col = jnp.tile(jnp.arange(GRID_W, dtype=F32), rows)
    af = _freqs(ATT_HD // 4)
    aang = jnp.concatenate([row[:, None] * af, col[:, None] * af], axis=-1)
    c, s = jnp.cos(aang), jnp.sin(aang)
    ca = jnp.concatenate([c, c, c, c], axis=-1)
    sa = jnp.concatenate([-s, s, -s, s], axis=-1)
    return cr, sr, ca, sa


def kernel(x, c, ctx, c_ctx, w_mod, b_mod, norm1_g, norm2_g, w_in, w_out, ret_log_rate, ret_gn_g,
           q_norm_g, k_norm_g, w_ff1, w_ff2, final_norm_g):
    B, L, D = x.shape
    assert w_mod.shape[0] == 1, "single-layer configuration"
    assert B + 1 <= MOD_ROWS and L % TM_PROJ == 0 and L % TQ_ATT == 0 and L % RET_CHUNK == 0

    c_rows = jnp.zeros((MOD_ROWS, D), F32).at[:B].set(c).at[B].set(c_ctx)
    mod = _mod_call(c_rows, w_mod[0], b_mod[0][None, :])
    mod3 = mod.reshape(MOD_ROWS, 1, 6 * D)

    w_in_b = w_in[0].astype(BF16)
    w_rkv = w_in_b[:, RK_OFF:RG_OFF]
    w_akv = w_in_b[:, AK_OFF:D_IN]
    wo = w_out[0].astype(BF16)
    w1 = w_ff1[0].astype(BF16)
    w2 = w_ff2[0].astype(BF16)
    n1 = norm1_g[0][None, :]
    n2 = norm2_g[0][None, :]
    nf = final_norm_g[None, :]
    gq2 = jnp.tile(q_norm_g[0], LANES // ATT_HD)[None, :]
    gk2 = jnp.tile(k_norm_g[0], LANES // ATT_HD)[None, :]
    gn = ret_gn_g[0][None, :]
    rate = jnp.broadcast_to(ret_log_rate[0].astype(F32)[:, :, None, None],
                            (2, RET_HEADS, 1, RET_CHUNK))
    cr, sr, ca, sa = _rope_tables(L)

    ctx_ret, ctx_akv = _ctx_proj_call(ctx, mod3, n1, w_rkv, w_akv, gk2)
    ret_in, aq, akv = _in_proj_call(x, mod3, n1, w_in_b, cr, sr, ca, sa, gq2, gk2)
    ret = _retention_call(ret_in, ctx_ret, rate, gn)
    att = _attention_call(aq, akv, ctx_akv)
    return _out_ffn_call(x, ret, att, mod3, n2, nf, wo[:RET_W], wo[RET_W:], w1, w2)
```

```python
import math

import jax
import jax.numpy as jnp
from jax import lax
from jax.experimental import pallas as pl
from jax.experimental.pallas import tpu as pltpu

D_MODEL = 1024
CTX_LEN = 256
GRID_W = 64

RET_HEADS = 4
RET_DK = 128
RET_DV = 128
RET_W = RET_HEADS * RET_DV
RET_SCALE = RET_DK ** -0.5

ATT_HEADS = 8
ATT_KV_HEADS = 2
ATT_GROUP = ATT_HEADS // ATT_KV_HEADS
ATT_HD = 64
ATT_W = ATT_HEADS * ATT_HD
ATT_SCALE = ATT_HD ** -0.5

MIX_W = RET_W + ATT_W
D_FF = 4 * D_MODEL
ROPE_BASE = 10000.0
EPS = 1e-6
LOG2E = math.log2(math.e)

RQ_OFF = 0
RK_OFF = RQ_OFF + RET_HEADS * RET_DK
RV_OFF = RK_OFF + RET_HEADS * RET_DK
RG_OFF = RV_OFF + RET_W
AQ_OFF = RG_OFF + RET_W
AK_OFF = AQ_OFF + ATT_W
AV_OFF = AK_OFF + ATT_KV_HEADS * ATT_HD
D_IN = AV_OFF + ATT_KV_HEADS * ATT_HD

LANES = 128
BF16_SUBLANES = 16
MXU_DIM = 256
VMEM_LIMIT = 56 * 1024 * 1024

MOD_ROWS = 16
TM_PROJ = 512
TQ_ATT = 512
ATT_KEYS = 256
ATT_AHEAD = 6
RET_CHUNK = 256
FF_CHUNK = 1024

F32 = jnp.float32
BF16 = jnp.bfloat16


def _dot(a, b):
    return jnp.dot(a, b, preferred_element_type=F32)


def _dot_nt(a, b):
    return lax.dot_general(a, b, (((1,), (1,)), ((), ())), preferred_element_type=F32)


def _dot_tn(a, b):
    return lax.dot_general(a, b, (((0,), (0,)), ((), ())), preferred_element_type=F32)


def _silu(x):
    return x * jax.nn.sigmoid(x)


def _head_mean_matrix(n):
    r = lax.broadcasted_iota(jnp.int32, (n, n), 0) // ATT_HD
    c = lax.broadcasted_iota(jnp.int32, (n, n), 1) // ATT_HD
    return jnp.where(r == c, 1.0 / ATT_HD, 0.0).astype(BF16)


def _head_rms_scale(blk, bd):
    ms = _dot((blk * blk).astype(BF16), bd)
    return lax.rsqrt(ms + EPS)


def _rope_ret(blk, c2, s2):
    return blk * c2 + pltpu.roll(blk, RET_DK // 2, 1) * s2


def _rope_att(blk, c2, s2, first_half):
    up = pltpu.roll(blk, ATT_HD // 2, 1)
    dn = pltpu.roll(blk, LANES - ATT_HD // 2, 1)
    return blk * c2 + jnp.where(first_half, dn, up) * s2


def _mod_kernel(c_ref, w_ref, b_ref, o_ref):
    a = _silu(c_ref[...])
    o_ref[...] = jnp.dot(a, w_ref[...], preferred_element_type=F32,
                         precision=lax.Precision.HIGHEST) + b_ref[...]


def _mod_call(c_rows, w_mod, b_mod):
    n = w_mod.shape[1]
    bn = 1024
    return pl.pallas_call(
        _mod_kernel,
        grid=(n // bn,),
        in_specs=[
            pl.BlockSpec((MOD_ROWS, D_MODEL), lambda j: (0, 0)),
            pl.BlockSpec((D_MODEL, bn), lambda j: (0, j)),
            pl.BlockSpec((1, bn), lambda j: (0, j)),
        ],
        out_specs=pl.BlockSpec((MOD_ROWS, bn), lambda j: (0, j)),
        out_shape=jax.ShapeDtypeStruct((MOD_ROWS, n), F32),
        compiler_params=pltpu.CompilerParams(dimension_semantics=("arbitrary",),
                                             vmem_limit_bytes=VMEM_LIMIT),
        name="mod",
    )(c_rows, w_mod, b_mod)


def _norm_modulate(x, g, sh, sc):
    ms = jnp.mean(x * x, axis=-1, keepdims=True)
    y = x * lax.rsqrt(ms + EPS) * g
    return y * (1.0 + sc) + sh


def _ctx_proj_kernel(x_ref, sh_ref, sc_ref, g_ref, wr_ref, wa_ref, gk_ref, ret_ref, akv_ref):
    h = _norm_modulate(x_ref[...], g_ref[...], sh_ref[...], sc_ref[...]).astype(BF16)
    pr = _dot(h, wr_ref[...])
    ret_ref[:, :RET_W] = (pr[:, :RET_W] * RET_SCALE).astype(BF16)
    ret_ref[:, RET_W:] = pr[:, RET_W:].astype(BF16)
    pa = _dot(h, wa_ref[...])
    ak = pa[:, :LANES]
    r = _head_rms_scale(ak, _head_mean_matrix(LANES))
    akv_ref[:, :LANES] = (ak * r * gk_ref[...]).astype(BF16)
    akv_ref[:, LANES:] = pa[:, LANES:].astype(BF16)


def _ctx_proj_call(ctx, mod3, norm1_g, w_rkv, w_akv, gk2):
    B, Lc, D = ctx.shape
    ctx_row = B
    const = lambda b: (0, 0)
    return pl.pallas_call(
        _ctx_proj_kernel,
        grid=(B,),
        in_specs=[
            pl.BlockSpec((None, Lc, D), lambda b: (b, 0, 0)),
            pl.BlockSpec((None, 1, D), lambda b: (ctx_row, 0, 0)),
            pl.BlockSpec((None, 1, D), lambda b: (ctx_row, 0, 1)),
            pl.BlockSpec((1, D), const),
            pl.BlockSpec(w_rkv.shape, const),
            pl.BlockSpec(w_akv.shape, const),
            pl.BlockSpec((1, LANES), const),
        ],
        out_specs=[
            pl.BlockSpec((None, Lc, 2 * RET_W), lambda b: (b, 0, 0)),
            pl.BlockSpec((None, Lc, 2 * LANES), lambda b: (b, 0, 0)),
        ],
        out_shape=[
            jax.ShapeDtypeStruct((B, Lc, 2 * RET_W), BF16),
            jax.ShapeDtypeStruct((B, Lc, 2 * LANES), BF16),
        ],
        compiler_params=pltpu.CompilerParams(dimension_semantics=("arbitrary",),
                                             vmem_limit_bytes=VMEM_LIMIT),
        name="ctx_proj",
    )(ctx, mod3, mod3, norm1_g, w_rkv, w_akv, gk2)


def _in_proj_kernel(x_ref, sh_ref, sc_ref, g_ref, w_ref, cr_ref, sr_ref, ca_ref, sa_ref,
                    gq_ref, gk_ref, ret_ref, aq_ref, akv_ref):
    tm = x_ref.shape[0]
    h = _norm_modulate(x_ref[...], g_ref[...], sh_ref[...], sc_ref[...]).astype(BF16)
    cr, sr = cr_ref[...], sr_ref[...]
    ca, sa = ca_ref[...], sa_ref[...]
    lane = lax.broadcasted_iota(jnp.int32, (tm, LANES), 1)
    first_half = (lane & (ATT_HD // 2)) == 0

    pq = _dot(h, w_ref[:, RQ_OFF:RK_OFF])
    for hh in range(RET_HEADS):
        sl = slice(hh * RET_DK, (hh + 1) * RET_DK)
        ret_ref[:, RQ_OFF + hh * RET_DK:RQ_OFF + (hh + 1) * RET_DK] = _rope_ret(pq[:, sl], cr, sr).astype(BF16)
    pk = _dot(h, w_ref[:, RK_OFF:RV_OFF])
    for hh in range(RET_HEADS):
        sl = slice(hh * RET_DK, (hh + 1) * RET_DK)
        ret_ref[:, RK_OFF + hh * RET_DK:RK_OFF + (hh + 1) * RET_DK] = (
            _rope_ret(pk[:, sl], cr, sr) * RET_SCALE).astype(BF16)
    ret_ref[:, RV_OFF:AQ_OFF] = _dot(h, w_ref[:, RV_OFF:AQ_OFF]).astype(BF16)

    bd = _head_mean_matrix(MXU_DIM)
    gq, gk = gq_ref[...], gk_ref[...]
    pa = _dot(h, w_ref[:, AQ_OFF:AK_OFF])
    for j in range(ATT_W // MXU_DIM):
        blk = pa[:, j * MXU_DIM:(j + 1) * MXU_DIM]
        r = _head_rms_scale(blk, bd) * (ATT_SCALE * LOG2E)
        for i in range(MXU_DIM // LANES):
            sl = slice(i * LANES, (i + 1) * LANES)
            o = _rope_att(blk[:, sl] * gq, ca, sa, first_half) * r[:, sl]
            aq_ref[j * MXU_DIM + i * LANES:j * MXU_DIM + (i + 1) * LANES, :] = o.T.astype(BF16)

    pkv = _dot(h, w_ref[:, AK_OFF:D_IN])
    ak = pkv[:, :LANES]
    rk = _head_rms_scale(ak, bd[:LANES, :LANES])
    akv_ref[:, :LANES] = (_rope_att(ak * gk, ca, sa, first_half) * rk).astype(BF16)
    akv_ref[:, LANES:] = pkv[:, LANES:].astype(BF16)


def _in_proj_call(x, mod3, norm1_g, w_in, cr, sr, ca, sa, gq2, gk2):
    B, L, D = x.shape
    tm = TM_PROJ
    const = lambda b, i: (0, 0)
    tok = lambda b, i: (i, 0)
    return pl.pallas_call(
        _in_proj_kernel,
        grid=(B, L // tm),
        in_specs=[
            pl.BlockSpec((None, tm, D), lambda b, i: (b, i, 0)),
            pl.BlockSpec((None, 1, D), lambda b, i: (b, 0, 0)),
            pl.BlockSpec((None, 1, D), lambda b, i: (b, 0, 1)),
            pl.BlockSpec((1, D), const),
            pl.BlockSpec(w_in.shape, const),
            pl.BlockSpec((tm, LANES), tok),
            pl.BlockSpec((tm, LANES), tok),
            pl.BlockSpec((tm, LANES), tok),
            pl.BlockSpec((tm, LANES), tok),
            pl.BlockSpec((1, LANES), const),
            pl.BlockSpec((1, LANES), const),
        ],
        out_specs=[
            pl.BlockSpec((None, tm, AQ_OFF), lambda b, i: (b, i, 0)),
            pl.BlockSpec((None, ATT_W, tm), lambda b, i: (b, 0, i)),
            pl.BlockSpec((None, tm, 2 * LANES), lambda b, i: (b, i, 0)),
        ],
        out_shape=[
            jax.ShapeDtypeStruct((B, L, AQ_OFF), BF16),
            jax.ShapeDtypeStruct((B, ATT_W, L), BF16),
            jax.ShapeDtypeStruct((B, L, 2 * LANES), BF16),
        ],
        compiler_params=pltpu.CompilerParams(dimension_semantics=("arbitrary", "arbitrary"),
                                             vmem_limit_bytes=VMEM_LIMIT),
        name="in_proj",
    )(x, mod3, mod3, norm1_g, w_in, cr, sr, ca, sa, gq2, gk2)


def _retention_kernel(q_ref, k_ref, v_ref, g_ref, kc_ref, vc_ref, rate_ref, gn_ref, o_ref,
                      u_ref, s_ref):
    L = q_ref.shape[0]
    Lc = kc_ref.shape[0]
    C = RET_CHUNK
    T = L // C

    lg = jnp.log1p(-jnp.exp(rate_ref[...]))
    lgf, lgb = lg[0], lg[1]
    lgf1, lgb1 = lgf[:, :RET_DK], lgb[:, :RET_DK]

    ri = lax.broadcasted_iota(jnp.int32, (C, C), 0)
    ci = lax.broadcasted_iota(jnp.int32, (C, C), 1)
    rel = (ri - ci).astype(F32)
    decay = (jnp.where(rel >= 0, jnp.exp(lgf * jnp.maximum(rel, 0.0)), 0.0)
             + jnp.where(rel <= 0, jnp.exp(lgb * jnp.maximum(-rel, 0.0)), 0.0))

    pos = lax.broadcasted_iota(jnp.int32, (C, RET_DK), 0).astype(F32)
    q_dec_f = jnp.exp(lgf1 * (pos + 1.0))
    q_dec_b = jnp.exp(lgb1 * (C - pos))
    k_dec_f = jnp.exp(lgf1 * (C - 1.0 - pos))
    k_dec_b = jnp.exp(lgb1 * pos)
    c_dec_f = jnp.exp(lgf1 * C)
    c_dec_b = jnp.exp(lgb1 * C)

    cpos = lax.broadcasted_iota(jnp.int32, (Lc, RET_DK), 0).astype(F32)
    kc = kc_ref[...].astype(F32)
    vc = vc_ref[...]
    s_f0 = _dot_tn((kc * jnp.exp(lgf1 * (Lc - 1.0 - cpos))).astype(BF16), vc)
    s_b0 = _dot_tn((kc * jnp.exp(lgb1 * cpos)).astype(BF16), vc)

    def rows(t):
        return slice(t * C, (t + 1) * C)

    for t in range(T):
        kf = k_ref[rows(t), :].astype(F32)
        kd = jnp.concatenate([(kf * k_dec_f).astype(BF16), (kf * k_dec_b).astype(BF16)], axis=1)
        u_ref[t] = _dot_tn(kd, v_ref[rows(t), :])

    s_f, s_b = s_f0, s_b0
    for t in range(T):
        tb = T - 1 - t
        s_ref[t, :RET_DK, :] = s_f.astype(BF16)
        s_ref[tb, RET_DK:, :] = s_b.astype(BF16)
        s_f = s_f * c_dec_f + u_ref[t, :RET_DK, :]
        s_b = s_b * c_dec_b + u_ref[tb, RET_DK:, :]

    gn = gn_ref[...]

    def qk(t):
        return _dot_nt(q_ref[rows(t), :], k_ref[rows(t), :])

    sc_next = qk(0)
    for t in range(T):
        sc = sc_next
        if t + 1 < T:
            sc_next = qk(t + 1)
        qf = q_ref[rows(t), :].astype(F32)
        lhs = jnp.concatenate([(sc * decay).astype(BF16), (qf * q_dec_f).astype(BF16),
                               (qf * q_dec_b).astype(BF16)], axis=1)
        rhs = jnp.concatenate([v_ref[rows(t), :], s_ref[t]], axis=0)
        o = _dot(lhs, rhs)
        mu = jnp.mean(o, axis=-1, keepdims=True)
        d = o - mu
        var = jnp.mean(d * d, axis=-1, keepdims=True)
        on = d * lax.rsqrt(var + EPS) * gn
        o_ref[rows(t), :] = (on * _silu(g_ref[rows(t), :].astype(F32))).astype(o_ref.dtype)


def _retention_call(ret_in, ctx_ret, rate, gn_g):
    B, L, _ = ret_in.shape
    Lc = ctx_ret.shape[1]
    H = RET_HEADS
    T = L // RET_CHUNK

    def col(off):
        return lambda b, h: (b, 0, off + h)

    return pl.pallas_call(
        _retention_kernel,
        grid=(B, H),
        in_specs=[
            pl.BlockSpec((None, L, RET_DK), col(RQ_OFF // RET_DK)),
            pl.BlockSpec((None, L, RET_DK), col(RK_OFF // RET_DK)),
            pl.BlockSpec((None, L, RET_DV), col(RV_OFF // RET_DV)),
            pl.BlockSpec((None, L, RET_DV), col(RG_OFF // RET_DV)),
            pl.BlockSpec((None, Lc, RET_DK), col(0)),
            pl.BlockSpec((None, Lc, RET_DV), col(RET_HEADS)),
            pl.BlockSpec((2, None, 1, RET_CHUNK), lambda b, h: (0, h, 0, 0)),
            pl.BlockSpec((1, RET_DV), lambda b, h: (0, h)),
        ],
        out_specs=pl.BlockSpec((None, L, RET_DV), col(0)),
        out_shape=jax.ShapeDtypeStruct((B, L, RET_W), BF16),
        scratch_shapes=[pltpu.VMEM((T, 2 * RET_DK, RET_DV), F32),
                        pltpu.VMEM((T, 2 * RET_DK, RET_DV), BF16)],
        compiler_params=pltpu.CompilerParams(dimension_semantics=("arbitrary", "arbitrary"),
                                             vmem_limit_bytes=VMEM_LIMIT),
        name="retention",
    )(ret_in, ret_in, ret_in, ret_in, ctx_ret, ctx_ret, rate, gn_g)


def _attention_kernel(q_ref, k_ref, v_ref, kc_ref, vc_ref, o_ref, ks_ref, vt_ref):
    kv = pl.program_id(1)
    Lc = kc_ref.shape[0]
    L = k_ref.shape[0]
    S = Lc + L
    tq = q_ref.shape[1]

    @pl.when(pl.program_id(2) == 0)
    def _():
        def own_twice(x):
            xf = x.astype(F32)
            lane = lax.broadcasted_iota(jnp.int32, xf.shape, 1)
            own = (lane // ATT_HD) == kv
            return jnp.where(own, xf, pltpu.roll(xf, ATT_HD, 1))

        ks_ref[0:Lc, :] = own_twice(kc_ref[...]).astype(BF16)
        ks_ref[Lc:S, :] = own_twice(k_ref[...]).astype(BF16)
        vt_ref[:ATT_HD, 0:Lc] = own_twice(vc_ref[...]).T[:ATT_HD].astype(BF16)
        vt_ref[:ATT_HD, Lc:S] = own_twice(v_ref[...]).T[:ATT_HD].astype(BF16)
        vt_ref[ATT_HD:, :] = jnp.ones((BF16_SUBLANES, S), BF16)

    streams = [(h, g) for h in range(tq // MXU_DIM) for g in range(ATT_GROUP)]
    pad = jnp.zeros((LANES - ATT_HD, MXU_DIM), BF16)

    def q_weights(h, g):
        return jnp.concatenate(
            [q_ref[g * ATT_HD:(g + 1) * ATT_HD, h * MXU_DIM:(h + 1) * MXU_DIM], pad], axis=0)

    wq = {st: q_weights(*st) for st in streams}

    def scores(c, st):
        s = _dot(ks_ref[c * ATT_KEYS:(c + 1) * ATT_KEYS, :], wq[st])
        return s, jnp.max(s, axis=0, keepdims=True)

    m = {}
    acc = {}

    def fold(c, st, s_mt):
        s, mt = s_mt
        v_c = vt_ref[:, c * ATT_KEYS:(c + 1) * ATT_KEYS]
        if c == 0:
            m[st] = mt
            acc[st] = _dot(v_c, jnp.exp2(s - mt).astype(BF16))
        else:
            m_new = jnp.maximum(m[st], mt)
            alpha = jnp.exp2(m[st] - m_new)
            acc[st] = acc[st] * alpha + _dot(v_c, jnp.exp2(s - m_new).astype(BF16))
            m[st] = m_new

    units = [(c, st) for c in range(S // ATT_KEYS) for st in streams]
    pending = []
    for i in range(len(units) + ATT_AHEAD):
        if i < len(units):
            pending.append(scores(*units[i]))
        if i >= ATT_AHEAD:
            fold(*units[i - ATT_AHEAD], pending.pop(0))
    for h, g in streams:
        a = acc[(h, g)]
        o_ref[g * ATT_HD:(g + 1) * ATT_HD, h * MXU_DIM:(h + 1) * MXU_DIM] = (
            a[:ATT_HD] / a[ATT_HD:ATT_HD + 1]).astype(o_ref.dtype)


def _attention_call(aq_t, akv, ctx_akv):
    B, _, L = aq_t.shape
    Lc = ctx_akv.shape[1]
    tq = TQ_ATT
    S = Lc + L
    assert S % ATT_KEYS == 0 and tq % MXU_DIM == 0
    return pl.pallas_call(
        _attention_kernel,
        grid=(B, ATT_KV_HEADS, L // tq),
        in_specs=[
            pl.BlockSpec((None, ATT_GROUP * ATT_HD, tq), lambda b, kv, i: (b, kv, i)),
            pl.BlockSpec((None, L, LANES), lambda b, kv, i: (b, 0, 0)),
            pl.BlockSpec((None, L, LANES), lambda b, kv, i: (b, 0, 1)),
            pl.BlockSpec((None, Lc, LANES), lambda b, kv, i: (b, 0, 0)),
            pl.BlockSpec((None, Lc, LANES), lambda b, kv, i: (b, 0, 1)),
        ],
        out_specs=pl.BlockSpec((None, ATT_GROUP * ATT_HD, tq), lambda b, kv, i: (b, kv, i)),
        out_shape=jax.ShapeDtypeStruct((B, ATT_W, L), BF16),
        scratch_shapes=[pltpu.VMEM((S, LANES), BF16), pltpu.VMEM((ATT_HD + BF16_SUBLANES, S), BF16)],
        compiler_params=pltpu.CompilerParams(
            dimension_semantics=("arbitrary", "arbitrary", "arbitrary"),
            vmem_limit_bytes=VMEM_LIMIT),
        name="attention",
    )(aq_t, akv, akv, ctx_akv, ctx_akv)


def _out_ffn_kernel(x_ref, ret_ref, att_ref, g1_ref, sh2_ref, sc2_ref, g2_ref, n2_ref, nf_ref,
                    wo_r_ref, wo_a_ref, w1_ref, w2_ref, o_ref):
    mix = _dot(ret_ref[...], wo_r_ref[...]) + _dot_tn(att_ref[...], wo_a_ref[...])
    x1 = x_ref[...] + g1_ref[...] * mix
    h2 = _norm_modulate(x1, n2_ref[...], sh2_ref[...], sc2_ref[...]).astype(BF16)
    ff = jnp.zeros(x1.shape, F32)
    for j in range(D_FF // FF_CHUNK):
        a = jnp.maximum(_dot(h2, w1_ref[:, j * FF_CHUNK:(j + 1) * FF_CHUNK]), 0.0)
        ff = ff + _dot((a * a).astype(BF16), w2_ref[j * FF_CHUNK:(j + 1) * FF_CHUNK, :])
    x2 = x1 + g2_ref[...] * ff
    ms = jnp.mean(x2 * x2, axis=-1, keepdims=True)
    o_ref[...] = x2 * lax.rsqrt(ms + EPS) * nf_ref[...]


def _out_ffn_call(x, ret, att, mod3, norm2_g, final_g, wo_r, wo_a, w1, w2):
    B, L, D = x.shape
    tm = TM_PROJ
    const = lambda b, i: (0, 0)
    once = pl.Buffered(1)

    def modrow(j):
        return pl.BlockSpec((None, 1, D), lambda b, i: (b, 0, j))

    return pl.pallas_call(
        _out_ffn_kernel,
        grid=(B, L // tm),
        in_specs=[
            pl.BlockSpec((None, tm, D), lambda b, i: (b, i, 0)),
            pl.BlockSpec((None, tm, RET_W), lambda b, i: (b, i, 0)),
            pl.BlockSpec((None, ATT_W, tm), lambda b, i: (b, 0, i)),
            modrow(2), modrow(3), modrow(4), modrow(5),
            pl.BlockSpec((1, D), const),
            pl.BlockSpec((1, D), const),
            pl.BlockSpec(wo_r.shape, const, pipeline_mode=once),
            pl.BlockSpec(wo_a.shape, const, pipeline_mode=once),
            pl.BlockSpec(w1.shape, const, pipeline_mode=once),
            pl.BlockSpec(w2.shape, const, pipeline_mode=once),
        ],
        out_specs=pl.BlockSpec((None, tm, D), lambda b, i: (b, i, 0)),
        out_shape=jax.ShapeDtypeStruct((B, L, D), F32),
        compiler_params=pltpu.CompilerParams(dimension_semantics=("arbitrary", "arbitrary"),
                                             vmem_limit_bytes=VMEM_LIMIT),
        name="out_ffn",
    )(x, ret, att, mod3, mod3, mod3, mod3, norm2_g, final_g, wo_r, wo_a, w1, w2)


def _freqs(n_pairs):
    return ROPE_BASE ** (-jnp.arange(n_pairs, dtype=F32) / n_pairs)


def _rope_tables(L):
    t = jnp.arange(L, dtype=F32)
    ang = t[:, None] * _freqs(RET_DK // 2)
    cr = jnp.concatenate([jnp.cos(ang), jnp.cos(ang)], axis=-1)
    sr = jnp.concatenate([-jnp.sin(ang), jnp.sin(ang)], axis=-1)
    rows = L // GRID_W
    row = jnp.repeat(jnp.arange(rows, dtype=F32), GRID_W)
    col = jnp.tile(jnp.arange(GRID_W, dtype=F32), rows)
    af = _freqs(ATT_HD // 4)
    aang = jnp.concatenate([row[:, None] * af, col[:, None] * af], axis=-1)
    c, s = jnp.cos(aang), jnp.sin(aang)
    ca = jnp.concatenate([c, c, c, c], axis=-1)
    sa = jnp.concatenate([-s, s, -s, s], axis=-1)
    return cr, sr, ca, sa


def kernel(x, c, ctx, c_ctx, w_mod, b_mod, norm1_g, norm2_g, w_in, w_out, ret_log_rate, ret_gn_g,
           q_norm_g, k_norm_g, w_ff1, w_ff2, final_norm_g):
    B, L, D = x.shape
    assert w_mod.shape[0] == 1, "single-layer configuration"
    assert B + 1 <= MOD_ROWS and L % TM_PROJ == 0 and L % TQ_ATT == 0 and L % RET_CHUNK == 0

    c_rows = jnp.zeros((MOD_ROWS, D), F32).at[:B].set(c).at[B].set(c_ctx)
    mod = _mod_call(c_rows, w_mod[0], b_mod[0][None, :])
    mod3 = mod.reshape(MOD_ROWS, 1, 6 * D)

    w_in_b = w_in[0].astype(BF16)
    w_rkv = w_in_b[:, RK_OFF:RG_OFF]
    w_akv = w_in_b[:, AK_OFF:D_IN]
    wo = w_out[0].astype(BF16)
    w1 = w_ff1[0].astype(BF16)
    w2 = w_ff2[0].astype(BF16)
    n1 = norm1_g[0][None, :]
    n2 = norm2_g[0][None, :]
    nf = final_norm_g[None, :]
    gq2 = jnp.tile(q_norm_g[0], LANES // ATT_HD)[None, :]
    gk2 = jnp.tile(k_norm_g[0], LANES // ATT_HD)[None, :]
    gn = ret_gn_g[0][None, :]
    rate = jnp.broadcast_to(ret_log_rate[0].astype(F32)[:, :, None, None],
                            (2, RET_HEADS, 1, RET_CHUNK))
    cr, sr, ca, sa = _rope_tables(L)

    ctx_ret, ctx_akv = _ctx_proj_call(ctx, mod3, n1, w_rkv, w_akv, gk2)
    ret_in, aq, akv = _in_proj_call(x, mod3, n1, w_in_b, cr, sr, ca, sa, gq2, gk2)
    ret = _retention_call(ret_in, ctx_ret, rate, gn)
    att = _attention_call(aq, akv, ctx_akv)
    return _out_ffn_call(x, ret, att, mod3, n2, nf, wo[:RET_W], wo[RET_W:], w1, w2)
```

```python
import math

import jax
import jax.numpy as jnp
from jax import lax
from jax.experimental import pallas as pl
from jax.experimental.pallas import tpu as pltpu

D_MODEL = 1024
CTX_LEN = 256
GRID_W = 64

RET_HEADS = 4
RET_DK = 128
RET_DV = 128
RET_W = RET_HEADS * RET_DV
RET_SCALE = RET_DK ** -0.5

ATT_HEADS = 8
ATT_KV_HEADS = 2
ATT_GROUP = ATT_HEADS // ATT_KV_HEADS
ATT_HD = 64
ATT_W = ATT_HEADS * ATT_HD
ATT_SCALE = ATT_HD ** -0.5

MIX_W = RET_W + ATT_W
D_FF = 4 * D_MODEL
ROPE_BASE = 10000.0
EPS = 1e-6
LOG2E = math.log2(math.e)

RQ_OFF = 0
RK_OFF = RQ_OFF + RET_HEADS * RET_DK
RV_OFF = RK_OFF + RET_HEADS * RET_DK
RG_OFF = RV_OFF + RET_W
AQ_OFF = RG_OFF + RET_W
AK_OFF = AQ_OFF + ATT_W
AV_OFF = AK_OFF + ATT_KV_HEADS * ATT_HD
D_IN = AV_OFF + ATT_KV_HEADS * ATT_HD

LANES = 128
BF16_SUBLANES = 16
ATT_VT_ROWS = ATT_HD + BF16_SUBLANES
MXU_DIM = 256
VMEM_LIMIT = 56 * 1024 * 1024

MOD_ROWS = 16
TM_IN = 512
TM_PROJ = 512
TQ_ATT = 512
ATT_KEYS = 256
ATT_AHEAD = 6
RET_CHUNK = 256
FF_CHUNK = 1024

F32 = jnp.float32
BF16 = jnp.bfloat16


def _dot(a, b):
    return jnp.dot(a, b, preferred_element_type=F32)


def _dot_nt(a, b):
    return lax.dot_general(a, b, (((1,), (1,)), ((), ())), preferred_element_type=F32)


def _dot_tn(a, b):
    return lax.dot_general(a, b, (((0,), (0,)), ((), ())), preferred_element_type=F32)


def _silu(x):
    return x * jax.nn.sigmoid(x)


def _head_mean_matrix(n):
    r = lax.broadcasted_iota(jnp.int32, (n, n), 0) // ATT_HD
    c = lax.broadcasted_iota(jnp.int32, (n, n), 1) // ATT_HD
    return jnp.where(r == c, 1.0 / ATT_HD, 0.0).astype(BF16)


def _head_rms_scale(blk, bd):
    ms = _dot((blk * blk).astype(BF16), bd)
    return lax.rsqrt(ms + EPS)


def _rope_ret(blk, c2, s2):
    return blk * c2 + pltpu.roll(blk, RET_DK // 2, 1) * s2


def _rope_att(blk, c2, s2, first_half):
    up = pltpu.roll(blk, ATT_HD // 2, 1)
    dn = pltpu.roll(blk, LANES - ATT_HD // 2, 1)
    return blk * c2 + jnp.where(first_half, dn, up) * s2


def _mod_kernel(c_ref, w_ref, b_ref, o_ref):
    a = _silu(c_ref[...])
    o_ref[...] = jnp.dot(a, w_ref[...], preferred_element_type=F32,
                         precision=lax.Precision.HIGHEST) + b_ref[...]


def _mod_call(c_rows, w_mod, b_mod):
    n = w_mod.shape[1]
    bn = 1024
    return pl.pallas_call(
        _mod_kernel,
        grid=(n // bn,),
        in_specs=[
            pl.BlockSpec((MOD_ROWS, D_MODEL), lambda j: (0, 0)),
            pl.BlockSpec((D_MODEL, bn), lambda j: (0, j)),
            pl.BlockSpec((1, bn), lambda j: (0, j)),
        ],
        out_specs=pl.BlockSpec((MOD_ROWS, bn), lambda j: (0, j)),
        out_shape=jax.ShapeDtypeStruct((MOD_ROWS, n), F32),
        compiler_params=pltpu.CompilerParams(dimension_semantics=("arbitrary",),
                                             vmem_limit_bytes=VMEM_LIMIT),
        name="mod",
    )(c_rows, w_mod, b_mod)


def _norm_modulate(x, g, sh, sc):
    ms = jnp.mean(x * x, axis=-1, keepdims=True)
    y = x * lax.rsqrt(ms + EPS) * g
    return y * (1.0 + sc) + sh


def _store_attention_kv(k, v, ks_ref, vt_ref):
    rows = k.shape[0]
    low = lax.broadcasted_iota(jnp.int32, k.shape, 1) < ATT_HD
    k_sw = pltpu.roll(k, ATT_HD, 1)
    ks_ref[0] = jnp.where(low, k, k_sw).astype(BF16)
    ks_ref[1] = jnp.where(low, k_sw, k).astype(BF16)
    v_t = v.T
    ones = jnp.ones((BF16_SUBLANES, rows), BF16)
    for j in range(ATT_KV_HEADS):
        vt_ref[j, :ATT_HD, :] = v_t[j * ATT_HD:(j + 1) * ATT_HD].astype(BF16)
        vt_ref[j, ATT_HD:, :] = ones


def _ctx_proj_kernel(x_ref, sh_ref, sc_ref, g_ref, wr_ref, wa_ref, gk_ref, ret_ref, ks_ref, vt_ref):
    h = _norm_modulate(x_ref[...], g_ref[...], sh_ref[...], sc_ref[...]).astype(BF16)
    pr = _dot(h, wr_ref[...])
    ret_ref[:, :RET_W] = (pr[:, :RET_W] * RET_SCALE).astype(BF16)
    ret_ref[:, RET_W:] = pr[:, RET_W:].astype(BF16)
    pa = _dot(h, wa_ref[...])
    ak = pa[:, :LANES]
    r = _head_rms_scale(ak, _head_mean_matrix(LANES))
    _store_attention_kv(ak * r * gk_ref[...], pa[:, LANES:], ks_ref, vt_ref)


def _ctx_proj_call(ctx, mod3, norm1_g, w_rkv, w_akv, gk2):
    B, Lc, D = ctx.shape
    ctx_row = B
    const = lambda b: (0, 0)
    return pl.pallas_call(
        _ctx_proj_kernel,
        grid=(B,),
        in_specs=[
            pl.BlockSpec((None, Lc, D), lambda b: (b, 0, 0)),
            pl.BlockSpec((None, 1, D), lambda b: (ctx_row, 0, 0)),
            pl.BlockSpec((None, 1, D), lambda b: (ctx_row, 0, 1)),
            pl.BlockSpec((1, D), const),
            pl.BlockSpec(w_rkv.shape, const),
            pl.BlockSpec(w_akv.shape, const),
            pl.BlockSpec((1, LANES), const),
        ],
        out_specs=[
            pl.BlockSpec((None, Lc, 2 * RET_W), lambda b: (b, 0, 0)),
            pl.BlockSpec((None, ATT_KV_HEADS, Lc, LANES), lambda b: (b, 0, 0, 0)),
            pl.BlockSpec((None, ATT_KV_HEADS, ATT_VT_ROWS, Lc), lambda b: (b, 0, 0, 0)),
        ],
        out_shape=[
            jax.ShapeDtypeStruct((B, Lc, 2 * RET_W), BF16),
            jax.ShapeDtypeStruct((B, ATT_KV_HEADS, Lc, LANES), BF16),
            jax.ShapeDtypeStruct((B, ATT_KV_HEADS, ATT_VT_ROWS, Lc), BF16),
        ],
        compiler_params=pltpu.CompilerParams(dimension_semantics=("arbitrary",),
                                             vmem_limit_bytes=VMEM_LIMIT),
        name="ctx_proj",
    )(ctx, mod3, mod3, norm1_g, w_rkv, w_akv, gk2)


def _in_proj_kernel(x_ref, sh_ref, sc_ref, g_ref, w_ref, cr_ref, sr_ref, ca_ref, sa_ref,
                    gq_ref, gk_ref, ret_ref, aq_ref, ks_ref, vt_ref):
    tm = x_ref.shape[0]
    h = _norm_modulate(x_ref[...], g_ref[...], sh_ref[...], sc_ref[...]).astype(BF16)
    cr, sr = cr_ref[...], sr_ref[...]
    ca, sa = ca_ref[...], sa_ref[...]
    lane = lax.broadcasted_iota(jnp.int32, (tm, LANES), 1)
    first_half = (lane & (ATT_HD // 2)) == 0

    bd = _head_mean_matrix(MXU_DIM)
    gq, gk = gq_ref[...], gk_ref[...]
    pa = _dot(h, w_ref[:, AQ_OFF:AK_OFF])
    for j in range(ATT_W // MXU_DIM):
        blk = pa[:, j * MXU_DIM:(j + 1) * MXU_DIM]
        r = _head_rms_scale(blk, bd) * (ATT_SCALE * LOG2E)
        for i in range(MXU_DIM // LANES):
            sl = slice(i * LANES, (i + 1) * LANES)
            o = _rope_att(blk[:, sl] * gq, ca, sa, first_half) * r[:, sl]
            aq_ref[j * MXU_DIM + i * LANES:j * MXU_DIM + (i + 1) * LANES, :] = o.T.astype(BF16)

    pkv = _dot(h, w_ref[:, AK_OFF:D_IN])
    ak = pkv[:, :LANES]
    rk = _head_rms_scale(ak, bd[:LANES, :LANES])
    _store_attention_kv(_rope_att(ak * gk, ca, sa, first_half) * rk, pkv[:, LANES:], ks_ref, vt_ref)

    pq = _dot(h, w_ref[:, RQ_OFF:RK_OFF])
    for hh in range(RET_HEADS):
        sl = slice(hh * RET_DK, (hh + 1) * RET_DK)
        ret_ref[:, RQ_OFF + hh * RET_DK:RQ_OFF + (hh + 1) * RET_DK] = _rope_ret(pq[:, sl], cr, sr).astype(BF16)
    pk = _dot(h, w_ref[:, RK_OFF:RV_OFF])
    for hh in range(RET_HEADS):
        sl = slice(hh * RET_DK, (hh + 1) * RET_DK)
        ret_ref[:, RK_OFF + hh * RET_DK:RK_OFF + (hh + 1) * RET_DK] = (
            _rope_ret(pk[:, sl], cr, sr) * RET_SCALE).astype(BF16)
    ret_ref[:, RV_OFF:AQ_OFF] = _dot(h, w_ref[:, RV_OFF:AQ_OFF]).astype(BF16)


def _in_proj_call(x, mod3, norm1_g, w_in, cr, sr, ca, sa, gq2, gk2):
    B, L, D = x.shape
    tm = TM_IN
    const = lambda b, i: (0, 0)
    tok = lambda b, i: (i, 0)
    return pl.pallas_call(
        _in_proj_kernel,
        grid=(B, L // tm),
        in_specs=[
            pl.BlockSpec((None, tm, D), lambda b, i: (b, i, 0)),
            pl.BlockSpec((None, 1, D), lambda b, i: (b, 0, 0)),
            pl.BlockSpec((None, 1, D), lambda b, i: (b, 0, 1)),
            pl.BlockSpec((1, D), const),
            pl.BlockSpec(w_in.shape, const),
            pl.BlockSpec((tm, LANES), tok),
            pl.BlockSpec((tm, LANES), tok),
            pl.BlockSpec((tm, LANES), tok),
            pl.BlockSpec((tm, LANES), tok),
            pl.BlockSpec((1, LANES), const),
            pl.BlockSpec((1, LANES), const),
        ],
        out_specs=[
            pl.BlockSpec((None, tm, AQ_OFF), lambda b, i: (b, i, 0)),
            pl.BlockSpec((None, ATT_W, tm), lambda b, i: (b, 0, i)),
            pl.BlockSpec((None, ATT_KV_HEADS, tm, LANES), lambda b, i: (b, 0, i, 0)),
            pl.BlockSpec((None, ATT_KV_HEADS, ATT_VT_ROWS, tm), lambda b, i: (b, 0, 0, i)),
        ],
        out_shape=[
            jax.ShapeDtypeStruct((B, L, AQ_OFF), BF16),
            jax.ShapeDtypeStruct((B, ATT_W, L), BF16),
            jax.ShapeDtypeStruct((B, ATT_KV_HEADS, L, LANES), BF16),
            jax.ShapeDtypeStruct((B, ATT_KV_HEADS, ATT_VT_ROWS, L), BF16),
        ],
        compiler_params=pltpu.CompilerParams(dimension_semantics=("arbitrary", "arbitrary"),
                                             vmem_limit_bytes=VMEM_LIMIT),
        name="in_proj",
    )(x, mod3, mod3, norm1_g, w_in, cr, sr, ca, sa, gq2, gk2)


def _retention_kernel(q_ref, k_ref, v_ref, g_ref, kc_ref, vc_ref, rate_ref, gn_ref, o_ref,
                      u_ref, s_ref):
    L = q_ref.shape[0]
    Lc = kc_ref.shape[0]
    C = RET_CHUNK
    T = L // C

    lg = jnp.log1p(-jnp.exp(rate_ref[...]))
    lgf, lgb = lg[0], lg[1]
    lgf1, lgb1 = lgf[:, :RET_DK], lgb[:, :RET_DK]

    ri = lax.broadcasted_iota(jnp.int32, (C, C), 0)
    ci = lax.broadcasted_iota(jnp.int32, (C, C), 1)
    rel = (ri - ci).astype(F32)
    decay = (jnp.where(rel >= 0, jnp.exp(lgf * jnp.maximum(rel, 0.0)), 0.0)
             + jnp.where(rel <= 0, jnp.exp(lgb * jnp.maximum(-rel, 0.0)), 0.0))

    pos = lax.broadcasted_iota(jnp.int32, (C, RET_DK), 0).astype(F32)
    q_dec_f = jnp.exp(lgf1 * (pos + 1.0))
    q_dec_b = jnp.exp(lgb1 * (C - pos))
    k_dec_f = jnp.exp(lgf1 * (C - 1.0 - pos))
    k_dec_b = jnp.exp(lgb1 * pos)
    c_dec_f = jnp.exp(lgf1 * C)
    c_dec_b = jnp.exp(lgb1 * C)

    cpos = lax.broadcasted_iota(jnp.int32, (Lc, RET_DK), 0).astype(F32)
    kc = kc_ref[...].astype(F32)
    vc = vc_ref[...]
    s_f0 = _dot_tn((kc * jnp.exp(lgf1 * (Lc - 1.0 - cpos))).astype(BF16), vc)
    s_b0 = _dot_tn((kc * jnp.exp(lgb1 * cpos)).astype(BF16), vc)

    def rows(t):
        return slice(t * C, (t + 1) * C)

    for t in range(T):
        kf = k_ref[rows(t), :].astype(F32)
        kd = jnp.concatenate([(kf * k_dec_f).astype(BF16), (kf * k_dec_b).astype(BF16)], axis=1)
        u_ref[t] = _dot_tn(kd, v_ref[rows(t), :])

    s_f, s_b = s_f0, s_b0
    for t in range(T):
        tb = T - 1 - t
        s_ref[t, :RET_DK, :] = s_f.astype(BF16)
        s_ref[tb, RET_DK:, :] = s_b.astype(BF16)
        s_f = s_f * c_dec_f + u_ref[t, :RET_DK, :]
        s_b = s_b * c_dec_b + u_ref[tb, RET_DK:, :]

    gn = gn_ref[...]

    def qk(t):
        return _dot_nt(q_ref[rows(t), :], k_ref[rows(t), :])

    sc_next = qk(0)
    for t in range(T):
        sc = sc_next
        if t + 1 < T:
            sc_next = qk(t + 1)
        qf = q_ref[rows(t), :].astype(F32)
        lhs = jnp.concatenate([(sc * decay).astype(BF16), (qf * q_dec_f).astype(BF16),
                               (qf * q_dec_b).astype(BF16)], axis=1)
        rhs = jnp.concatenate([v_ref[rows(t), :], s_ref[t]], axis=0)
        o = _dot(lhs, rhs)
        mu = jnp.mean(o, axis=-1, keepdims=True)
        d = o - mu
        var = jnp.mean(d * d, axis=-1, keepdims=True)
        on = d * lax.rsqrt(var + EPS) * gn
        o_ref[rows(t), :] = (on * _silu(g_ref[rows(t), :].astype(F32))).astype(o_ref.dtype)


def _retention_call(ret_in, ctx_ret, rate, gn_g):
    B, L, _ = ret_in.shape
    Lc = ctx_ret.shape[1]
    H = RET_HEADS
    T = L // RET_CHUNK

    def col(off):
        return lambda b, h: (b, 0, off + h)

    return pl.pallas_call(
        _retention_kernel,
        grid=(B, H),
        in_specs=[
            pl.BlockSpec((None, L, RET_DK), col(RQ_OFF // RET_DK)),
            pl.BlockSpec((None, L, RET_DK), col(RK_OFF // RET_DK)),
            pl.BlockSpec((None, L, RET_DV), col(RV_OFF // RET_DV)),
            pl.BlockSpec((None, L, RET_DV), col(RG_OFF // RET_DV)),
            pl.BlockSpec((None, Lc, RET_DK), col(0)),
            pl.BlockSpec((None, Lc, RET_DV), col(RET_HEADS)),
            pl.BlockSpec((2, None, 1, RET_CHUNK), lambda b, h: (0, h, 0, 0)),
            pl.BlockSpec((1, RET_DV), lambda b, h: (0, h)),
        ],
        out_specs=pl.BlockSpec((None, L, RET_DV), col(0)),
        out_shape=jax.ShapeDtypeStruct((B, L, RET_W), BF16),
        scratch_shapes=[pltpu.VMEM((T, 2 * RET_DK, RET_DV), F32),
                        pltpu.VMEM((T, 2 * RET_DK, RET_DV), BF16)],
        compiler_params=pltpu.CompilerParams(dimension_semantics=("arbitrary", "arbitrary"),
                                             vmem_limit_bytes=VMEM_LIMIT),
        name="retention",
    )(ret_in, ret_in, ret_in, ret_in, ctx_ret, ctx_ret, rate, gn_g)


def _attention_kernel(q_ref, k_ref, vt_ref, kc_ref, vtc_ref, o_ref):
    Lc = kc_ref.shape[0]
    L = k_ref.shape[0]
    tq = q_ref.shape[1]
    ctx_tiles = Lc // ATT_KEYS
    n_tiles = ctx_tiles + L // ATT_KEYS

    def key_tile(c):
        if c < ctx_tiles:
            return kc_ref[c * ATT_KEYS:(c + 1) * ATT_KEYS, :]
        return k_ref[(c - ctx_tiles) * ATT_KEYS:(c - ctx_tiles + 1) * ATT_KEYS, :]

    def value_tile(c):
        if c < ctx_tiles:
            return vtc_ref[:, c * ATT_KEYS:(c + 1) * ATT_KEYS]
        return vt_ref[:, (c - ctx_tiles) * ATT_KEYS:(c - ctx_tiles + 1) * ATT_KEYS]

    streams = [(h, g) for h in range(tq // MXU_DIM) for g in range(ATT_GROUP)]
    pad = jnp.zeros((LANES - ATT_HD, MXU_DIM), BF16)

    def q_weights(h, g):
        return jnp.concatenate(
            [q_ref[g * ATT_HD:(g + 1) * ATT_HD, h * MXU_DIM:(h + 1) * MXU_DIM], pad], axis=0)

    wq = {st: q_weights(*st) for st in streams}

    def scores(c, st):
        s = _dot(key_tile(c), wq[st])
        return s, jnp.max(s, axis=0, keepdims=True)

    m = {}
    acc = {}

    def fold(c, st, s_mt):
        s, mt = s_mt
        v_c = value_tile(c)
        if c == 0:
            m[st] = mt
            acc[st] = _dot(v_c, jnp.exp2(s - mt).astype(BF16))
        else:
            m_new = jnp.maximum(m[st], mt)
            alpha = jnp.exp2(m[st] - m_new)
            acc[st] = acc[st] * alpha + _dot(v_c, jnp.exp2(s - m_new).astype(BF16))
            m[st] = m_new

    units = [(c, st) for c in range(n_tiles) for st in streams]
    pending = []
    for i in range(len(units) + ATT_AHEAD):
        if i < len(units):
            pending.append(scores(*units[i]))
        if i >= ATT_AHEAD:
            fold(*units[i - ATT_AHEAD], pending.pop(0))
    for h, g in streams:
        a = acc[(h, g)]
        o_ref[g * ATT_HD:(g + 1) * ATT_HD, h * MXU_DIM:(h + 1) * MXU_DIM] = (
            a[:ATT_HD] / a[ATT_HD:ATT_HD + 1]).astype(o_ref.dtype)


def _attention_call(aq_t, ks, vt, ctx_ks, ctx_vt):
    B, _, L = aq_t.shape
    Lc = ctx_ks.shape[2]
    tq = TQ_ATT
    assert L % ATT_KEYS == 0 and Lc % ATT_KEYS == 0 and tq % MXU_DIM == 0
    per_head = lambda b, kv, i: (b, kv, 0, 0)
    return pl.pallas_call(
        _attention_kernel,
        grid=(B, ATT_KV_HEADS, L // tq),
        in_specs=[
            pl.BlockSpec((None, ATT_GROUP * ATT_HD, tq), lambda b, kv, i: (b, kv, i)),
            pl.BlockSpec((None, None, L, LANES), per_head),
            pl.BlockSpec((None, None, ATT_VT_ROWS, L), per_head),
            pl.BlockSpec((None, None, Lc, LANES), per_head),
            pl.BlockSpec((None, None, ATT_VT_ROWS, Lc), per_head),
        ],
        out_specs=pl.BlockSpec((None, ATT_GROUP * ATT_HD, tq), lambda b, kv, i: (b, kv, i)),
        out_shape=jax.ShapeDtypeStruct((B, ATT_W, L), BF16),
        compiler_params=pltpu.CompilerParams(
            dimension_semantics=("arbitrary", "arbitrary", "arbitrary"),
            vmem_limit_bytes=VMEM_LIMIT),
        name="attention",
    )(aq_t, ks, vt, ctx_ks, ctx_vt)


def _out_ffn_kernel(x_ref, ret_ref, att_ref, g1_ref, sh2_ref, sc2_ref, g2_ref, n2_ref, nf_ref,
                    wo_r_ref, wo_a_ref, w1_ref, w2_ref, o_ref):
    mix = _dot(ret_ref[...], wo_r_ref[...]) + _dot_tn(att_ref[...], wo_a_ref[...])
    x1 = x_ref[...] + g1_ref[...] * mix
    h2 = _norm_modulate(x1, n2_ref[...], sh2_ref[...], sc2_ref[...]).astype(BF16)
    ff = jnp.zeros(x1.shape, F32)
    for j in range(D_FF // FF_CHUNK):
        a = jnp.maximum(_dot(h2, w1_ref[:, j * FF_CHUNK:(j + 1) * FF_CHUNK]), 0.0)
        ff = ff + _dot((a * a).astype(BF16), w2_ref[j * FF_CHUNK:(j + 1) * FF_CHUNK, :])
    x2 = x1 + g2_ref[...] * ff
    ms = jnp.mean(x2 * x2, axis=-1, keepdims=True)
    o_ref[...] = x2 * lax.rsqrt(ms + EPS) * nf_ref[...]


def _out_ffn_call(x, ret, att, mod3, norm2_g, final_g, wo_r, wo_a, w1, w2):
    B, L, D = x.shape
    tm = TM_PROJ
    const = lambda b, i: (0, 0)
    once = pl.Buffered(1)

    def modrow(j):
        return pl.BlockSpec((None, 1, D), lambda b, i: (b, 0, j))

    return pl.pallas_call(
        _out_ffn_kernel,
        grid=(B, L // tm),
        in_specs=[
            pl.BlockSpec((None, tm, D), lambda b, i: (b, i, 0)),
            pl.BlockSpec((None, tm, RET_W), lambda b, i: (b, i, 0)),
            pl.BlockSpec((None, ATT_W, tm), lambda b, i: (b, 0, i)),
            modrow(2), modrow(3), modrow(4), modrow(5),
            pl.BlockSpec((1, D), const),
            pl.BlockSpec((1, D), const),
            pl.BlockSpec(wo_r.shape, const, pipeline_mode=once),
            pl.BlockSpec(wo_a.shape, const, pipeline_mode=once),
            pl.BlockSpec(w1.shape, const, pipeline_mode=once),
            pl.BlockSpec(w2.shape, const, pipeline_mode=once),
        ],
        out_specs=pl.BlockSpec((None, tm, D), lambda b, i: (b, i, 0)),
        out_shape=jax.ShapeDtypeStruct((B, L, D), F32),
        compiler_params=pltpu.CompilerParams(dimension_semantics=("arbitrary", "arbitrary"),
                                             vmem_limit_bytes=VMEM_LIMIT),
        name="out_ffn",
    )(x, ret, att, mod3, mod3, mod3, mod3, norm2_g, final_g, wo_r, wo_a, w1, w2)


def _freqs(n_pairs):
    return ROPE_BASE ** (-jnp.arange(n_pairs, dtype=F32) / n_pairs)


def _rope_tables(L):
    t = jnp.arange(L, dtype=F32)
    ang = t[:, None] * _freqs(RET_DK // 2)
    cr = jnp.concatenate([jnp.cos(ang), jnp.cos(ang)], axis=-1)
    sr = jnp.concatenate([-jnp.sin(ang), jnp.sin(ang)], axis=-1)
    rows = L // GRID_W
    row = jnp.repeat(jnp.arange(rows, dtype=F32), GRID_W)
    col = jnp.tile(jnp.arange(GRID_W, dtype=F32), rows)
    af = _freqs(ATT_HD // 4)
    aang = jnp.concatenate([row[:, None] * af, col[:, None] * af], axis=-1)
    c, s = jnp.cos(aang), jnp.sin(aang)
    ca = jnp.concatenate([c, c, c, c], axis=-1)
    sa = jnp.concatenate([-s, s, -s, s], axis=-1)
    return cr, sr, ca, sa


def kernel(x, c, ctx, c_ctx, w_mod, b_mod, norm1_g, norm2_g, w_in, w_out, ret_log_rate, ret_gn_g,
           q_norm_g, k_norm_g, w_ff1, w_ff2, final_norm_g):
    B, L, D = x.shape
    assert w_mod.shape[0] == 1, "single-layer configuration"
    assert B + 1 <= MOD_ROWS and L % TM_PROJ == 0 and L % TM_IN == 0 and L % TQ_ATT == 0 and L % RET_CHUNK == 0

    c_rows = jnp.zeros((MOD_ROWS, D), F32).at[:B].set(c).at[B].set(c_ctx)
    mod = _mod_call(c_rows, w_mod[0], b_mod[0][None, :])
    mod3 = mod.reshape(MOD_ROWS, 1, 6 * D)

    w_in_b = w_in[0].astype(BF16)
    w_rkv = w_in_b[:, RK_OFF:RG_OFF]
    w_akv = w_in_b[:, AK_OFF:D_IN]
    wo = w_out[0].astype(BF16)
    w1 = w_ff1[0].astype(BF16)
    w2 = w_ff2[0].astype(BF16)
    n1 = norm1_g[0][None, :]
    n2 = norm2_g[0][None, :]
    nf = final_norm_g[None, :]
    gq2 = jnp.tile(q_norm_g[0], LANES // ATT_HD)[None, :]
    gk2 = jnp.tile(k_norm_g[0], LANES // ATT_HD)[None, :]
    gn = ret_gn_g[0][None, :]
    rate = jnp.broadcast_to(ret_log_rate[0].astype(F32)[:, :, None, None],
                            (2, RET_HEADS, 1, RET_CHUNK))
    cr, sr, ca, sa = _rope_tables(L)

    ctx_ret, ctx_ks, ctx_vt = _ctx_proj_call(ctx, mod3, n1, w_rkv, w_akv, gk2)
    ret_in, aq_t, ks, vt = _in_proj_call(x, mod3, n1, w_in_b, cr, sr, ca, sa, gq2, gk2)
    ret = _retention_call(ret_in, ctx_ret, rate, gn)
    att = _attention_call(aq_t, ks, vt, ctx_ks, ctx_vt)
    return _out_ffn_call(x, ret, att, mod3, n2, nf, wo[:RET_W], wo[RET_W:], w1, w2)
```

```python
import math

import jax
import jax.numpy as jnp
from jax import lax
from jax.experimental import pallas as pl
from jax.experimental.pallas import tpu as pltpu

D_MODEL = 1024
CTX_LEN = 256
GRID_W = 64

RET_HEADS = 4
RET_DK = 128
RET_DV = 128
RET_W = RET_HEADS * RET_DV
RET_SCALE = RET_DK ** -0.5

ATT_HEADS = 8
ATT_KV_HEADS = 2
ATT_GROUP = ATT_HEADS // ATT_KV_HEADS
ATT_HD = 64
ATT_W = ATT_HEADS * ATT_HD
ATT_SCALE = ATT_HD ** -0.5

MIX_W = RET_W + ATT_W
D_FF = 4 * D_MODEL
ROPE_BASE = 10000.0
EPS = 1e-6
LOG2E = math.log2(math.e)

RQ_OFF = 0
RK_OFF = RQ_OFF + RET_HEADS * RET_DK
RV_OFF = RK_OFF + RET_HEADS * RET_DK
RG_OFF = RV_OFF + RET_W
AQ_OFF = RG_OFF + RET_W
AK_OFF = AQ_OFF + ATT_W
AV_OFF = AK_OFF + ATT_KV_HEADS * ATT_HD
D_IN = AV_OFF + ATT_KV_HEADS * ATT_HD

LANES = 128
BF16_SUBLANES = 16
ATT_VT_ROWS = ATT_HD + BF16_SUBLANES
MXU_DIM = 256
VMEM_LIMIT = 56 * 1024 * 1024

MOD_ROWS = 16
TM_CTX = 512
TM_IN = 512
TM_PROJ = 512
TQ_ATT = 1024
ATT_KEYS = 256
ATT_AHEAD = 6
RET_CHUNK = 256
FF_CHUNK = 1024

F32 = jnp.float32
BF16 = jnp.bfloat16


def _dot(a, b):
    return jnp.dot(a, b, preferred_element_type=F32)


def _dot_nt(a, b):
    return lax.dot_general(a, b, (((1,), (1,)), ((), ())), preferred_element_type=F32)


def _dot_tn(a, b):
    return lax.dot_general(a, b, (((0,), (0,)), ((), ())), preferred_element_type=F32)


def _silu(x):
    return x * jax.nn.sigmoid(x)


def _head_mean_matrix(n):
    r = lax.broadcasted_iota(jnp.int32, (n, n), 0) // ATT_HD
    c = lax.broadcasted_iota(jnp.int32, (n, n), 1) // ATT_HD
    return jnp.where(r == c, 1.0 / ATT_HD, 0.0).astype(BF16)


def _head_rms_scale(blk, bd):
    ms = _dot((blk * blk).astype(BF16), bd)
    return lax.rsqrt(ms + EPS)


def _rope_ret(blk, c2, s2):
    return blk * c2 + pltpu.roll(blk, RET_DK // 2, 1) * s2


def _rope_att(blk, c2, s2, first_half):
    up = pltpu.roll(blk, ATT_HD // 2, 1)
    dn = pltpu.roll(blk, LANES - ATT_HD // 2, 1)
    return blk * c2 + jnp.where(first_half, dn, up) * s2


def _split_bf16(x):
    hi = x.astype(BF16)
    return hi, (x - hi.astype(F32)).astype(BF16)


def _mod_kernel(c_ref, w_ref, b_ref, o_ref):
    a_hi, a_lo = _split_bf16(_silu(c_ref[...]))
    w_hi, w_lo = _split_bf16(w_ref[...])
    by_hi = _dot(jnp.concatenate([a_hi, a_lo], axis=0), w_hi)
    o_ref[...] = by_hi[:MOD_ROWS] + by_hi[MOD_ROWS:] + _dot(a_hi, w_lo) + b_ref[...]


def _mod_call(c_rows, w_mod, b_mod):
    n = w_mod.shape[1]
    bn = 1024
    return pl.pallas_call(
        _mod_kernel,
        grid=(n // bn,),
        in_specs=[
            pl.BlockSpec((MOD_ROWS, D_MODEL), lambda j: (0, 0)),
            pl.BlockSpec((D_MODEL, bn), lambda j: (0, j)),
            pl.BlockSpec((1, bn), lambda j: (0, j)),
        ],
        out_specs=pl.BlockSpec((MOD_ROWS, bn), lambda j: (0, j)),
        out_shape=jax.ShapeDtypeStruct((MOD_ROWS, n), F32),
        compiler_params=pltpu.CompilerParams(dimension_semantics=("arbitrary",),
                                             vmem_limit_bytes=VMEM_LIMIT),
        name="mod",
    )(c_rows, w_mod, b_mod)


def _norm_modulate(x, g, sh, sc):
    ms = jnp.mean(x * x, axis=-1, keepdims=True)
    y = x * lax.rsqrt(ms + EPS) * g
    return y * (1.0 + sc) + sh


def _store_attention_kv(k, v, ks_ref, vt_ref):
    rows = k.shape[0]
    low = lax.broadcasted_iota(jnp.int32, k.shape, 1) < ATT_HD
    k_sw = pltpu.roll(k, ATT_HD, 1)
    ks_ref[0] = jnp.where(low, k, k_sw).astype(BF16)
    ks_ref[1] = jnp.where(low, k_sw, k).astype(BF16)
    v_t = v.T
    ones = jnp.ones((BF16_SUBLANES, rows), BF16)
    for j in range(ATT_KV_HEADS):
        vt_ref[j, :ATT_HD, :] = v_t[j * ATT_HD:(j + 1) * ATT_HD].astype(BF16)
        vt_ref[j, ATT_HD:, :] = ones


def _ctx_proj_kernel(x_ref, sh_ref, sc_ref, g_ref, wr_ref, wa_ref, gk_ref, ret_ref, ks_ref, vt_ref):
    h = _norm_modulate(x_ref[...], g_ref[...], sh_ref[...], sc_ref[...]).astype(BF16)
    pr = _dot(h, wr_ref[...])
    ret_ref[:, :RET_W] = (pr[:, :RET_W] * RET_SCALE).astype(BF16)
    ret_ref[:, RET_W:] = pr[:, RET_W:].astype(BF16)
    pa = _dot(h, wa_ref[...])
    ak = pa[:, :LANES]
    r = _head_rms_scale(ak, _head_mean_matrix(LANES))
    _store_attention_kv(ak * r * gk_ref[...], pa[:, LANES:], ks_ref, vt_ref)


def _ctx_proj_call(ctx_rows, ctx_row, mod3, norm1_g, w_rkv, w_akv, gk2):
    N, D = ctx_rows.shape
    tm = TM_CTX
    const = lambda i: (0, 0)
    return pl.pallas_call(
        _ctx_proj_kernel,
        grid=(N // tm,),
        in_specs=[
            pl.BlockSpec((tm, D), lambda i: (i, 0)),
            pl.BlockSpec((None, 1, D), lambda i: (ctx_row, 0, 0)),
            pl.BlockSpec((None, 1, D), lambda i: (ctx_row, 0, 1)),
            pl.BlockSpec((1, D), const),
            pl.BlockSpec(w_rkv.shape, const),
            pl.BlockSpec(w_akv.shape, const),
            pl.BlockSpec((1, LANES), const),
        ],
        out_specs=[
            pl.BlockSpec((tm, 2 * RET_W), lambda i: (i, 0)),
            pl.BlockSpec((ATT_KV_HEADS, tm, LANES), lambda i: (0, i, 0)),
            pl.BlockSpec((ATT_KV_HEADS, ATT_VT_ROWS, tm), lambda i: (0, 0, i)),
        ],
        out_shape=[
            jax.ShapeDtypeStruct((N, 2 * RET_W), BF16),
            jax.ShapeDtypeStruct((ATT_KV_HEADS, N, LANES), BF16),
            jax.ShapeDtypeStruct((ATT_KV_HEADS, ATT_VT_ROWS, N), BF16),
        ],
        compiler_params=pltpu.CompilerParams(dimension_semantics=("arbitrary",),
                                             vmem_limit_bytes=VMEM_LIMIT),
        name="ctx_proj",
    )(ctx_rows, mod3, mod3, norm1_g, w_rkv, w_akv, gk2)


def _in_proj_kernel(x_ref, sh_ref, sc_ref, g_ref, w_ref, cr_ref, sr_ref, ca_ref, sa_ref,
                    gq_ref, gk_ref, ret_ref, aq_ref, ks_ref, vt_ref):
    tm = x_ref.shape[0]
    h = _norm_modulate(x_ref[...], g_ref[...], sh_ref[...], sc_ref[...]).astype(BF16)
    cr, sr = cr_ref[...], sr_ref[...]
    ca, sa = ca_ref[...], sa_ref[...]
    lane = lax.broadcasted_iota(jnp.int32, (tm, LANES), 1)
    first_half = (lane & (ATT_HD // 2)) == 0

    bd = _head_mean_matrix(MXU_DIM)
    gq, gk = gq_ref[...], gk_ref[...]
    pa = _dot(h, w_ref[:, AQ_OFF:AK_OFF])
    for j in range(ATT_W // MXU_DIM):
        blk = pa[:, j * MXU_DIM:(j + 1) * MXU_DIM]
        r = _head_rms_scale(blk, bd) * (ATT_SCALE * LOG2E)
        for i in range(MXU_DIM // LANES):
            sl = slice(i * LANES, (i + 1) * LANES)
            o = _rope_att(blk[:, sl] * gq, ca, sa, first_half) * r[:, sl]
            aq_ref[j * MXU_DIM + i * LANES:j * MXU_DIM + (i + 1) * LANES, :] = o.T.astype(BF16)

    pkv = _dot(h, w_ref[:, AK_OFF:D_IN])
    ak = pkv[:, :LANES]
    rk = _head_rms_scale(ak, bd[:LANES, :LANES])
    _store_attention_kv(_rope_att(ak * gk, ca, sa, first_half) * rk, pkv[:, LANES:], ks_ref, vt_ref)

    pq = _dot(h, w_ref[:, RQ_OFF:RK_OFF])
    for hh in range(RET_HEADS):
        sl = slice(hh * RET_DK, (hh + 1) * RET_DK)
        ret_ref[:, RQ_OFF + hh * RET_DK:RQ_OFF + (hh + 1) * RET_DK] = _rope_ret(pq[:, sl], cr, sr).astype(BF16)
    pk = _dot(h, w_ref[:, RK_OFF:RV_OFF])
    for hh in range(RET_HEADS):
        sl = slice(hh * RET_DK, (hh + 1) * RET_DK)
        ret_ref[:, RK_OFF + hh * RET_DK:RK_OFF + (hh + 1) * RET_DK] = (
            _rope_ret(pk[:, sl], cr, sr) * RET_SCALE).astype(BF16)
    ret_ref[:, RV_OFF:AQ_OFF] = _dot(h, w_ref[:, RV_OFF:AQ_OFF]).astype(BF16)


def _in_proj_call(x, mod3, norm1_g, w_in, cr, sr, ca, sa, gq2, gk2):
    B, L, D = x.shape
    tm = TM_IN
    const = lambda b, i: (0, 0)
    tok = lambda b, i: (i, 0)
    return pl.pallas_call(
        _in_proj_kernel,
        grid=(B, L // tm),
        in_specs=[
            pl.BlockSpec((None, tm, D), lambda b, i: (b, i, 0)),
            pl.BlockSpec((None, 1, D), lambda b, i: (b, 0, 0)),
            pl.BlockSpec((None, 1, D), lambda b, i: (b, 0, 1)),
            pl.BlockSpec((1, D), const),
            pl.BlockSpec(w_in.shape, const),
            pl.BlockSpec((tm, LANES), tok),
            pl.BlockSpec((tm, LANES), tok),
            pl.BlockSpec((tm, LANES), tok),
            pl.BlockSpec((tm, LANES), tok),
            pl.BlockSpec((1, LANES), const),
            pl.BlockSpec((1, LANES), const),
        ],
        out_specs=[
            pl.BlockSpec((None, tm, AQ_OFF), lambda b, i: (b, i, 0)),
            pl.BlockSpec((None, ATT_W, tm), lambda b, i: (b, 0, i)),
            pl.BlockSpec((None, ATT_KV_HEADS, tm, LANES), lambda b, i: (b, 0, i, 0)),
            pl.BlockSpec((None, ATT_KV_HEADS, ATT_VT_ROWS, tm), lambda b, i: (b, 0, 0, i)),
        ],
        out_shape=[
            jax.ShapeDtypeStruct((B, L, AQ_OFF), BF16),
            jax.ShapeDtypeStruct((B, ATT_W, L), BF16),
            jax.ShapeDtypeStruct((B, ATT_KV_HEADS, L, LANES), BF16),
            jax.ShapeDtypeStruct((B, ATT_KV_HEADS, ATT_VT_ROWS, L), BF16),
        ],
        compiler_params=pltpu.CompilerParams(dimension_semantics=("arbitrary", "arbitrary"),
                                             vmem_limit_bytes=VMEM_LIMIT),
        name="in_proj",
    )(x, mod3, mod3, norm1_g, w_in, cr, sr, ca, sa, gq2, gk2)


def _retention_kernel(q_ref, k_ref, v_ref, g_ref, kc_ref, vc_ref, rate_ref, gn_ref, o_ref,
                      u_ref, s_ref):
    L = q_ref.shape[0]
    Lc = kc_ref.shape[0]
    C = RET_CHUNK
    T = L // C

    lg = jnp.log1p(-jnp.exp(rate_ref[...]))
    lgf, lgb = lg[0], lg[1]
    lgf1, lgb1 = lgf[:, :RET_DK], lgb[:, :RET_DK]

    ri = lax.broadcasted_iota(jnp.int32, (C, C), 0)
    ci = lax.broadcasted_iota(jnp.int32, (C, C), 1)
    rel = (ri - ci).astype(F32)
    decay = (jnp.where(rel >= 0, jnp.exp(lgf * jnp.maximum(rel, 0.0)), 0.0)
             + jnp.where(rel <= 0, jnp.exp(lgb * jnp.maximum(-rel, 0.0)), 0.0))

    pos = lax.broadcasted_iota(jnp.int32, (C, RET_DK), 0).astype(F32)
    q_dec_f = jnp.exp(lgf1 * (pos + 1.0))
    q_dec_b = jnp.exp(lgb1 * (C - pos))
    k_dec_f = jnp.exp(lgf1 * (C - 1.0 - pos))
    k_dec_b = jnp.exp(lgb1 * pos)
    c_dec_f = jnp.exp(lgf1 * C)
    c_dec_b = jnp.exp(lgb1 * C)

    cpos = lax.broadcasted_iota(jnp.int32, (Lc, RET_DK), 0).astype(F32)
    kc = kc_ref[...].astype(F32)
    vc = vc_ref[...]
    s_f0 = _dot_tn((kc * jnp.exp(lgf1 * (Lc - 1.0 - cpos))).astype(BF16), vc)
    s_b0 = _dot_tn((kc * jnp.exp(lgb1 * cpos)).astype(BF16), vc)

    def rows(t):
        return slice(t * C, (t + 1) * C)

    for t in range(T):
        kf = k_ref[rows(t), :].astype(F32)
        kd = jnp.concatenate([(kf * k_dec_f).astype(BF16), (kf * k_dec_b).astype(BF16)], axis=1)
        u_ref[t] = _dot_tn(kd, v_ref[rows(t), :])

    s_f, s_b = s_f0, s_b0
    for t in range(T):
        tb = T - 1 - t
        s_ref[t, :RET_DK, :] = s_f.astype(BF16)
        s_ref[tb, RET_DK:, :] = s_b.astype(BF16)
        s_f = s_f * c_dec_f + u_ref[t, :RET_DK, :]
        s_b = s_b * c_dec_b + u_ref[tb, RET_DK:, :]

    gn = gn_ref[...]

    def qk(t):
        return _dot_nt(q_ref[rows(t), :], k_ref[rows(t), :])

    sc_next = qk(0)
    for t in range(T):
        sc = sc_next
        if t + 1 < T:
            sc_next = qk(t + 1)
        qf = q_ref[rows(t), :].astype(F32)
        lhs = jnp.concatenate([(sc * decay).astype(BF16), (qf * q_dec_f).astype(BF16),
                               (qf * q_dec_b).astype(BF16)], axis=1)
        rhs = jnp.concatenate([v_ref[rows(t), :], s_ref[t]], axis=0)
        o = _dot(lhs, rhs)
        mu = jnp.mean(o, axis=-1, keepdims=True)
        d = o - mu
        var = jnp.mean(d * d, axis=-1, keepdims=True)
        on = d * lax.rsqrt(var + EPS) * gn
        o_ref[rows(t), :] = (on * _silu(g_ref[rows(t), :].astype(F32))).astype(o_ref.dtype)


def _retention_call(ret_in, ctx_ret, rate, gn_g):
    B, L, _ = ret_in.shape
    Lc = ctx_ret.shape[0] // B
    H = RET_HEADS
    T = L // RET_CHUNK

    def col(off):
        return lambda b, h: (b, 0, off + h)

    return pl.pallas_call(
        _retention_kernel,
        grid=(B, H),
        in_specs=[
            pl.BlockSpec((None, L, RET_DK), col(RQ_OFF // RET_DK)),
            pl.BlockSpec((None, L, RET_DK), col(RK_OFF // RET_DK)),
            pl.BlockSpec((None, L, RET_DV), col(RV_OFF // RET_DV)),
            pl.BlockSpec((None, L, RET_DV), col(RG_OFF // RET_DV)),
            pl.BlockSpec((Lc, RET_DK), lambda b, h: (b, h)),
            pl.BlockSpec((Lc, RET_DV), lambda b, h: (b, RET_HEADS + h)),
            pl.BlockSpec((2, None, 1, RET_CHUNK), lambda b, h: (0, h, 0, 0)),
            pl.BlockSpec((1, RET_DV), lambda b, h: (0, h)),
        ],
        out_specs=pl.BlockSpec((None, L, RET_DV), col(0)),
        out_shape=jax.ShapeDtypeStruct((B, L, RET_W), BF16),
        scratch_shapes=[pltpu.VMEM((T, 2 * RET_DK, RET_DV), F32),
                        pltpu.VMEM((T, 2 * RET_DK, RET_DV), BF16)],
        compiler_params=pltpu.CompilerParams(dimension_semantics=("arbitrary", "arbitrary"),
                                             vmem_limit_bytes=VMEM_LIMIT),
        name="retention",
    )(ret_in, ret_in, ret_in, ret_in, ctx_ret, ctx_ret, rate, gn_g)


def _attention_kernel(q_ref, k_ref, vt_ref, kc_ref, vtc_ref, o_ref):
    Lc = kc_ref.shape[0]
    L = k_ref.shape[0]
    tq = q_ref.shape[1]
    ctx_tiles = Lc // ATT_KEYS
    n_tiles = ctx_tiles + L // ATT_KEYS

    def key_tile(c):
        if c < ctx_tiles:
            return kc_ref[c * ATT_KEYS:(c + 1) * ATT_KEYS, :]
        return k_ref[(c - ctx_tiles) * ATT_KEYS:(c - ctx_tiles + 1) * ATT_KEYS, :]

    def value_tile(c):
        if c < ctx_tiles:
            return vtc_ref[:, c * ATT_KEYS:(c + 1) * ATT_KEYS]
        return vt_ref[:, (c - ctx_tiles) * ATT_KEYS:(c - ctx_tiles + 1) * ATT_KEYS]

    streams = [(h, g) for h in range(tq // MXU_DIM) for g in range(ATT_GROUP)]
    pad = jnp.zeros((LANES - ATT_HD, MXU_DIM), BF16)

    def q_weights(h, g):
        return jnp.concatenate(
            [q_ref[g * ATT_HD:(g + 1) * ATT_HD, h * MXU_DIM:(h + 1) * MXU_DIM], pad], axis=0)

    wq = {st: q_weights(*st) for st in streams}

    def scores(c, st):
        s = _dot(key_tile(c), wq[st])
        return s, jnp.max(s, axis=0, keepdims=True)

    m = {}
    acc = {}

    def fold(c, st, s_mt):
        s, mt = s_mt
        v_c = value_tile(c)
        if c == 0:
            m[st] = mt
            acc[st] = _dot(v_c, jnp.exp2(s - mt).astype(BF16))
        else:
            m_new = jnp.maximum(m[st], mt)
            alpha = jnp.exp2(m[st] - m_new)
            acc[st] = acc[st] * alpha + _dot(v_c, jnp.exp2(s - m_new).astype(BF16))
            m[st] = m_new

    units = [(c, st) for c in range(n_tiles) for st in streams]
    pending = []
    for i in range(len(units) + ATT_AHEAD):
        if i < len(units):
            pending.append(scores(*units[i]))
        if i >= ATT_AHEAD:
            fold(*units[i - ATT_AHEAD], pending.pop(0))
    for h, g in streams:
        a = acc[(h, g)]
        o_ref[g * ATT_HD:(g + 1) * ATT_HD, h * MXU_DIM:(h + 1) * MXU_DIM] = (
            a[:ATT_HD] / a[ATT_HD:ATT_HD + 1]).astype(o_ref.dtype)


def _attention_call(aq_t, ks, vt, ctx_ks, ctx_vt):
    B, _, L = aq_t.shape
    Lc = ctx_ks.shape[1] // B
    tq = TQ_ATT
    assert L % ATT_KEYS == 0 and Lc % ATT_KEYS == 0 and tq % MXU_DIM == 0
    per_head = lambda b, kv, i: (b, kv, 0, 0)
    return pl.pallas_call(
        _attention_kernel,
        grid=(B, ATT_KV_HEADS, L // tq),
        in_specs=[
            pl.BlockSpec((None, ATT_GROUP * ATT_HD, tq), lambda b, kv, i: (b, kv, i)),
            pl.BlockSpec((None, None, L, LANES), per_head),
            pl.BlockSpec((None, None, ATT_VT_ROWS, L), per_head),
            pl.BlockSpec((None, Lc, LANES), lambda b, kv, i: (kv, b, 0)),
            pl.BlockSpec((None, ATT_VT_ROWS, Lc), lambda b, kv, i: (kv, 0, b)),
        ],
        out_specs=pl.BlockSpec((None, ATT_GROUP * ATT_HD, tq), lambda b, kv, i: (b, kv, i)),
        out_shape=jax.ShapeDtypeStruct((B, ATT_W, L), BF16),
        compiler_params=pltpu.CompilerParams(
            dimension_semantics=("arbitrary", "arbitrary", "arbitrary"),
            vmem_limit_bytes=VMEM_LIMIT),
        name="attention",
    )(aq_t, ks, vt, ctx_ks, ctx_vt)


def _out_ffn_kernel(x_ref, ret_ref, att_ref, g1_ref, sh2_ref, sc2_ref, g2_ref, n2_ref, nf_ref,
                    wo_r_ref, wo_a_ref, w1_ref, w2_ref, o_ref):
    mix = _dot(ret_ref[...], wo_r_ref[...]) + _dot_tn(att_ref[...], wo_a_ref[...])
    x1 = x_ref[...] + g1_ref[...] * mix
    h2 = _norm_modulate(x1, n2_ref[...], sh2_ref[...], sc2_ref[...]).astype(BF16)
    ff = jnp.zeros(x1.shape, F32)
    for j in range(D_FF // FF_CHUNK):
        a = jnp.maximum(_dot(h2, w1_ref[:, j * FF_CHUNK:(j + 1) * FF_CHUNK]), 0.0)
        ff = ff + _dot((a * a).astype(BF16), w2_ref[j * FF_CHUNK:(j + 1) * FF_CHUNK, :])
    x2 = x1 + g2_ref[...] * ff
    ms = jnp.mean(x2 * x2, axis=-1, keepdims=True)
    o_ref[...] = x2 * lax.rsqrt(ms + EPS) * nf_ref[...]


def _out_ffn_call(x, ret, att, mod3, norm2_g, final_g, wo_r, wo_a, w1, w2):
    B, L, D = x.shape
    tm = TM_PROJ
    const = lambda b, i: (0, 0)
    once = pl.Buffered(1)

    def modrow(j):
        return pl.BlockSpec((None, 1, D), lambda b, i: (b, 0, j))

    return pl.pallas_call(
        _out_ffn_kernel,
        grid=(B, L // tm),
        in_specs=[
            pl.BlockSpec((None, tm, D), lambda b, i: (b, i, 0)),
            pl.BlockSpec((None, tm, RET_W), lambda b, i: (b, i, 0)),
            pl.BlockSpec((None, ATT_W, tm), lambda b, i: (b, 0, i)),
            modrow(2), modrow(3), modrow(4), modrow(5),
            pl.BlockSpec((1, D), const),
            pl.BlockSpec((1, D), const),
            pl.BlockSpec(wo_r.shape, const, pipeline_mode=once),
            pl.BlockSpec(wo_a.shape, const, pipeline_mode=once),
            pl.BlockSpec(w1.shape, const, pipeline_mode=once),
            pl.BlockSpec(w2.shape, const, pipeline_mode=once),
        ],
        out_specs=pl.BlockSpec((None, tm, D), lambda b, i: (b, i, 0)),
        out_shape=jax.ShapeDtypeStruct((B, L, D), F32),
        compiler_params=pltpu.CompilerParams(dimension_semantics=("arbitrary", "arbitrary"),
                                             vmem_limit_bytes=VMEM_LIMIT),
        name="out_ffn",
    )(x, ret, att, mod3, mod3, mod3, mod3, norm2_g, final_g, wo_r, wo_a, w1, w2)


def _freqs(n_pairs):
    return ROPE_BASE ** (-jnp.arange(n_pairs, dtype=F32) / n_pairs)


def _rope_tables(L):
    t = jnp.arange(L, dtype=F32)
    ang = t[:, None] * _freqs(RET_DK // 2)
    cr = jnp.concatenate([jnp.cos(ang), jnp.cos(ang)], axis=-1)
    sr = jnp.concatenate([-jnp.sin(ang), jnp.sin(ang)], axis=-1)
    rows = L // GRID_W
    af = _freqs(ATT_HD // 4)
    row_ang = jnp.arange(rows, dtype=F32)[:, None] * af
    col_ang = jnp.arange(GRID_W, dtype=F32)[:, None] * af

    def over_grid(f):
        return jnp.concatenate([jnp.repeat(f(row_ang), GRID_W, axis=0),
                                jnp.tile(f(col_ang), (rows, 1))], axis=-1)

    c, s = over_grid(jnp.cos), over_grid(jnp.sin)
    ca = jnp.concatenate([c, c, c, c], axis=-1)
    sa = jnp.concatenate([-s, s, -s, s], axis=-1)
    return cr, sr, ca, sa


def kernel(x, c, ctx, c_ctx, w_mod, b_mod, norm1_g, norm2_g, w_in, w_out, ret_log_rate, ret_gn_g,
           q_norm_g, k_norm_g, w_ff1, w_ff2, final_norm_g):
    B, L, D = x.shape
    assert w_mod.shape[0] == 1, "single-layer configuration"
    assert B + 1 <= MOD_ROWS and L % TM_PROJ == 0 and L % TM_IN == 0 and L % TQ_ATT == 0 and L % RET_CHUNK == 0

    c_rows = jnp.zeros((MOD_ROWS, D), F32).at[:B].set(c).at[B].set(c_ctx)
    mod = _mod_call(c_rows, w_mod[0], b_mod[0][None, :])
    mod3 = mod.reshape(MOD_ROWS, 1, 6 * D)

    w_in_b = w_in[0].astype(BF16)
    w_rkv = w_in_b[:, RK_OFF:RG_OFF]
    w_akv = w_in_b[:, AK_OFF:D_IN]
    wo = w_out[0].astype(BF16)
    w1 = w_ff1[0].astype(BF16)
    w2 = w_ff2[0].astype(BF16)
    n1 = norm1_g[0][None, :]
    n2 = norm2_g[0][None, :]
    nf = final_norm_g[None, :]
    gq2 = jnp.tile(q_norm_g[0], LANES // ATT_HD)[None, :]
    gk2 = jnp.tile(k_norm_g[0], LANES // ATT_HD)[None, :]
    gn = ret_gn_g[0][None, :]
    rate = jnp.broadcast_to(ret_log_rate[0].astype(F32)[:, :, None, None],
                            (2, RET_HEADS, 1, RET_CHUNK))
    cr, sr, ca, sa = _rope_tables(L)

    Lc = ctx.shape[1]
    assert (B * Lc) % TM_CTX == 0
    ctx_ret, ctx_ks, ctx_vt = _ctx_proj_call(ctx.reshape(B * Lc, D), B, mod3, n1, w_rkv, w_akv, gk2)
    ret_in, aq_t, ks, vt = _in_proj_call(x, mod3, n1, w_in_b, cr, sr, ca, sa, gq2, gk2)
    ret = _retention_call(ret_in, ctx_ret, rate, gn)
    att = _attention_call(aq_t, ks, vt, ctx_ks, ctx_vt)
    return _out_ffn_call(x, ret, att, mod3, n2, nf, wo[:RET_W], wo[RET_W:], w1, w2)
```

```python
import math

import jax
import jax.numpy as jnp
from jax import lax
from jax.experimental import pallas as pl
from jax.experimental.pallas import tpu as pltpu

D_MODEL = 1024
CTX_LEN = 256
GRID_W = 64

RET_HEADS = 4
RET_DK = 128
RET_DV = 128
RET_W = RET_HEADS * RET_DV
RET_SCALE = RET_DK ** -0.5

ATT_HEADS = 8
ATT_KV_HEADS = 2
ATT_GROUP = ATT_HEADS // ATT_KV_HEADS
ATT_HD = 64
ATT_W = ATT_HEADS * ATT_HD
ATT_SCALE = ATT_HD ** -0.5

MIX_W = RET_W + ATT_W
D_FF = 4 * D_MODEL
ROPE_BASE = 10000.0
EPS = 1e-6
LOG2E = math.log2(math.e)

RQ_OFF = 0
RK_OFF = RQ_OFF + RET_HEADS * RET_DK
RV_OFF = RK_OFF + RET_HEADS * RET_DK
RG_OFF = RV_OFF + RET_W
AQ_OFF = RG_OFF + RET_W
AK_OFF = AQ_OFF + ATT_W
AV_OFF = AK_OFF + ATT_KV_HEADS * ATT_HD
D_IN = AV_OFF + ATT_KV_HEADS * ATT_HD

LANES = 128
BF16_SUBLANES = 16
ATT_VT_ROWS = ATT_HD + BF16_SUBLANES
MXU_DIM = 256
VMEM_LIMIT = 56 * 1024 * 1024

MOD_ROWS = 16
TM_CTX = 512
TM_IN = 512
TM_PROJ = 1024
TM_SUB = 512
TQ_ATT = 1024
ATT_KEYS = 256
ATT_AHEAD = 6
RET_CHUNK = 256
FF_CHUNK = 1024

F32 = jnp.float32
BF16 = jnp.bfloat16


def _dot(a, b):
    return jnp.dot(a, b, preferred_element_type=F32)


def _dot_nt(a, b):
    return lax.dot_general(a, b, (((1,), (1,)), ((), ())), preferred_element_type=F32)


def _dot_tn(a, b):
    return lax.dot_general(a, b, (((0,), (0,)), ((), ())), preferred_element_type=F32)


def _silu(x):
    return x * jax.nn.sigmoid(x)


def _head_mean_matrix(n):
    r = lax.broadcasted_iota(jnp.int32, (n, n), 0) // ATT_HD
    c = lax.broadcasted_iota(jnp.int32, (n, n), 1) // ATT_HD
    return jnp.where(r == c, 1.0 / ATT_HD, 0.0).astype(BF16)


def _head_rms_scale(blk, bd):
    ms = _dot((blk * blk).astype(BF16), bd)
    return lax.rsqrt(ms + EPS)


def _rope_ret(blk, c2, s2):
    return blk * c2 + pltpu.roll(blk, RET_DK // 2, 1) * s2


def _rope_att(blk, c2, s2, first_half):
    up = pltpu.roll(blk, ATT_HD // 2, 1)
    dn = pltpu.roll(blk, LANES - ATT_HD // 2, 1)
    return blk * c2 + jnp.where(first_half, dn, up) * s2


def _split_bf16(x):
    hi = x.astype(BF16)
    return hi, (x - hi.astype(F32)).astype(BF16)


def _mod_kernel(c_ref, w_ref, b_ref, o_ref):
    a_hi, a_lo = _split_bf16(_silu(c_ref[...]))
    w_hi, w_lo = _split_bf16(w_ref[...])
    by_hi = _dot(jnp.concatenate([a_hi, a_lo], axis=0), w_hi)
    o_ref[...] = by_hi[:MOD_ROWS] + by_hi[MOD_ROWS:] + _dot(a_hi, w_lo) + b_ref[...]


def _mod_call(c_rows, w_mod, b_mod):
    n = w_mod.shape[1]
    bn = 1024
    return pl.pallas_call(
        _mod_kernel,
        grid=(n // bn,),
        in_specs=[
            pl.BlockSpec((MOD_ROWS, D_MODEL), lambda j: (0, 0)),
            pl.BlockSpec((D_MODEL, bn), lambda j: (0, j)),
            pl.BlockSpec((1, bn), lambda j: (0, j)),
        ],
        out_specs=pl.BlockSpec((MOD_ROWS, bn), lambda j: (0, j)),
        out_shape=jax.ShapeDtypeStruct((MOD_ROWS, n), F32),
        compiler_params=pltpu.CompilerParams(dimension_semantics=("arbitrary",),
                                             vmem_limit_bytes=VMEM_LIMIT),
        name="mod",
    )(c_rows, w_mod, b_mod)


def _norm_modulate(x, g, sh, sc):
    ms = jnp.mean(x * x, axis=-1, keepdims=True)
    y = x * lax.rsqrt(ms + EPS) * g
    return y * (1.0 + sc) + sh


def _store_attention_kv(k, v, ks_ref, vt_ref):
    rows = k.shape[0]
    low = lax.broadcasted_iota(jnp.int32, k.shape, 1) < ATT_HD
    k_sw = pltpu.roll(k, ATT_HD, 1)
    ks_ref[0] = jnp.where(low, k, k_sw).astype(BF16)
    ks_ref[1] = jnp.where(low, k_sw, k).astype(BF16)
    v_t = v.T
    ones = jnp.ones((BF16_SUBLANES, rows), BF16)
    for j in range(ATT_KV_HEADS):
        vt_ref[j, :ATT_HD, :] = v_t[j * ATT_HD:(j + 1) * ATT_HD].astype(BF16)
        vt_ref[j, ATT_HD:, :] = ones


def _ctx_proj_kernel(x_ref, sh_ref, sc_ref, g_ref, wr_ref, wa_ref, gk_ref, ret_ref, ks_ref, vt_ref):
    h = _norm_modulate(x_ref[...], g_ref[...], sh_ref[...], sc_ref[...]).astype(BF16)
    pr = _dot(h, wr_ref[...])
    ret_ref[:, :RET_W] = (pr[:, :RET_W] * RET_SCALE).astype(BF16)
    ret_ref[:, RET_W:] = pr[:, RET_W:].astype(BF16)
    pa = _dot(h, wa_ref[...])
    ak = pa[:, :LANES]
    r = _head_rms_scale(ak, _head_mean_matrix(LANES))
    _store_attention_kv(ak * r * gk_ref[...], pa[:, LANES:], ks_ref, vt_ref)


def _ctx_proj_call(ctx_rows, ctx_row, mod3, norm1_g, w_rkv, w_akv, gk2):
    N, D = ctx_rows.shape
    tm = TM_CTX
    const = lambda i: (0, 0)
    return pl.pallas_call(
        _ctx_proj_kernel,
        grid=(N // tm,),
        in_specs=[
            pl.BlockSpec((tm, D), lambda i: (i, 0)),
            pl.BlockSpec((None, 1, D), lambda i: (ctx_row, 0, 0)),
            pl.BlockSpec((None, 1, D), lambda i: (ctx_row, 0, 1)),
            pl.BlockSpec((1, D), const),
            pl.BlockSpec(w_rkv.shape, const),
            pl.BlockSpec(w_akv.shape, const),
            pl.BlockSpec((1, LANES), const),
        ],
        out_specs=[
            pl.BlockSpec((tm, 2 * RET_W), lambda i: (i, 0)),
            pl.BlockSpec((ATT_KV_HEADS, tm, LANES), lambda i: (0, i, 0)),
            pl.BlockSpec((ATT_KV_HEADS, ATT_VT_ROWS, tm), lambda i: (0, 0, i)),
        ],
        out_shape=[
            jax.ShapeDtypeStruct((N, 2 * RET_W), BF16),
            jax.ShapeDtypeStruct((ATT_KV_HEADS, N, LANES), BF16),
            jax.ShapeDtypeStruct((ATT_KV_HEADS, ATT_VT_ROWS, N), BF16),
        ],
        compiler_params=pltpu.CompilerParams(dimension_semantics=("arbitrary",),
                                             vmem_limit_bytes=VMEM_LIMIT),
        name="ctx_proj",
    )(ctx_rows, mod3, mod3, norm1_g, w_rkv, w_akv, gk2)


def _in_proj_kernel(x_ref, sh_ref, sc_ref, g_ref, w_ref, cr_ref, sr_ref, ca_ref, sa_ref,
                    gq_ref, gk_ref, ret_ref, aq_ref, ks_ref, vt_ref):
    tm = x_ref.shape[0]
    h = _norm_modulate(x_ref[...], g_ref[...], sh_ref[...], sc_ref[...]).astype(BF16)
    cr, sr = cr_ref[...], sr_ref[...]
    ca, sa = ca_ref[...], sa_ref[...]
    lane = lax.broadcasted_iota(jnp.int32, (tm, LANES), 1)
    first_half = (lane & (ATT_HD // 2)) == 0

    bd = _head_mean_matrix(MXU_DIM)
    gq, gk = gq_ref[...], gk_ref[...]
    pa = _dot(h, w_ref[:, AQ_OFF:AK_OFF])
    for j in range(ATT_W // MXU_DIM):
        blk = pa[:, j * MXU_DIM:(j + 1) * MXU_DIM]
        r = _head_rms_scale(blk, bd) * (ATT_SCALE * LOG2E)
        for i in range(MXU_DIM // LANES):
            sl = slice(i * LANES, (i + 1) * LANES)
            o = _rope_att(blk[:, sl] * gq, ca, sa, first_half) * r[:, sl]
            aq_ref[j * MXU_DIM + i * LANES:j * MXU_DIM + (i + 1) * LANES, :] = o.T.astype(BF16)

    pkv = _dot(h, w_ref[:, AK_OFF:D_IN])
    ak = pkv[:, :LANES]
    rk = _head_rms_scale(ak, bd[:LANES, :LANES])
    _store_attention_kv(_rope_att(ak * gk, ca, sa, first_half) * rk, pkv[:, LANES:], ks_ref, vt_ref)

    pq = _dot(h, w_ref[:, RQ_OFF:RK_OFF])
    for hh in range(RET_HEADS):
        sl = slice(hh * RET_DK, (hh + 1) * RET_DK)
        ret_ref[:, RQ_OFF + hh * RET_DK:RQ_OFF + (hh + 1) * RET_DK] = _rope_ret(pq[:, sl], cr, sr).astype(BF16)
    pk = _dot(h, w_ref[:, RK_OFF:RV_OFF])
    for hh in range(RET_HEADS):
        sl = slice(hh * RET_DK, (hh + 1) * RET_DK)
        ret_ref[:, RK_OFF + hh * RET_DK:RK_OFF + (hh + 1) * RET_DK] = (
            _rope_ret(pk[:, sl], cr, sr) * RET_SCALE).astype(BF16)
    ret_ref[:, RV_OFF:AQ_OFF] = _dot(h, w_ref[:, RV_OFF:AQ_OFF]).astype(BF16)


def _in_proj_call(x, mod3, norm1_g, w_in, cr, sr, ca, sa, gq2, gk2):
    B, L, D = x.shape
    tm = TM_IN
    const = lambda b, i: (0, 0)
    tok = lambda b, i: (i, 0)
    return pl.pallas_call(
        _in_proj_kernel,
        grid=(B, L // tm),
        in_specs=[
            pl.BlockSpec((None, tm, D), lambda b, i: (b, i, 0)),
            pl.BlockSpec((None, 1, D), lambda b, i: (b, 0, 0)),
            pl.BlockSpec((None, 1, D), lambda b, i: (b, 0, 1)),
            pl.BlockSpec((1, D), const),
            pl.BlockSpec(w_in.shape, const),
            pl.BlockSpec((tm, LANES), tok),
            pl.BlockSpec((tm, LANES), tok),
            pl.BlockSpec((tm, LANES), tok),
            pl.BlockSpec((tm, LANES), tok),
            pl.BlockSpec((1, LANES), const),
            pl.BlockSpec((1, LANES), const),
        ],
        out_specs=[
            pl.BlockSpec((None, tm, AQ_OFF), lambda b, i: (b, i, 0)),
            pl.BlockSpec((None, ATT_W, tm), lambda b, i: (b, 0, i)),
            pl.BlockSpec((None, ATT_KV_HEADS, tm, LANES), lambda b, i: (b, 0, i, 0)),
            pl.BlockSpec((None, ATT_KV_HEADS, ATT_VT_ROWS, tm), lambda b, i: (b, 0, 0, i)),
        ],
        out_shape=[
            jax.ShapeDtypeStruct((B, L, AQ_OFF), BF16),
            jax.ShapeDtypeStruct((B, ATT_W, L), BF16),
            jax.ShapeDtypeStruct((B, ATT_KV_HEADS, L, LANES), BF16),
            jax.ShapeDtypeStruct((B, ATT_KV_HEADS, ATT_VT_ROWS, L), BF16),
        ],
        compiler_params=pltpu.CompilerParams(dimension_semantics=("arbitrary", "arbitrary"),
                                             vmem_limit_bytes=VMEM_LIMIT),
        name="in_proj",
    )(x, mod3, mod3, norm1_g, w_in, cr, sr, ca, sa, gq2, gk2)


def _retention_kernel(q_ref, k_ref, v_ref, g_ref, kc_ref, vc_ref, rate_ref, gn_ref, o_ref,
                      u_ref, s_ref):
    L = q_ref.shape[0]
    Lc = kc_ref.shape[0]
    C = RET_CHUNK
    T = L // C

    lg = jnp.log1p(-jnp.exp(rate_ref[...]))
    lgf, lgb = lg[0], lg[1]
    lgf1, lgb1 = lgf[:, :RET_DK], lgb[:, :RET_DK]

    ri = lax.broadcasted_iota(jnp.int32, (C, C), 0)
    ci = lax.broadcasted_iota(jnp.int32, (C, C), 1)
    rel = (ri - ci).astype(F32)
    decay = (jnp.where(rel >= 0, jnp.exp(lgf * jnp.maximum(rel, 0.0)), 0.0)
             + jnp.where(rel <= 0, jnp.exp(lgb * jnp.maximum(-rel, 0.0)), 0.0))

    pos = lax.broadcasted_iota(jnp.int32, (C, RET_DK), 0).astype(F32)
    q_dec_f = jnp.exp(lgf1 * (pos + 1.0))
    q_dec_b = jnp.exp(lgb1 * (C - pos))
    k_dec_f = jnp.exp(lgf1 * (C - 1.0 - pos))
    k_dec_b = jnp.exp(lgb1 * pos)
    c_dec_f = jnp.exp(lgf1 * C)
    c_dec_b = jnp.exp(lgb1 * C)

    cpos = lax.broadcasted_iota(jnp.int32, (Lc, RET_DK), 0).astype(F32)
    kc = kc_ref[...].astype(F32)
    vc = vc_ref[...]
    s_f0 = _dot_tn((kc * jnp.exp(lgf1 * (Lc - 1.0 - cpos))).astype(BF16), vc)
    s_b0 = _dot_tn((kc * jnp.exp(lgb1 * cpos)).astype(BF16), vc)

    def rows(t):
        return slice(t * C, (t + 1) * C)

    for t in range(T):
        kf = k_ref[rows(t), :].astype(F32)
        kd = jnp.concatenate([(kf * k_dec_f).astype(BF16), (kf * k_dec_b).astype(BF16)], axis=1)
        u_ref[t] = _dot_tn(kd, v_ref[rows(t), :])

    s_f, s_b = s_f0, s_b0
    for t in range(T):
        tb = T - 1 - t
        s_ref[t, :RET_DK, :] = s_f.astype(BF16)
        s_ref[tb, RET_DK:, :] = s_b.astype(BF16)
        s_f = s_f * c_dec_f + u_ref[t, :RET_DK, :]
        s_b = s_b * c_dec_b + u_ref[tb, RET_DK:, :]

    gn = gn_ref[...]

    def qk(t):
        return _dot_nt(q_ref[rows(t), :], k_ref[rows(t), :])

    sc_next = qk(0)
    for t in range(T):
        sc = sc_next
        if t + 1 < T:
            sc_next = qk(t + 1)
        qf = q_ref[rows(t), :].astype(F32)
        lhs = jnp.concatenate([(sc * decay).astype(BF16), (qf * q_dec_f).astype(BF16),
                               (qf * q_dec_b).astype(BF16)], axis=1)
        rhs = jnp.concatenate([v_ref[rows(t), :], s_ref[t]], axis=0)
        o = _dot(lhs, rhs)
        mu = jnp.mean(o, axis=-1, keepdims=True)
        d = o - mu
        var = jnp.mean(d * d, axis=-1, keepdims=True)
        on = d * lax.rsqrt(var + EPS) * gn
        o_ref[rows(t), :] = (on * _silu(g_ref[rows(t), :].astype(F32))).astype(o_ref.dtype)


def _retention_call(ret_in, ctx_ret, rate, gn_g):
    B, L, _ = ret_in.shape
    Lc = ctx_ret.shape[0] // B
    H = RET_HEADS
    T = L // RET_CHUNK

    def col(off):
        return lambda b, h: (b, 0, off + h)

    return pl.pallas_call(
        _retention_kernel,
        grid=(B, H),
        in_specs=[
            pl.BlockSpec((None, L, RET_DK), col(RQ_OFF // RET_DK)),
            pl.BlockSpec((None, L, RET_DK), col(RK_OFF // RET_DK)),
            pl.BlockSpec((None, L, RET_DV), col(RV_OFF // RET_DV)),
            pl.BlockSpec((None, L, RET_DV), col(RG_OFF // RET_DV)),
            pl.BlockSpec((Lc, RET_DK), lambda b, h: (b, h)),
            pl.BlockSpec((Lc, RET_DV), lambda b, h: (b, RET_HEADS + h)),
            pl.BlockSpec((2, None, 1, RET_CHUNK), lambda b, h: (0, h, 0, 0)),
            pl.BlockSpec((1, RET_DV), lambda b, h: (0, h)),
        ],
        out_specs=pl.BlockSpec((None, L, RET_DV), col(0)),
        out_shape=jax.ShapeDtypeStruct((B, L, RET_W), BF16),
        scratch_shapes=[pltpu.VMEM((T, 2 * RET_DK, RET_DV), F32),
                        pltpu.VMEM((T, 2 * RET_DK, RET_DV), BF16)],
        compiler_params=pltpu.CompilerParams(dimension_semantics=("arbitrary", "arbitrary"),
                                             vmem_limit_bytes=VMEM_LIMIT),
        name="retention",
    )(ret_in, ret_in, ret_in, ret_in, ctx_ret, ctx_ret, rate, gn_g)


def _attention_kernel(q_ref, k_ref, vt_ref, kc_ref, vtc_ref, o_ref):
    Lc = kc_ref.shape[0]
    L = k_ref.shape[0]
    tq = q_ref.shape[1]
    ctx_tiles = Lc // ATT_KEYS
    n_tiles = ctx_tiles + L // ATT_KEYS

    def key_tile(c):
        if c < ctx_tiles:
            return kc_ref[c * ATT_KEYS:(c + 1) * ATT_KEYS, :]
        return k_ref[(c - ctx_tiles) * ATT_KEYS:(c - ctx_tiles + 1) * ATT_KEYS, :]

    def value_tile(c):
        if c < ctx_tiles:
            return vtc_ref[:, c * ATT_KEYS:(c + 1) * ATT_KEYS]
        return vt_ref[:, (c - ctx_tiles) * ATT_KEYS:(c - ctx_tiles + 1) * ATT_KEYS]

    streams = [(h, g) for h in range(tq // MXU_DIM) for g in range(ATT_GROUP)]
    pad = jnp.zeros((LANES - ATT_HD, MXU_DIM), BF16)

    def q_weights(h, g):
        return jnp.concatenate(
            [q_ref[g * ATT_HD:(g + 1) * ATT_HD, h * MXU_DIM:(h + 1) * MXU_DIM], pad], axis=0)

    wq = {st: q_weights(*st) for st in streams}

    def scores(c, st):
        s = _dot(key_tile(c), wq[st])
        return s, jnp.max(s, axis=0, keepdims=True)

    m = {}
    acc = {}

    def fold(c, st, s_mt):
        s, mt = s_mt
        v_c = value_tile(c)
        if c == 0:
            m[st] = mt
            acc[st] = _dot(v_c, jnp.exp2(s - mt).astype(BF16))
        else:
            m_new = jnp.maximum(m[st], mt)
            alpha = jnp.exp2(m[st] - m_new)
            acc[st] = acc[st] * alpha + _dot(v_c, jnp.exp2(s - m_new).astype(BF16))
            m[st] = m_new

    units = [(c, st) for c in range(n_tiles) for st in streams]
    pending = []
    for i in range(len(units) + ATT_AHEAD):
        if i < len(units):
            pending.append(scores(*units[i]))
        if i >= ATT_AHEAD:
            fold(*units[i - ATT_AHEAD], pending.pop(0))
    for h, g in streams:
        a = acc[(h, g)]
        o_ref[g * ATT_HD:(g + 1) * ATT_HD, h * MXU_DIM:(h + 1) * MXU_DIM] = (
            a[:ATT_HD] / a[ATT_HD:ATT_HD + 1]).astype(o_ref.dtype)


def _attention_call(aq_t, ks, vt, ctx_ks, ctx_vt):
    B, _, L = aq_t.shape
    Lc = ctx_ks.shape[1] // B
    tq = TQ_ATT
    assert L % ATT_KEYS == 0 and Lc % ATT_KEYS == 0 and tq % MXU_DIM == 0
    per_head = lambda b, kv, i: (b, kv, 0, 0)
    return pl.pallas_call(
        _attention_kernel,
        grid=(B, ATT_KV_HEADS, L // tq),
        in_specs=[
            pl.BlockSpec((None, ATT_GROUP * ATT_HD, tq), lambda b, kv, i: (b, kv, i)),
            pl.BlockSpec((None, None, L, LANES), per_head),
            pl.BlockSpec((None, None, ATT_VT_ROWS, L), per_head),
            pl.BlockSpec((None, Lc, LANES), lambda b, kv, i: (kv, b, 0)),
            pl.BlockSpec((None, ATT_VT_ROWS, Lc), lambda b, kv, i: (kv, 0, b)),
        ],
        out_specs=pl.BlockSpec((None, ATT_GROUP * ATT_HD, tq), lambda b, kv, i: (b, kv, i)),
        out_shape=jax.ShapeDtypeStruct((B, ATT_W, L), BF16),
        compiler_params=pltpu.CompilerParams(
            dimension_semantics=("arbitrary", "arbitrary", "arbitrary"),
            vmem_limit_bytes=VMEM_LIMIT),
        name="attention",
    )(aq_t, ks, vt, ctx_ks, ctx_vt)


def _out_ffn_kernel(x_ref, ret_ref, att_ref, g1_ref, sh2_ref, sc2_ref, g2_ref, n2_ref, nf_ref,
                    wo_r_ref, wo_a_ref, w1_ref, w2_ref, o_ref):
    tm = x_ref.shape[0]
    subs = [slice(s * TM_SUB, (s + 1) * TM_SUB) for s in range(tm // TM_SUB)]
    x1, h2 = [], []
    for r in subs:
        mix = _dot(ret_ref[r, :], wo_r_ref[...]) + _dot_tn(att_ref[:, r], wo_a_ref[...])
        x1.append(x_ref[r, :] + g1_ref[...] * mix)
        h2.append(_norm_modulate(x1[-1], n2_ref[...], sh2_ref[...], sc2_ref[...]).astype(BF16))
    for s, r in enumerate(subs):
        ff = jnp.zeros((TM_SUB, D_MODEL), F32)
        for j in range(D_FF // FF_CHUNK):
            a = jnp.maximum(_dot(h2[s], w1_ref[:, j * FF_CHUNK:(j + 1) * FF_CHUNK]), 0.0)
            ff = ff + _dot((a * a).astype(BF16), w2_ref[j * FF_CHUNK:(j + 1) * FF_CHUNK, :])
        x2 = x1[s] + g2_ref[...] * ff
        ms = jnp.mean(x2 * x2, axis=-1, keepdims=True)
        o_ref[r, :] = x2 * lax.rsqrt(ms + EPS) * nf_ref[...]


def _out_ffn_call(x, ret, att, mod3, norm2_g, final_g, wo_r, wo_a, w1, w2):
    B, L, D = x.shape
    tm = TM_PROJ
    const = lambda b, i: (0, 0)
    once = pl.Buffered(1)

    def modrow(j):
        return pl.BlockSpec((None, 1, D), lambda b, i: (b, 0, j))

    return pl.pallas_call(
        _out_ffn_kernel,
        grid=(B, L // tm),
        in_specs=[
            pl.BlockSpec((None, tm, D), lambda b, i: (b, i, 0)),
            pl.BlockSpec((None, tm, RET_W), lambda b, i: (b, i, 0)),
            pl.BlockSpec((None, ATT_W, tm), lambda b, i: (b, 0, i)),
            modrow(2), modrow(3), modrow(4), modrow(5),
            pl.BlockSpec((1, D), const),
            pl.BlockSpec((1, D), const),
            pl.BlockSpec(wo_r.shape, const, pipeline_mode=once),
            pl.BlockSpec(wo_a.shape, const, pipeline_mode=once),
            pl.BlockSpec(w1.shape, const, pipeline_mode=once),
            pl.BlockSpec(w2.shape, const, pipeline_mode=once),
        ],
        out_specs=pl.BlockSpec((None, tm, D), lambda b, i: (b, i, 0)),
        out_shape=jax.ShapeDtypeStruct((B, L, D), F32),
        compiler_params=pltpu.CompilerParams(dimension_semantics=("arbitrary", "arbitrary"),
                                             vmem_limit_bytes=VMEM_LIMIT),
        name="out_ffn",
    )(x, ret, att, mod3, mod3, mod3, mod3, norm2_g, final_g, wo_r, wo_a, w1, w2)


def _freqs(n_pairs):
    return ROPE_BASE ** (-jnp.arange(n_pairs, dtype=F32) / n_pairs)


def _rope_tables(L):
    lane = jnp.arange(LANES)
    t = jnp.arange(L, dtype=F32)
    ang = t[:, None] * jnp.tile(_freqs(RET_DK // 2), 2)[None, :]
    cr = jnp.cos(ang)
    sr = jnp.where((lane < RET_DK // 2)[None, :], -jnp.sin(ang), jnp.sin(ang))
    pair = lane % (ATT_HD // 2)
    freq = jnp.tile(_freqs(ATT_HD // 4), LANES // (ATT_HD // 4))
    tok = jnp.arange(L)
    row = (tok // GRID_W).astype(F32)[:, None]
    col = (tok % GRID_W).astype(F32)[:, None]
    aang = jnp.where((pair >= ATT_HD // 4)[None, :], col, row) * freq[None, :]
    ca = jnp.cos(aang)
    sa = jnp.where((lane % ATT_HD < ATT_HD // 2)[None, :], -jnp.sin(aang), jnp.sin(aang))
    return cr, sr, ca, sa


def kernel(x, c, ctx, c_ctx, w_mod, b_mod, norm1_g, norm2_g, w_in, w_out, ret_log_rate, ret_gn_g,
           q_norm_g, k_norm_g, w_ff1, w_ff2, final_norm_g):
    B, L, D = x.shape
    assert w_mod.shape[0] == 1, "single-layer configuration"
    assert B + 1 <= MOD_ROWS and L % TM_PROJ == 0 and L % TM_IN == 0 and L % TQ_ATT == 0 and L % RET_CHUNK == 0

    c_rows = jnp.zeros((MOD_ROWS, D), F32).at[:B].set(c).at[B].set(c_ctx)
    mod = _mod_call(c_rows, w_mod[0], b_mod[0][None, :])
    mod3 = mod.reshape(MOD_ROWS, 1, 6 * D)

    w_in_b = w_in[0].astype(BF16)
    w_rkv = w_in_b[:, RK_OFF:RG_OFF]
    w_akv = w_in_b[:, AK_OFF:D_IN]
    wo = w_out[0].astype(BF16)
    w1 = w_ff1[0].astype(BF16)
    w2 = w_ff2[0].astype(BF16)
    n1 = norm1_g[0][None, :]
    n2 = norm2_g[0][None, :]
    nf = final_norm_g[None, :]
    gq2 = jnp.tile(q_norm_g[0], LANES // ATT_HD)[None, :]
    gk2 = jnp.tile(k_norm_g[0], LANES // ATT_HD)[None, :]
    gn = ret_gn_g[0][None, :]
    rate = jnp.broadcast_to(ret_log_rate[0].astype(F32)[:, :, None, None],
                            (2, RET_HEADS, 1, RET_CHUNK))
    cr, sr, ca, sa = _rope_tables(L)

    Lc = ctx.shape[1]
    assert (B * Lc) % TM_CTX == 0
    ctx_ret, ctx_ks, ctx_vt = _ctx_proj_call(ctx.reshape(B * Lc, D), B, mod3, n1, w_rkv, w_akv, gk2)
    ret_in, aq_t, ks, vt = _in_proj_call(x, mod3, n1, w_in_b, cr, sr, ca, sa, gq2, gk2)
    ret = _retention_call(ret_in, ctx_ret, rate, gn)
    att = _attention_call(aq_t, ks, vt, ctx_ks, ctx_vt)
    return _out_ffn_call(x, ret, att, mod3, n2, nf, wo[:RET_W], wo[RET_W:], w1, w2)
```

```python
import math

import jax
import jax.numpy as jnp
from jax import lax
from jax.experimental import pallas as pl
from jax.experimental.pallas import tpu as pltpu

D_MODEL = 1024
CTX_LEN = 256
GRID_W = 64

RET_HEADS = 4
RET_DK = 128
RET_DV = 128
RET_W = RET_HEADS * RET_DV
RET_SCALE = RET_DK ** -0.5

ATT_HEADS = 8
ATT_KV_HEADS = 2
ATT_GROUP = ATT_HEADS // ATT_KV_HEADS
ATT_HD = 64
ATT_W = ATT_HEADS * ATT_HD
ATT_SCALE = ATT_HD ** -0.5

MIX_W = RET_W + ATT_W
D_FF = 4 * D_MODEL
ROPE_BASE = 10000.0
EPS = 1e-6
LOG2E = math.log2(math.e)

RQ_OFF = 0
RK_OFF = RQ_OFF + RET_HEADS * RET_DK
RV_OFF = RK_OFF + RET_HEADS * RET_DK
RG_OFF = RV_OFF + RET_W
AQ_OFF = RG_OFF + RET_W
AK_OFF = AQ_OFF + ATT_W
AV_OFF = AK_OFF + ATT_KV_HEADS * ATT_HD
D_IN = AV_OFF + ATT_KV_HEADS * ATT_HD

LANES = 128
BF16_SUBLANES = 16
ATT_VT_ROWS = ATT_HD + BF16_SUBLANES
MXU_DIM = 256
VMEM_LIMIT = 56 * 1024 * 1024

MOD_ROWS = 16
TM_CTX = 512
TM_IN = 512
TM_PROJ = 1024
TM_SUB = 512
TQ_ATT = 1024
ATT_KEYS = 256
ATT_AHEAD = 6
RET_CHUNK = 256
FF_CHUNK = 1024

F32 = jnp.float32
BF16 = jnp.bfloat16


def _dot(a, b):
    return jnp.dot(a, b, preferred_element_type=F32)


def _dot_nt(a, b):
    return lax.dot_general(a, b, (((1,), (1,)), ((), ())), preferred_element_type=F32)


def _dot_tn(a, b):
    return lax.dot_general(a, b, (((0,), (0,)), ((), ())), preferred_element_type=F32)


def _silu(x):
    return x * jax.nn.sigmoid(x)


def _head_mean_matrix(n):
    r = lax.broadcasted_iota(jnp.int32, (n, n), 0) // ATT_HD
    c = lax.broadcasted_iota(jnp.int32, (n, n), 1) // ATT_HD
    return jnp.where(r == c, 1.0 / ATT_HD, 0.0).astype(BF16)


def _head_rms_scale(blk, bd):
    ms = _dot((blk * blk).astype(BF16), bd)
    return lax.rsqrt(ms + EPS)


def _rope_ret(blk, c2, s2):
    return blk * c2 + pltpu.roll(blk, RET_DK // 2, 1) * s2


def _rope_att(blk, c2, s2, first_half):
    up = pltpu.roll(blk, ATT_HD // 2, 1)
    dn = pltpu.roll(blk, LANES - ATT_HD // 2, 1)
    return blk * c2 + jnp.where(first_half, dn, up) * s2


def _split_bf16(x):
    hi = x.astype(BF16)
    return hi, (x - hi.astype(F32)).astype(BF16)


def _mod_kernel(c_ref, w_ref, b_ref, o_ref):
    a_hi, a_lo = _split_bf16(_silu(c_ref[...]))
    w_hi, w_lo = _split_bf16(w_ref[...])
    by_hi = _dot(jnp.concatenate([a_hi, a_lo], axis=0), w_hi)
    out = by_hi[:MOD_ROWS] + by_hi[MOD_ROWS:] + _dot(a_hi, w_lo) + b_ref[...]
    for r in range(MOD_ROWS):
        o_ref[r] = out[r:r + 1, :]


def _mod_call(c_rows, w_mod, b_mod):
    n = w_mod.shape[1]
    bn = 1024
    return pl.pallas_call(
        _mod_kernel,
        grid=(n // bn,),
        in_specs=[
            pl.BlockSpec((MOD_ROWS, D_MODEL), lambda j: (0, 0)),
            pl.BlockSpec((D_MODEL, bn), lambda j: (0, j)),
            pl.BlockSpec((1, bn), lambda j: (0, j)),
        ],
        out_specs=pl.BlockSpec((MOD_ROWS, 1, bn), lambda j: (0, 0, j)),
        out_shape=jax.ShapeDtypeStruct((MOD_ROWS, 1, n), F32),
        compiler_params=pltpu.CompilerParams(dimension_semantics=("arbitrary",),
                                             vmem_limit_bytes=VMEM_LIMIT),
        name="mod",
    )(c_rows, w_mod, b_mod)


def _norm_modulate(x, g, sh, sc):
    ms = jnp.mean(x * x, axis=-1, keepdims=True)
    y = x * lax.rsqrt(ms + EPS) * g
    return y * (1.0 + sc) + sh


def _store_attention_kv(k, v, ks_ref, vt_ref):
    rows = k.shape[0]
    low = lax.broadcasted_iota(jnp.int32, k.shape, 1) < ATT_HD
    k_sw = pltpu.roll(k, ATT_HD, 1)
    ks_ref[0] = jnp.where(low, k, k_sw).astype(BF16)
    ks_ref[1] = jnp.where(low, k_sw, k).astype(BF16)
    v_t = v.T
    ones = jnp.ones((BF16_SUBLANES, rows), BF16)
    for j in range(ATT_KV_HEADS):
        vt_ref[j, :ATT_HD, :] = v_t[j * ATT_HD:(j + 1) * ATT_HD].astype(BF16)
        vt_ref[j, ATT_HD:, :] = ones


def _ctx_proj_kernel(x_ref, sh_ref, sc_ref, g_ref, wrk_ref, wrv_ref, wa_ref, gk_ref,
                     ret_ref, ks_ref, vt_ref):
    h = _norm_modulate(x_ref[...], g_ref[...], sh_ref[...], sc_ref[...]).astype(BF16)
    ret_ref[:, :RET_W] = (_dot(h, wrk_ref[...]) * RET_SCALE).astype(BF16)
    ret_ref[:, RET_W:] = _dot(h, wrv_ref[...]).astype(BF16)
    pa = _dot(h, wa_ref[...])
    ak = pa[:, :LANES]
    r = _head_rms_scale(ak, _head_mean_matrix(LANES))
    _store_attention_kv(ak * r * gk_ref[...], pa[:, LANES:], ks_ref, vt_ref)


def _ctx_proj_call(ctx_rows, ctx_row, mod3, norm1_g, w_in, gk2):
    N, D = ctx_rows.shape
    tm = TM_CTX
    const = lambda i: (0, 0)
    kv_w = D_IN - AK_OFF
    assert RK_OFF % RET_W == 0 and RV_OFF % RET_W == 0 and AK_OFF % kv_w == 0
    return pl.pallas_call(
        _ctx_proj_kernel,
        grid=(N // tm,),
        in_specs=[
            pl.BlockSpec((tm, D), lambda i: (i, 0)),
            pl.BlockSpec((None, 1, D), lambda i: (ctx_row, 0, 0)),
            pl.BlockSpec((None, 1, D), lambda i: (ctx_row, 0, 1)),
            pl.BlockSpec((1, D), const),
            pl.BlockSpec((D, RET_W), lambda i: (0, RK_OFF // RET_W)),
            pl.BlockSpec((D, RET_W), lambda i: (0, RV_OFF // RET_W)),
            pl.BlockSpec((D, kv_w), lambda i: (0, AK_OFF // kv_w)),
            pl.BlockSpec((1, LANES), const),
        ],
        out_specs=[
            pl.BlockSpec((tm, 2 * RET_W), lambda i: (i, 0)),
            pl.BlockSpec((ATT_KV_HEADS, tm, LANES), lambda i: (0, i, 0)),
            pl.BlockSpec((ATT_KV_HEADS, ATT_VT_ROWS, tm), lambda i: (0, 0, i)),
        ],
        out_shape=[
            jax.ShapeDtypeStruct((N, 2 * RET_W), BF16),
            jax.ShapeDtypeStruct((ATT_KV_HEADS, N, LANES), BF16),
            jax.ShapeDtypeStruct((ATT_KV_HEADS, ATT_VT_ROWS, N), BF16),
        ],
        compiler_params=pltpu.CompilerParams(dimension_semantics=("arbitrary",),
                                             vmem_limit_bytes=VMEM_LIMIT),
        name="ctx_proj",
    )(ctx_rows, mod3, mod3, norm1_g, w_in, w_in, w_in, gk2)


def _in_proj_kernel(x_ref, sh_ref, sc_ref, g_ref, w_ref, cr_ref, sr_ref, rc_ref, rs_ref, cc_ref, cs_ref,
                    gq_ref, gk_ref, ret_ref, aq_ref, ks_ref, vt_ref):
    tm = x_ref.shape[0]
    h = _norm_modulate(x_ref[...], g_ref[...], sh_ref[...], sc_ref[...]).astype(BF16)
    cr, sr = cr_ref[...], sr_ref[...]
    lane = lax.broadcasted_iota(jnp.int32, (tm, LANES), 1)
    first_half = (lane & (ATT_HD // 2)) == 0

    grid_rows = tm // GRID_W
    row0 = pl.program_id(1) * grid_rows
    col_lane = (lax.broadcasted_iota(jnp.int32, (GRID_W, LANES), 1) & (ATT_HD // 4)) != 0

    def axial(row_ref, col_ref):
        col_t = col_ref[...]
        return jnp.concatenate([jnp.where(col_lane, col_t, row_ref[pl.ds(row0 + r, 1), :])
                                for r in range(grid_rows)], axis=0)

    ca, sa = axial(rc_ref, cc_ref), axial(rs_ref, cs_ref)

    bd = _head_mean_matrix(MXU_DIM)
    gq, gk = gq_ref[...], gk_ref[...]
    pa = _dot(h, w_ref[:, AQ_OFF:AK_OFF])
    for j in range(ATT_W // MXU_DIM):
        blk = pa[:, j * MXU_DIM:(j + 1) * MXU_DIM]
        r = _head_rms_scale(blk, bd) * (ATT_SCALE * LOG2E)
        for i in range(MXU_DIM // LANES):
            sl = slice(i * LANES, (i + 1) * LANES)
            o = _rope_att(blk[:, sl] * gq, ca, sa, first_half) * r[:, sl]
            aq_ref[j * MXU_DIM + i * LANES:j * MXU_DIM + (i + 1) * LANES, :] = o.T.astype(BF16)

    pkv = _dot(h, w_ref[:, AK_OFF:D_IN])
    ak = pkv[:, :LANES]
    rk = _head_rms_scale(ak, bd[:LANES, :LANES])
    _store_attention_kv(_rope_att(ak * gk, ca, sa, first_half) * rk, pkv[:, LANES:], ks_ref, vt_ref)

    pq = _dot(h, w_ref[:, RQ_OFF:RK_OFF])
    for hh in range(RET_HEADS):
        sl = slice(hh * RET_DK, (hh + 1) * RET_DK)
        ret_ref[:, RQ_OFF + hh * RET_DK:RQ_OFF + (hh + 1) * RET_DK] = _rope_ret(pq[:, sl], cr, sr).astype(BF16)
    pk = _dot(h, w_ref[:, RK_OFF:RV_OFF])
    for hh in range(RET_HEADS):
        sl = slice(hh * RET_DK, (hh + 1) * RET_DK)
        ret_ref[:, RK_OFF + hh * RET_DK:RK_OFF + (hh + 1) * RET_DK] = (
            _rope_ret(pk[:, sl], cr, sr) * RET_SCALE).astype(BF16)
    ret_ref[:, RV_OFF:AQ_OFF] = _dot(h, w_ref[:, RV_OFF:AQ_OFF]).astype(BF16)


def _in_proj_call(x, mod3, norm1_g, w_in, cr, sr, row_c, row_s, col_c, col_s, gq2, gk2):
    B, L, D = x.shape
    tm = TM_IN
    assert tm % GRID_W == 0
    const = lambda b, i: (0, 0)
    tok = lambda b, i: (i, 0)
    return pl.pallas_call(
        _in_proj_kernel,
        grid=(B, L // tm),
        in_specs=[
            pl.BlockSpec((None, tm, D), lambda b, i: (b, i, 0)),
            pl.BlockSpec((None, 1, D), lambda b, i: (b, 0, 0)),
            pl.BlockSpec((None, 1, D), lambda b, i: (b, 0, 1)),
            pl.BlockSpec((1, D), const),
            pl.BlockSpec(w_in.shape, const),
            pl.BlockSpec((tm, LANES), tok),
            pl.BlockSpec((tm, LANES), tok),
            pl.BlockSpec(row_c.shape, const),
            pl.BlockSpec(row_s.shape, const),
            pl.BlockSpec(col_c.shape, const),
            pl.BlockSpec(col_s.shape, const),
            pl.BlockSpec((1, LANES), const),
            pl.BlockSpec((1, LANES), const),
        ],
        out_specs=[
            pl.BlockSpec((None, tm, AQ_OFF), lambda b, i: (b, i, 0)),
            pl.BlockSpec((None, ATT_W, tm), lambda b, i: (b, 0, i)),
            pl.BlockSpec((None, ATT_KV_HEADS, tm, LANES), lambda b, i: (b, 0, i, 0)),
            pl.BlockSpec((None, ATT_KV_HEADS, ATT_VT_ROWS, tm), lambda b, i: (b, 0, 0, i)),
        ],
        out_shape=[
            jax.ShapeDtypeStruct((B, L, AQ_OFF), BF16),
            jax.ShapeDtypeStruct((B, ATT_W, L), BF16),
            jax.ShapeDtypeStruct((B, ATT_KV_HEADS, L, LANES), BF16),
            jax.ShapeDtypeStruct((B, ATT_KV_HEADS, ATT_VT_ROWS, L), BF16),
        ],
        compiler_params=pltpu.CompilerParams(dimension_semantics=("arbitrary", "arbitrary"),
                                             vmem_limit_bytes=VMEM_LIMIT),
        name="in_proj",
    )(x, mod3, mod3, norm1_g, w_in, cr, sr, row_c, row_s, col_c, col_s, gq2, gk2)


def _retention_kernel(q_ref, k_ref, v_ref, g_ref, kc_ref, vc_ref, rate_ref, gn_ref, o_ref,
                      u_ref, s_ref):
    L = q_ref.shape[0]
    Lc = kc_ref.shape[0]
    C = RET_CHUNK
    T = L // C

    lg = jnp.log1p(-jnp.exp(rate_ref[...]))
    lgf, lgb = lg[0], lg[1]
    lgf1, lgb1 = lgf[:, :RET_DK], lgb[:, :RET_DK]

    ri = lax.broadcasted_iota(jnp.int32, (C, C), 0)
    ci = lax.broadcasted_iota(jnp.int32, (C, C), 1)
    rel = (ri - ci).astype(F32)
    decay = (jnp.where(rel >= 0, jnp.exp(lgf * jnp.maximum(rel, 0.0)), 0.0)
             + jnp.where(rel <= 0, jnp.exp(lgb * jnp.maximum(-rel, 0.0)), 0.0))

    pos = lax.broadcasted_iota(jnp.int32, (C, RET_DK), 0).astype(F32)
    q_dec_f = jnp.exp(lgf1 * (pos + 1.0))
    q_dec_b = jnp.exp(lgb1 * (C - pos))
    k_dec_f = jnp.exp(lgf1 * (C - 1.0 - pos))
    k_dec_b = jnp.exp(lgb1 * pos)
    c_dec_f = jnp.exp(lgf1 * C)
    c_dec_b = jnp.exp(lgb1 * C)

    cpos = lax.broadcasted_iota(jnp.int32, (Lc, RET_DK), 0).astype(F32)
    kc = kc_ref[...].astype(F32)
    vc = vc_ref[...]
    s_f0 = _dot_tn((kc * jnp.exp(lgf1 * (Lc - 1.0 - cpos))).astype(BF16), vc)
    s_b0 = _dot_tn((kc * jnp.exp(lgb1 * cpos)).astype(BF16), vc)

    def rows(t):
        return slice(t * C, (t + 1) * C)

    for t in range(T):
        kf = k_ref[rows(t), :].astype(F32)
        kd = jnp.concatenate([(kf * k_dec_f).astype(BF16), (kf * k_dec_b).astype(BF16)], axis=1)
        u_ref[t] = _dot_tn(kd, v_ref[rows(t), :])

    s_f, s_b = s_f0, s_b0
    for t in range(T):
        tb = T - 1 - t
        s_ref[t, :RET_DK, :] = s_f.astype(BF16)
        s_ref[tb, RET_DK:, :] = s_b.astype(BF16)
        s_f = s_f * c_dec_f + u_ref[t, :RET_DK, :]
        s_b = s_b * c_dec_b + u_ref[tb, RET_DK:, :]

    gn = gn_ref[...]

    def qk(t):
        return _dot_nt(q_ref[rows(t), :], k_ref[rows(t), :])

    sc_next = qk(0)
    for t in range(T):
        sc = sc_next
        if t + 1 < T:
            sc_next = qk(t + 1)
        qf = q_ref[rows(t), :].astype(F32)
        lhs = jnp.concatenate([(sc * decay).astype(BF16), (qf * q_dec_f).astype(BF16),
                               (qf * q_dec_b).astype(BF16)], axis=1)
        rhs = jnp.concatenate([v_ref[rows(t), :], s_ref[t]], axis=0)
        o = _dot(lhs, rhs)
        mu = jnp.mean(o, axis=-1, keepdims=True)
        d = o - mu
        var = jnp.mean(d * d, axis=-1, keepdims=True)
        on = d * lax.rsqrt(var + EPS) * gn
        o_ref[rows(t), :] = (on * _silu(g_ref[rows(t), :].astype(F32))).astype(o_ref.dtype)


def _retention_call(ret_in, ctx_ret, rate, gn_g):
    B, L, _ = ret_in.shape
    Lc = ctx_ret.shape[0] // B
    H = RET_HEADS
    T = L // RET_CHUNK

    def col(off):
        return lambda b, h: (b, 0, off + h)

    return pl.pallas_call(
        _retention_kernel,
        grid=(B, H),
        in_specs=[
            pl.BlockSpec((None, L, RET_DK), col(RQ_OFF // RET_DK)),
            pl.BlockSpec((None, L, RET_DK), col(RK_OFF // RET_DK)),
            pl.BlockSpec((None, L, RET_DV), col(RV_OFF // RET_DV)),
            pl.BlockSpec((None, L, RET_DV), col(RG_OFF // RET_DV)),
            pl.BlockSpec((Lc, RET_DK), lambda b, h: (b, h)),
            pl.BlockSpec((Lc, RET_DV), lambda b, h: (b, RET_HEADS + h)),
            pl.BlockSpec((2, None, 1, RET_CHUNK), lambda b, h: (0, h, 0, 0)),
            pl.BlockSpec((1, RET_DV), lambda b, h: (0, h)),
        ],
        out_specs=pl.BlockSpec((None, L, RET_DV), col(0)),
        out_shape=jax.ShapeDtypeStruct((B, L, RET_W), BF16),
        scratch_shapes=[pltpu.VMEM((T, 2 * RET_DK, RET_DV), F32),
                        pltpu.VMEM((T, 2 * RET_DK, RET_DV), BF16)],
        compiler_params=pltpu.CompilerParams(dimension_semantics=("arbitrary", "arbitrary"),
                                             vmem_limit_bytes=VMEM_LIMIT),
        name="retention",
    )(ret_in, ret_in, ret_in, ret_in, ctx_ret, ctx_ret, rate, gn_g)


def _attention_kernel(q_ref, k_ref, vt_ref, kc_ref, vtc_ref, o_ref):
    Lc = kc_ref.shape[0]
    L = k_ref.shape[0]
    tq = q_ref.shape[1]
    ctx_tiles = Lc // ATT_KEYS
    n_tiles = ctx_tiles + L // ATT_KEYS

    def key_tile(c):
        if c < ctx_tiles:
            return kc_ref[c * ATT_KEYS:(c + 1) * ATT_KEYS, :]
        return k_ref[(c - ctx_tiles) * ATT_KEYS:(c - ctx_tiles + 1) * ATT_KEYS, :]

    def value_tile(c):
        if c < ctx_tiles:
            return vtc_ref[:, c * ATT_KEYS:(c + 1) * ATT_KEYS]
        return vt_ref[:, (c - ctx_tiles) * ATT_KEYS:(c - ctx_tiles + 1) * ATT_KEYS]

    streams = [(h, g) for h in range(tq // MXU_DIM) for g in range(ATT_GROUP)]
    pad = jnp.zeros((LANES - ATT_HD, MXU_DIM), BF16)

    def q_weights(h, g):
        return jnp.concatenate(
            [q_ref[g * ATT_HD:(g + 1) * ATT_HD, h * MXU_DIM:(h + 1) * MXU_DIM], pad], axis=0)

    wq = {st: q_weights(*st) for st in streams}

    def scores(c, st):
        s = _dot(key_tile(c), wq[st])
        return s, jnp.max(s, axis=0, keepdims=True)

    m = {}
    acc = {}

    def fold(c, st, s_mt):
        s, mt = s_mt
        v_c = value_tile(c)
        if c == 0:
            m[st] = mt
            acc[st] = _dot(v_c, jnp.exp2(s - mt).astype(BF16))
        else:
            m_new = jnp.maximum(m[st], mt)
            alpha = jnp.exp2(m[st] - m_new)
            acc[st] = acc[st] * alpha + _dot(v_c, jnp.exp2(s - m_new).astype(BF16))
            m[st] = m_new

    units = [(c, st) for c in range(n_tiles) for st in streams]
    pending = []
    for i in range(len(units) + ATT_AHEAD):
        if i < len(units):
            pending.append(scores(*units[i]))
        if i >= ATT_AHEAD:
            fold(*units[i - ATT_AHEAD], pending.pop(0))
    for h, g in streams:
        a = acc[(h, g)]
        o_ref[g * ATT_HD:(g + 1) * ATT_HD, h * MXU_DIM:(h + 1) * MXU_DIM] = (
            a[:ATT_HD] / a[ATT_HD:ATT_HD + 1]).astype(o_ref.dtype)


def _attention_call(aq_t, ks, vt, ctx_ks, ctx_vt):
    B, _, L = aq_t.shape
    Lc = ctx_ks.shape[1] // B
    tq = TQ_ATT
    assert L % ATT_KEYS == 0 and Lc % ATT_KEYS == 0 and tq % MXU_DIM == 0
    per_head = lambda b, kv, i: (b, kv, 0, 0)
    return pl.pallas_call(
        _attention_kernel,
        grid=(B, ATT_KV_HEADS, L // tq),
        in_specs=[
            pl.BlockSpec((None, ATT_GROUP * ATT_HD, tq), lambda b, kv, i: (b, kv, i)),
            pl.BlockSpec((None, None, L, LANES), per_head),
            pl.BlockSpec((None, None, ATT_VT_ROWS, L), per_head),
            pl.BlockSpec((None, Lc, LANES), lambda b, kv, i: (kv, b, 0)),
            pl.BlockSpec((None, ATT_VT_ROWS, Lc), lambda b, kv, i: (kv, 0, b)),
        ],
        out_specs=pl.BlockSpec((None, ATT_GROUP * ATT_HD, tq), lambda b, kv, i: (b, kv, i)),
        out_shape=jax.ShapeDtypeStruct((B, ATT_W, L), BF16),
        compiler_params=pltpu.CompilerParams(
            dimension_semantics=("arbitrary", "arbitrary", "arbitrary"),
            vmem_limit_bytes=VMEM_LIMIT),
        name="attention",
    )(aq_t, ks, vt, ctx_ks, ctx_vt)


def _out_ffn_kernel(x_ref, ret_ref, att_ref, g1_ref, sh2_ref, sc2_ref, g2_ref, n2_ref, nf_ref,
                    wo_r_ref, wo_a_ref, w1_ref, w2_ref, o_ref):
    tm = x_ref.shape[0]
    subs = [slice(s * TM_SUB, (s + 1) * TM_SUB) for s in range(tm // TM_SUB)]
    x1, h2 = [], []
    for r in subs:
        mix = _dot(ret_ref[r, :], wo_r_ref[...]) + _dot_tn(att_ref[:, r], wo_a_ref[...])
        x1.append(x_ref[r, :] + g1_ref[...] * mix)
        h2.append(_norm_modulate(x1[-1], n2_ref[...], sh2_ref[...], sc2_ref[...]).astype(BF16))
    for s, r in enumerate(subs):
        ff = jnp.zeros((TM_SUB, D_MODEL), F32)
        for j in range(D_FF // FF_CHUNK):
            a = jnp.maximum(_dot(h2[s], w1_ref[:, j * FF_CHUNK:(j + 1) * FF_CHUNK]), 0.0)
            ff = ff + _dot((a * a).astype(BF16), w2_ref[j * FF_CHUNK:(j + 1) * FF_CHUNK, :])
        x2 = x1[s] + g2_ref[...] * ff
        ms = jnp.mean(x2 * x2, axis=-1, keepdims=True)
        o_ref[r, :] = x2 * lax.rsqrt(ms + EPS) * nf_ref[...]


def _out_ffn_call(x, ret, att, mod3, norm2_g, final_g, wo, w1, w2):
    B, L, D = x.shape
    assert RET_W == ATT_W
    tm = TM_PROJ
    const = lambda b, i: (0, 0)
    once = pl.Buffered(1)

    def modrow(j):
        return pl.BlockSpec((None, 1, D), lambda b, i: (b, 0, j))

    return pl.pallas_call(
        _out_ffn_kernel,
        grid=(B, L // tm),
        in_specs=[
            pl.BlockSpec((None, tm, D), lambda b, i: (b, i, 0)),
            pl.BlockSpec((None, tm, RET_W), lambda b, i: (b, i, 0)),
            pl.BlockSpec((None, ATT_W, tm), lambda b, i: (b, 0, i)),
            modrow(2), modrow(3), modrow(4), modrow(5),
            pl.BlockSpec((1, D), const),
            pl.BlockSpec((1, D), const),
            pl.BlockSpec((RET_W, D), lambda b, i: (0, 0), pipeline_mode=once),
            pl.BlockSpec((ATT_W, D), lambda b, i: (1, 0), pipeline_mode=once),
            pl.BlockSpec(w1.shape, const, pipeline_mode=once),
            pl.BlockSpec(w2.shape, const, pipeline_mode=once),
        ],
        out_specs=pl.BlockSpec((None, tm, D), lambda b, i: (b, i, 0)),
        out_shape=jax.ShapeDtypeStruct((B, L, D), F32),
        compiler_params=pltpu.CompilerParams(dimension_semantics=("arbitrary", "arbitrary"),
                                             vmem_limit_bytes=VMEM_LIMIT),
        name="out_ffn",
    )(x, ret, att, mod3, mod3, mod3, mod3, norm2_g, final_g, wo, wo, w1, w2)


def _freqs(n_pairs):
    return ROPE_BASE ** (-jnp.arange(n_pairs, dtype=F32) / n_pairs)


def _rope_tables(L):
    lane = jnp.arange(LANES)
    t = jnp.arange(L, dtype=F32)
    ang = t[:, None] * jnp.tile(_freqs(RET_DK // 2), 2)[None, :]
    cr = jnp.cos(ang)
    sr = jnp.where((lane < RET_DK // 2)[None, :], -jnp.sin(ang), jnp.sin(ang))
    freq = jnp.tile(_freqs(ATT_HD // 4), LANES // (ATT_HD // 4))[None, :]
    neg = (lane % ATT_HD < ATT_HD // 2)[None, :]

    def axis_tables(n):
        a = jnp.arange(n, dtype=F32)[:, None] * freq
        return jnp.cos(a), jnp.where(neg, -jnp.sin(a), jnp.sin(a))

    row_c, row_s = axis_tables(L // GRID_W)
    col_c, col_s = axis_tables(GRID_W)
    return cr, sr, row_c, row_s, col_c, col_s


def kernel(x, c, ctx, c_ctx, w_mod, b_mod, norm1_g, norm2_g, w_in, w_out, ret_log_rate, ret_gn_g,
           q_norm_g, k_norm_g, w_ff1, w_ff2, final_norm_g):
    B, L, D = x.shape
    assert w_mod.shape[0] == 1, "single-layer configuration"
    assert B + 1 <= MOD_ROWS and L % TM_PROJ == 0 and L % TM_IN == 0 and L % TQ_ATT == 0 and L % RET_CHUNK == 0

    c_rows = jnp.zeros((MOD_ROWS, D), F32).at[:B].set(c).at[B].set(c_ctx)
    mod3 = _mod_call(c_rows, w_mod[0], b_mod[0][None, :])

    w_in_b = w_in[0].astype(BF16)
    wo = w_out[0].astype(BF16)
    w1 = w_ff1[0].astype(BF16)
    w2 = w_ff2[0].astype(BF16)
    n1 = norm1_g[0][None, :]
    n2 = norm2_g[0][None, :]
    nf = final_norm_g[None, :]
    gq2 = jnp.tile(q_norm_g[0], LANES // ATT_HD)[None, :]
    gk2 = jnp.tile(k_norm_g[0], LANES // ATT_HD)[None, :]
    gn = ret_gn_g[0][None, :]
    rate = jnp.broadcast_to(ret_log_rate[0].astype(F32)[:, :, None, None],
                            (2, RET_HEADS, 1, RET_CHUNK))
    cr, sr, row_c, row_s, col_c, col_s = _rope_tables(L)

    Lc = ctx.shape[1]
    assert (B * Lc) % TM_CTX == 0
    ctx_ret, ctx_ks, ctx_vt = _ctx_proj_call(ctx.reshape(B * Lc, D), B, mod3, n1, w_in_b, gk2)
    ret_in, aq_t, ks, vt = _in_proj_call(x, mod3, n1, w_in_b, cr, sr, row_c, row_s, col_c, col_s, gq2, gk2)
    ret = _retention_call(ret_in, ctx_ret, rate, gn)
    att = _attention_call(aq_t, ks, vt, ctx_ks, ctx_vt)
    return _out_ffn_call(x, ret, att, mod3, n2, nf, wo, w1, w2)
```

```python
import math

import jax
import jax.numpy as jnp
from jax import lax
from jax.experimental import pallas as pl
from jax.experimental.pallas import tpu as pltpu

D_MODEL = 1024
CTX_LEN = 256
GRID_W = 64

RET_HEADS = 4
RET_DK = 128
RET_DV = 128
RET_W = RET_HEADS * RET_DV
RET_SCALE = RET_DK ** -0.5

ATT_HEADS = 8
ATT_KV_HEADS = 2
ATT_GROUP = ATT_HEADS // ATT_KV_HEADS
ATT_HD = 64
ATT_W = ATT_HEADS * ATT_HD
ATT_SCALE = ATT_HD ** -0.5

MIX_W = RET_W + ATT_W
D_FF = 4 * D_MODEL
ROPE_BASE = 10000.0
EPS = 1e-6
LOG2E = math.log2(math.e)

RQ_OFF = 0
RK_OFF = RQ_OFF + RET_HEADS * RET_DK
RV_OFF = RK_OFF + RET_HEADS * RET_DK
RG_OFF = RV_OFF + RET_W
AQ_OFF = RG_OFF + RET_W
AK_OFF = AQ_OFF + ATT_W
AV_OFF = AK_OFF + ATT_KV_HEADS * ATT_HD
D_IN = AV_OFF + ATT_KV_HEADS * ATT_HD

LANES = 128
BF16_SUBLANES = 16
ATT_VT_ROWS = ATT_HD + BF16_SUBLANES
MXU_DIM = 256
VMEM_LIMIT = 56 * 1024 * 1024

MOD_ROWS = 16
TM_CTX = 512
TM_IN = 512
TM_PROJ = 1024
TM_SUB = 512
TQ_ATT = 1024
ATT_KEYS = 256
ATT_AHEAD = 6
RET_CHUNK = 256
FF_CHUNK = 1024

F32 = jnp.float32
BF16 = jnp.bfloat16


def _dot(a, b):
    return jnp.dot(a, b, preferred_element_type=F32)


def _dot_nt(a, b):
    return lax.dot_general(a, b, (((1,), (1,)), ((), ())), preferred_element_type=F32)


def _dot_tn(a, b):
    return lax.dot_general(a, b, (((0,), (0,)), ((), ())), preferred_element_type=F32)


def _silu(x):
    return x * jax.nn.sigmoid(x)


def _head_mean_matrix(n):
    r = lax.broadcasted_iota(jnp.int32, (n, n), 0) // ATT_HD
    c = lax.broadcasted_iota(jnp.int32, (n, n), 1) // ATT_HD
    return jnp.where(r == c, 1.0 / ATT_HD, 0.0).astype(BF16)


def _head_rms_scale(blk, bd):
    ms = _dot((blk * blk).astype(BF16), bd)
    return lax.rsqrt(ms + EPS)


def _rope_ret(blk, c2, s2):
    return blk * c2 + pltpu.roll(blk, RET_DK // 2, 1) * s2


def _rope_att(blk, c2, s2, first_half):
    up = pltpu.roll(blk, ATT_HD // 2, 1)
    dn = pltpu.roll(blk, LANES - ATT_HD // 2, 1)
    return blk * c2 + jnp.where(first_half, dn, up) * s2


def _split_bf16(x):
    hi = x.astype(BF16)
    return hi, (x - hi.astype(F32)).astype(BF16)


def _mod_kernel(c_ref, w_ref, b_ref, o_ref):
    a_hi, a_lo = _split_bf16(_silu(c_ref[...]))
    w_hi, w_lo = _split_bf16(w_ref[...])
    by_hi = _dot(jnp.concatenate([a_hi, a_lo], axis=0), w_hi)
    out = by_hi[:MOD_ROWS] + by_hi[MOD_ROWS:] + _dot(a_hi, w_lo) + b_ref[...]
    for r in range(MOD_ROWS):
        o_ref[r] = out[r:r + 1, :]


def _mod_call(c_rows, w_mod, b_mod):
    n = w_mod.shape[1]
    bn = 1024
    return pl.pallas_call(
        _mod_kernel,
        grid=(n // bn,),
        in_specs=[
            pl.BlockSpec((MOD_ROWS, D_MODEL), lambda j: (0, 0)),
            pl.BlockSpec((D_MODEL, bn), lambda j: (0, j)),
            pl.BlockSpec((1, bn), lambda j: (0, j)),
        ],
        out_specs=pl.BlockSpec((MOD_ROWS, 1, bn), lambda j: (0, 0, j)),
        out_shape=jax.ShapeDtypeStruct((MOD_ROWS, 1, n), F32),
        compiler_params=pltpu.CompilerParams(dimension_semantics=("arbitrary",),
                                             vmem_limit_bytes=VMEM_LIMIT),
        name="mod",
    )(c_rows, w_mod, b_mod)


def _norm_modulate(x, g, sh, sc):
    ms = jnp.mean(x * x, axis=-1, keepdims=True)
    y = x * lax.rsqrt(ms + EPS) * g
    return y * (1.0 + sc) + sh


def _store_attention_kv(k, v, ks_ref, vt_ref):
    rows = k.shape[0]
    low = lax.broadcasted_iota(jnp.int32, k.shape, 1) < ATT_HD
    k_sw = pltpu.roll(k, ATT_HD, 1)
    ks_ref[0] = jnp.where(low, k, k_sw).astype(BF16)
    ks_ref[1] = jnp.where(low, k_sw, k).astype(BF16)
    v_t = v.T
    ones = jnp.ones((BF16_SUBLANES, rows), BF16)
    for j in range(ATT_KV_HEADS):
        vt_ref[j, :ATT_HD, :] = v_t[j * ATT_HD:(j + 1) * ATT_HD].astype(BF16)
        vt_ref[j, ATT_HD:, :] = ones


def _ctx_proj_kernel(x_ref, sh_ref, sc_ref, g_ref, wrk_ref, wrv_ref, wa_ref, gk_ref,
                     ret_ref, ks_ref, vt_ref):
    h = _norm_modulate(x_ref[...], g_ref[...], sh_ref[...], sc_ref[...]).astype(BF16)
    ret_ref[:, :RET_W] = (_dot(h, wrk_ref[...]) * RET_SCALE).astype(BF16)
    ret_ref[:, RET_W:] = _dot(h, wrv_ref[...]).astype(BF16)
    pa = _dot(h, wa_ref[...])
    ak = pa[:, :LANES]
    r = _head_rms_scale(ak, _head_mean_matrix(LANES))
    _store_attention_kv(ak * r * gk_ref[...], pa[:, LANES:], ks_ref, vt_ref)


def _ctx_proj_call(ctx_rows, ctx_row, mod3, norm1_g, w_in, gk2):
    N, D = ctx_rows.shape
    tm = TM_CTX
    const = lambda i: (0, 0)
    kv_w = D_IN - AK_OFF
    assert RK_OFF % RET_W == 0 and RV_OFF % RET_W == 0 and AK_OFF % kv_w == 0
    return pl.pallas_call(
        _ctx_proj_kernel,
        grid=(N // tm,),
        in_specs=[
            pl.BlockSpec((tm, D), lambda i: (i, 0)),
            pl.BlockSpec((None, 1, D), lambda i: (ctx_row, 0, 0)),
            pl.BlockSpec((None, 1, D), lambda i: (ctx_row, 0, 1)),
            pl.BlockSpec((1, D), const),
            pl.BlockSpec((D, RET_W), lambda i: (0, RK_OFF // RET_W)),
            pl.BlockSpec((D, RET_W), lambda i: (0, RV_OFF // RET_W)),
            pl.BlockSpec((D, kv_w), lambda i: (0, AK_OFF // kv_w)),
            pl.BlockSpec((1, LANES), const),
        ],
        out_specs=[
            pl.BlockSpec((tm, 2 * RET_W), lambda i: (i, 0)),
            pl.BlockSpec((ATT_KV_HEADS, tm, LANES), lambda i: (0, i, 0)),
            pl.BlockSpec((ATT_KV_HEADS, ATT_VT_ROWS, tm), lambda i: (0, 0, i)),
        ],
        out_shape=[
            jax.ShapeDtypeStruct((N, 2 * RET_W), BF16),
            jax.ShapeDtypeStruct((ATT_KV_HEADS, N, LANES), BF16),
            jax.ShapeDtypeStruct((ATT_KV_HEADS, ATT_VT_ROWS, N), BF16),
        ],
        compiler_params=pltpu.CompilerParams(dimension_semantics=("arbitrary",),
                                             vmem_limit_bytes=VMEM_LIMIT),
        name="ctx_proj",
    )(ctx_rows, mod3, mod3, norm1_g, w_in, w_in, w_in, gk2)


def _in_proj_kernel(x_ref, sh_ref, sc_ref, g_ref, w_ref, tc_ref, ts_ref, oc_ref, os_ref, ocs_ref, oss_ref,
                    rc_ref, rs_ref, cc_ref, cs_ref, gq_ref, gk_ref, ret_ref, aq_ref, ks_ref, vt_ref):
    tm = x_ref.shape[0]
    tile = pl.program_id(1)
    h = _norm_modulate(x_ref[...], g_ref[...], sh_ref[...], sc_ref[...]).astype(BF16)
    a_c, a_s = tc_ref[pl.ds(tile, 1), :], ts_ref[pl.ds(tile, 1), :]
    cr = a_c * oc_ref[...] - a_s * os_ref[...]
    sr = a_s * ocs_ref[...] + a_c * oss_ref[...]
    lane = lax.broadcasted_iota(jnp.int32, (tm, LANES), 1)
    first_half = (lane & (ATT_HD // 2)) == 0

    grid_rows = tm // GRID_W
    row0 = tile * grid_rows
    col_lane = (lax.broadcasted_iota(jnp.int32, (GRID_W, LANES), 1) & (ATT_HD // 4)) != 0

    def axial(row_ref, col_ref):
        col_t = col_ref[...]
        return jnp.concatenate([jnp.where(col_lane, col_t, row_ref[pl.ds(row0 + r, 1), :])
                                for r in range(grid_rows)], axis=0)

    ca, sa = axial(rc_ref, cc_ref), axial(rs_ref, cs_ref)

    bd = _head_mean_matrix(MXU_DIM)
    gq, gk = gq_ref[...], gk_ref[...]
    pa = _dot(h, w_ref[:, AQ_OFF:AK_OFF])
    for j in range(ATT_W // MXU_DIM):
        blk = pa[:, j * MXU_DIM:(j + 1) * MXU_DIM]
        r = _head_rms_scale(blk, bd) * (ATT_SCALE * LOG2E)
        for i in range(MXU_DIM // LANES):
            sl = slice(i * LANES, (i + 1) * LANES)
            o = _rope_att(blk[:, sl] * gq, ca, sa, first_half) * r[:, sl]
            aq_ref[j * MXU_DIM + i * LANES:j * MXU_DIM + (i + 1) * LANES, :] = o.T.astype(BF16)

    pkv = _dot(h, w_ref[:, AK_OFF:D_IN])
    ak = pkv[:, :LANES]
    rk = _head_rms_scale(ak, bd[:LANES, :LANES])
    _store_attention_kv(_rope_att(ak * gk, ca, sa, first_half) * rk, pkv[:, LANES:], ks_ref, vt_ref)

    pq = _dot(h, w_ref[:, RQ_OFF:RK_OFF])
    for hh in range(RET_HEADS):
        sl = slice(hh * RET_DK, (hh + 1) * RET_DK)
        ret_ref[:, RQ_OFF + hh * RET_DK:RQ_OFF + (hh + 1) * RET_DK] = _rope_ret(pq[:, sl], cr, sr).astype(BF16)
    pk = _dot(h, w_ref[:, RK_OFF:RV_OFF])
    for hh in range(RET_HEADS):
        sl = slice(hh * RET_DK, (hh + 1) * RET_DK)
        ret_ref[:, RK_OFF + hh * RET_DK:RK_OFF + (hh + 1) * RET_DK] = (
            _rope_ret(pk[:, sl], cr, sr) * RET_SCALE).astype(BF16)
    ret_ref[:, RV_OFF:AQ_OFF] = _dot(h, w_ref[:, RV_OFF:AQ_OFF]).astype(BF16)


def _in_proj_call(x, mod3, norm1_g, w_in, ret_tables, axial_tables, gq2, gk2):
    B, L, D = x.shape
    tm = TM_IN
    assert tm % GRID_W == 0
    const = lambda b, i: (0, 0)
    tables = list(ret_tables) + list(axial_tables)
    return pl.pallas_call(
        _in_proj_kernel,
        grid=(B, L // tm),
        in_specs=[
            pl.BlockSpec((None, tm, D), lambda b, i: (b, i, 0)),
            pl.BlockSpec((None, 1, D), lambda b, i: (b, 0, 0)),
            pl.BlockSpec((None, 1, D), lambda b, i: (b, 0, 1)),
            pl.BlockSpec((1, D), const),
            pl.BlockSpec(w_in.shape, const),
            *[pl.BlockSpec(t.shape, const) for t in tables],
            pl.BlockSpec((1, LANES), const),
            pl.BlockSpec((1, LANES), const),
        ],
        out_specs=[
            pl.BlockSpec((None, tm, AQ_OFF), lambda b, i: (b, i, 0)),
            pl.BlockSpec((None, ATT_W, tm), lambda b, i: (b, 0, i)),
            pl.BlockSpec((None, ATT_KV_HEADS, tm, LANES), lambda b, i: (b, 0, i, 0)),
            pl.BlockSpec((None, ATT_KV_HEADS, ATT_VT_ROWS, tm), lambda b, i: (b, 0, 0, i)),
        ],
        out_shape=[
            jax.ShapeDtypeStruct((B, L, AQ_OFF), BF16),
            jax.ShapeDtypeStruct((B, ATT_W, L), BF16),
            jax.ShapeDtypeStruct((B, ATT_KV_HEADS, L, LANES), BF16),
            jax.ShapeDtypeStruct((B, ATT_KV_HEADS, ATT_VT_ROWS, L), BF16),
        ],
        compiler_params=pltpu.CompilerParams(dimension_semantics=("arbitrary", "arbitrary"),
                                             vmem_limit_bytes=VMEM_LIMIT),
        name="in_proj",
    )(x, mod3, mod3, norm1_g, w_in, *tables, gq2, gk2)


def _retention_kernel(q_ref, k_ref, v_ref, g_ref, kc_ref, vc_ref, rate_ref, gn_ref, o_ref,
                      u_ref, s_ref):
    L = q_ref.shape[0]
    Lc = kc_ref.shape[0]
    C = RET_CHUNK
    T = L // C

    lg = jnp.log1p(-jnp.exp(rate_ref[...]))
    lgf, lgb = lg[0], lg[1]
    lgf1, lgb1 = lgf[:, :RET_DK], lgb[:, :RET_DK]

    ri = lax.broadcasted_iota(jnp.int32, (C, C), 0)
    ci = lax.broadcasted_iota(jnp.int32, (C, C), 1)
    rel = (ri - ci).astype(F32)
    decay = (jnp.where(rel >= 0, jnp.exp(lgf * jnp.maximum(rel, 0.0)), 0.0)
             + jnp.where(rel <= 0, jnp.exp(lgb * jnp.maximum(-rel, 0.0)), 0.0))

    pos = lax.broadcasted_iota(jnp.int32, (C, RET_DK), 0).astype(F32)
    q_dec_f = jnp.exp(lgf1 * (pos + 1.0))
    q_dec_b = jnp.exp(lgb1 * (C - pos))
    k_dec_f = jnp.exp(lgf1 * (C - 1.0 - pos))
    k_dec_b = jnp.exp(lgb1 * pos)
    c_dec_f = jnp.exp(lgf1 * C)
    c_dec_b = jnp.exp(lgb1 * C)

    cpos = lax.broadcasted_iota(jnp.int32, (Lc, RET_DK), 0).astype(F32)
    kc = kc_ref[...].astype(F32)
    vc = vc_ref[...]
    s_f0 = _dot_tn((kc * jnp.exp(lgf1 * (Lc - 1.0 - cpos))).astype(BF16), vc)
    s_b0 = _dot_tn((kc * jnp.exp(lgb1 * cpos)).astype(BF16), vc)

    def rows(t):
        return slice(t * C, (t + 1) * C)

    for t in range(T):
        kf = k_ref[rows(t), :].astype(F32)
        kd = jnp.concatenate([(kf * k_dec_f).astype(BF16), (kf * k_dec_b).astype(BF16)], axis=1)
        u_ref[t] = _dot_tn(kd, v_ref[rows(t), :])

    s_f, s_b = s_f0, s_b0
    for t in range(T):
        tb = T - 1 - t
        s_ref[t, :RET_DK, :] = s_f.astype(BF16)
        s_ref[tb, RET_DK:, :] = s_b.astype(BF16)
        s_f = s_f * c_dec_f + u_ref[t, :RET_DK, :]
        s_b = s_b * c_dec_b + u_ref[tb, RET_DK:, :]

    gn = gn_ref[...]

    def qk(t):
        return _dot_nt(q_ref[rows(t), :], k_ref[rows(t), :])

    sc_next = qk(0)
    for t in range(T):
        sc = sc_next
        if t + 1 < T:
            sc_next = qk(t + 1)
        qf = q_ref[rows(t), :].astype(F32)
        lhs = jnp.concatenate([(sc * decay).astype(BF16), (qf * q_dec_f).astype(BF16),
                               (qf * q_dec_b).astype(BF16)], axis=1)
        rhs = jnp.concatenate([v_ref[rows(t), :], s_ref[t]], axis=0)
        o = _dot(lhs, rhs)
        mu = jnp.mean(o, axis=-1, keepdims=True)
        d = o - mu
        var = jnp.mean(d * d, axis=-1, keepdims=True)
        on = d * lax.rsqrt(var + EPS) * gn
        o_ref[rows(t), :] = (on * _silu(g_ref[rows(t), :].astype(F32))).astype(o_ref.dtype)


def _retention_call(ret_in, ctx_ret, rate, gn_g):
    B, L, _ = ret_in.shape
    Lc = ctx_ret.shape[0] // B
    H = RET_HEADS
    T = L // RET_CHUNK

    def col(off):
        return lambda b, h: (b, 0, off + h)

    return pl.pallas_call(
        _retention_kernel,
        grid=(B, H),
        in_specs=[
            pl.BlockSpec((None, L, RET_DK), col(RQ_OFF // RET_DK)),
            pl.BlockSpec((None, L, RET_DK), col(RK_OFF // RET_DK)),
            pl.BlockSpec((None, L, RET_DV), col(RV_OFF // RET_DV)),
            pl.BlockSpec((None, L, RET_DV), col(RG_OFF // RET_DV)),
            pl.BlockSpec((Lc, RET_DK), lambda b, h: (b, h)),
            pl.BlockSpec((Lc, RET_DV), lambda b, h: (b, RET_HEADS + h)),
            pl.BlockSpec((2, None, 1, RET_CHUNK), lambda b, h: (0, h, 0, 0)),
            pl.BlockSpec((1, RET_DV), lambda b, h: (0, h)),
        ],
        out_specs=pl.BlockSpec((None, L, RET_DV), col(0)),
        out_shape=jax.ShapeDtypeStruct((B, L, RET_W), BF16),
        scratch_shapes=[pltpu.VMEM((T, 2 * RET_DK, RET_DV), F32),
                        pltpu.VMEM((T, 2 * RET_DK, RET_DV), BF16)],
        compiler_params=pltpu.CompilerParams(dimension_semantics=("arbitrary", "arbitrary"),
                                             vmem_limit_bytes=VMEM_LIMIT),
        name="retention",
    )(ret_in, ret_in, ret_in, ret_in, ctx_ret, ctx_ret, rate, gn_g)


def _attention_kernel(q_ref, k_ref, vt_ref, kc_ref, vtc_ref, o_ref):
    Lc = kc_ref.shape[0]
    L = k_ref.shape[0]
    tq = q_ref.shape[1]
    ctx_tiles = Lc // ATT_KEYS
    n_tiles = ctx_tiles + L // ATT_KEYS

    def key_tile(c):
        if c < ctx_tiles:
            return kc_ref[c * ATT_KEYS:(c + 1) * ATT_KEYS, :]
        return k_ref[(c - ctx_tiles) * ATT_KEYS:(c - ctx_tiles + 1) * ATT_KEYS, :]

    def value_tile(c):
        if c < ctx_tiles:
            return vtc_ref[:, c * ATT_KEYS:(c + 1) * ATT_KEYS]
        return vt_ref[:, (c - ctx_tiles) * ATT_KEYS:(c - ctx_tiles + 1) * ATT_KEYS]

    streams = [(h, g) for h in range(tq // MXU_DIM) for g in range(ATT_GROUP)]
    pad = jnp.zeros((LANES - ATT_HD, MXU_DIM), BF16)

    def q_weights(h, g):
        return jnp.concatenate(
            [q_ref[g * ATT_HD:(g + 1) * ATT_HD, h * MXU_DIM:(h + 1) * MXU_DIM], pad], axis=0)

    wq = {st: q_weights(*st) for st in streams}

    def scores(c, st):
        s = _dot(key_tile(c), wq[st])
        return s, jnp.max(s, axis=0, keepdims=True)

    m = {}
    acc = {}

    def fold(c, st, s_mt):
        s, mt = s_mt
        v_c = value_tile(c)
        if c == 0:
            m[st] = mt
            acc[st] = _dot(v_c, jnp.exp2(s - mt).astype(BF16))
        else:
            m_new = jnp.maximum(m[st], mt)
            alpha = jnp.exp2(m[st] - m_new)
            acc[st] = acc[st] * alpha + _dot(v_c, jnp.exp2(s - m_new).astype(BF16))
            m[st] = m_new

    units = [(c, st) for c in range(n_tiles) for st in streams]
    pending = []
    for i in range(len(units) + ATT_AHEAD):
        if i < len(units):
            pending.append(scores(*units[i]))
        if i >= ATT_AHEAD:
            fold(*units[i - ATT_AHEAD], pending.pop(0))
    for h, g in streams:
        a = acc[(h, g)]
        o_ref[g * ATT_HD:(g + 1) * ATT_HD, h * MXU_DIM:(h + 1) * MXU_DIM] = (
            a[:ATT_HD] / a[ATT_HD:ATT_HD + 1]).astype(o_ref.dtype)


def _attention_call(aq_t, ks, vt, ctx_ks, ctx_vt):
    B, _, L = aq_t.shape
    Lc = ctx_ks.shape[1] // B
    tq = TQ_ATT
    assert L % ATT_KEYS == 0 and Lc % ATT_KEYS == 0 and tq % MXU_DIM == 0
    per_head = lambda b, kv, i: (b, kv, 0, 0)
    return pl.pallas_call(
        _attention_kernel,
        grid=(B, ATT_KV_HEADS, L // tq),
        in_specs=[
            pl.BlockSpec((None, ATT_GROUP * ATT_HD, tq), lambda b, kv, i: (b, kv, i)),
            pl.BlockSpec((None, None, L, LANES), per_head),
            pl.BlockSpec((None, None, ATT_VT_ROWS, L), per_head),
            pl.BlockSpec((None, Lc, LANES), lambda b, kv, i: (kv, b, 0)),
            pl.BlockSpec((None, ATT_VT_ROWS, Lc), lambda b, kv, i: (kv, 0, b)),
        ],
        out_specs=pl.BlockSpec((None, ATT_GROUP * ATT_HD, tq), lambda b, kv, i: (b, kv, i)),
        out_shape=jax.ShapeDtypeStruct((B, ATT_W, L), BF16),
        compiler_params=pltpu.CompilerParams(
            dimension_semantics=("arbitrary", "arbitrary", "arbitrary"),
            vmem_limit_bytes=VMEM_LIMIT),
        name="attention",
    )(aq_t, ks, vt, ctx_ks, ctx_vt)


def _out_ffn_kernel(x_ref, ret_ref, att_ref, g1_ref, sh2_ref, sc2_ref, g2_ref, n2_ref, nf_ref,
                    wo_r_ref, wo_a_ref, w1_ref, w2_ref, o_ref):
    tm = x_ref.shape[0]
    subs = [slice(s * TM_SUB, (s + 1) * TM_SUB) for s in range(tm // TM_SUB)]
    x1, h2 = [], []
    for r in subs:
        mix = _dot(ret_ref[r, :], wo_r_ref[...]) + _dot_tn(att_ref[:, r], wo_a_ref[...])
        x1.append(x_ref[r, :] + g1_ref[...] * mix)
        h2.append(_norm_modulate(x1[-1], n2_ref[...], sh2_ref[...], sc2_ref[...]).astype(BF16))
    for s, r in enumerate(subs):
        ff = jnp.zeros((TM_SUB, D_MODEL), F32)
        for j in range(D_FF // FF_CHUNK):
            a = jnp.maximum(_dot(h2[s], w1_ref[:, j * FF_CHUNK:(j + 1) * FF_CHUNK]), 0.0)
            ff = ff + _dot((a * a).astype(BF16), w2_ref[j * FF_CHUNK:(j + 1) * FF_CHUNK, :])
        x2 = x1[s] + g2_ref[...] * ff
        ms = jnp.mean(x2 * x2, axis=-1, keepdims=True)
        o_ref[r, :] = x2 * lax.rsqrt(ms + EPS) * nf_ref[...]


def _out_ffn_call(x, ret, att, mod3, norm2_g, final_g, wo, w1, w2):
    B, L, D = x.shape
    assert RET_W == ATT_W
    tm = TM_PROJ
    const = lambda b, i: (0, 0)
    once = pl.Buffered(1)

    def modrow(j):
        return pl.BlockSpec((None, 1, D), lambda b, i: (b, 0, j))

    return pl.pallas_call(
        _out_ffn_kernel,
        grid=(B, L // tm),
        in_specs=[
            pl.BlockSpec((None, tm, D), lambda b, i: (b, i, 0)),
            pl.BlockSpec((None, tm, RET_W), lambda b, i: (b, i, 0)),
            pl.BlockSpec((None, ATT_W, tm), lambda b, i: (b, 0, i)),
            modrow(2), modrow(3), modrow(4), modrow(5),
            pl.BlockSpec((1, D), const),
            pl.BlockSpec((1, D), const),
            pl.BlockSpec((RET_W, D), lambda b, i: (0, 0), pipeline_mode=once),
            pl.BlockSpec((ATT_W, D), lambda b, i: (1, 0), pipeline_mode=once),
            pl.BlockSpec(w1.shape, const, pipeline_mode=once),
            pl.BlockSpec(w2.shape, const, pipeline_mode=once),
        ],
        out_specs=pl.BlockSpec((None, tm, D), lambda b, i: (b, i, 0)),
        out_shape=jax.ShapeDtypeStruct((B, L, D), F32),
        compiler_params=pltpu.CompilerParams(dimension_semantics=("arbitrary", "arbitrary"),
                                             vmem_limit_bytes=VMEM_LIMIT),
        name="out_ffn",
    )(x, ret, att, mod3, mod3, mod3, mod3, norm2_g, final_g, wo, wo, w1, w2)


def _freqs(n_pairs):
    return ROPE_BASE ** (-jnp.arange(n_pairs, dtype=F32) / n_pairs)


def _rope_tables(L, tm):
    lane = jnp.arange(LANES)
    fr = jnp.tile(_freqs(RET_DK // 2), 2)[None, :]
    sign = jnp.where(lane < RET_DK // 2, -1.0, 1.0).astype(F32)[None, :]
    base = (jnp.arange(L // tm) * tm).astype(F32)[:, None] * fr
    off = jnp.arange(tm, dtype=F32)[:, None] * fr
    ret = (jnp.cos(base), jnp.sin(base), jnp.cos(off), jnp.sin(off),
           sign * jnp.cos(off), sign * jnp.sin(off))
    freq = jnp.tile(_freqs(ATT_HD // 4), LANES // (ATT_HD // 4))[None, :]
    neg = (lane % ATT_HD < ATT_HD // 2)[None, :]

    def axis_tables(n):
        a = jnp.arange(n, dtype=F32)[:, None] * freq
        return jnp.cos(a), jnp.where(neg, -jnp.sin(a), jnp.sin(a))

    row_c, row_s = axis_tables(L // GRID_W)
    col_c, col_s = axis_tables(GRID_W)
    return ret, (row_c, row_s, col_c, col_s)


def kernel(x, c, ctx, c_ctx, w_mod, b_mod, norm1_g, norm2_g, w_in, w_out, ret_log_rate, ret_gn_g,
           q_norm_g, k_norm_g, w_ff1, w_ff2, final_norm_g):
    B, L, D = x.shape
    assert w_mod.shape[0] == 1, "single-layer configuration"
    assert B + 1 <= MOD_ROWS and L % TM_PROJ == 0 and L % TM_IN == 0 and L % TQ_ATT == 0 and L % RET_CHUNK == 0

    c_rows = jnp.zeros((MOD_ROWS, D), F32).at[:B].set(c).at[B].set(c_ctx)
    mod3 = _mod_call(c_rows, w_mod[0], b_mod[0][None, :])

    w_in_b = w_in[0].astype(BF16)
    wo = w_out[0].astype(BF16)
    w1 = w_ff1[0].astype(BF16)
    w2 = w_ff2[0].astype(BF16)
    n1 = norm1_g[0][None, :]
    n2 = norm2_g[0][None, :]
    nf = final_norm_g[None, :]
    gq2 = jnp.tile(q_norm_g[0], LANES // ATT_HD)[None, :]
    gk2 = jnp.tile(k_norm_g[0], LANES // ATT_HD)[None, :]
    gn = ret_gn_g[0][None, :]
    rate = jnp.broadcast_to(ret_log_rate[0].astype(F32)[:, :, None, None],
                            (2, RET_HEADS, 1, RET_CHUNK))
    ret_tables, axial_tables = _rope_tables(L, TM_IN)

    Lc = ctx.shape[1]
    assert (B * Lc) % TM_CTX == 0
    ctx_ret, ctx_ks, ctx_vt = _ctx_proj_call(ctx.reshape(B * Lc, D), B, mod3, n1, w_in_b, gk2)
    ret_in, aq_t, ks, vt = _in_proj_call(x, mod3, n1, w_in_b, ret_tables, axial_tables, gq2, gk2)
    ret = _retention_call(ret_in, ctx_ret, rate, gn)
    att = _attention_call(aq_t, ks, vt, ctx_ks, ctx_vt)
    return _out_ffn_call(x, ret, att, mod3, n2, nf, wo, w1, w2)
```

```python
import math

import jax
import jax.numpy as jnp
from jax import lax
from jax.experimental import pallas as pl
from jax.experimental.pallas import tpu as pltpu

D_MODEL = 1024
GRID_W = 64

RET_HEADS = 4
RET_DK = 128
RET_DV = 128
RET_W = RET_HEADS * RET_DV
RET_SCALE = RET_DK ** -0.5

ATT_HEADS = 8
ATT_KV_HEADS = 2
ATT_GROUP = ATT_HEADS // ATT_KV_HEADS
ATT_HD = 64
ATT_W = ATT_HEADS * ATT_HD
ATT_SCALE = ATT_HD ** -0.5

MIX_W = RET_W + ATT_W
D_FF = 4 * D_MODEL
ROPE_BASE = 10000.0
EPS = 1e-6
LOG2E = math.log2(math.e)

RQ_OFF = 0
RK_OFF = RQ_OFF + RET_HEADS * RET_DK
RV_OFF = RK_OFF + RET_HEADS * RET_DK
RG_OFF = RV_OFF + RET_W
AQ_OFF = RG_OFF + RET_W
AK_OFF = AQ_OFF + ATT_W
AV_OFF = AK_OFF + ATT_KV_HEADS * ATT_HD
D_IN = AV_OFF + ATT_KV_HEADS * ATT_HD

LANES = 128
BF16_SUBLANES = 16
ATT_VT_ROWS = ATT_HD + BF16_SUBLANES
MXU_DIM = 256
V7X_VMEM_BYTES = 64 * 1024 * 1024
VMEM_LIMIT = V7X_VMEM_BYTES // 2
VMEM_LIMIT_OUT_FFN = V7X_VMEM_BYTES * 7 // 8

MOD_ROWS = 16
MOD_BN = 1024
TM_CTX = 512
TM_IN = 512
TM_PROJ = 1024
TM_SUB = 512
TQ_ATT = 1024
ATT_KEYS = 256
ATT_AHEAD = 6
RET_CHUNK = 256
FF_CHUNK = 1024

F32 = jnp.float32
BF16 = jnp.bfloat16


def _dot(a, b):
    return jnp.dot(a, b, preferred_element_type=F32)


def _dot_nt(a, b):
    return lax.dot_general(a, b, (((1,), (1,)), ((), ())), preferred_element_type=F32)


def _dot_tn(a, b):
    return lax.dot_general(a, b, (((0,), (0,)), ((), ())), preferred_element_type=F32)


def _silu(x):
    return x * jax.nn.sigmoid(x)


def _head_mean_matrix(n):
    r = lax.broadcasted_iota(jnp.int32, (n, n), 0) // ATT_HD
    c = lax.broadcasted_iota(jnp.int32, (n, n), 1) // ATT_HD
    return jnp.where(r == c, 1.0 / ATT_HD, 0.0).astype(BF16)


def _head_rms_scale(blk, bd):
    ms = _dot((blk * blk).astype(BF16), bd)
    return lax.rsqrt(ms + EPS)


def _rope_ret(blk, c2, s2):
    return blk * c2 + pltpu.roll(blk, RET_DK // 2, 1) * s2


def _rope_att(blk, c2, s2, first_half):
    up = pltpu.roll(blk, ATT_HD // 2, 1)
    dn = pltpu.roll(blk, LANES - ATT_HD // 2, 1)
    return blk * c2 + jnp.where(first_half, dn, up) * s2


def _split_bf16(x):
    hi = x.astype(BF16)
    return hi, (x - hi.astype(F32)).astype(BF16)


def _mod_kernel(c_ref, w_ref, b_ref, o_ref):
    a_hi, a_lo = _split_bf16(_silu(c_ref[...]))
    w_hi, w_lo = _split_bf16(w_ref[...])
    by_hi = _dot(jnp.concatenate([a_hi, a_lo], axis=0), w_hi)
    out = by_hi[:MOD_ROWS] + by_hi[MOD_ROWS:] + _dot(a_hi, w_lo) + b_ref[...]
    for r in range(MOD_ROWS):
        o_ref[r] = out[r:r + 1, :]


def _mod_call(c_rows, w_mod, b_mod):
    n = w_mod.shape[1]
    bn = MOD_BN
    return pl.pallas_call(
        _mod_kernel,
        grid=(n // bn,),
        in_specs=[
            pl.BlockSpec((MOD_ROWS, D_MODEL), lambda j: (0, 0)),
            pl.BlockSpec((D_MODEL, bn), lambda j: (0, j)),
            pl.BlockSpec((1, bn), lambda j: (0, j)),
        ],
        out_specs=pl.BlockSpec((MOD_ROWS, 1, bn), lambda j: (0, 0, j)),
        out_shape=jax.ShapeDtypeStruct((MOD_ROWS, 1, n), F32),
        compiler_params=pltpu.CompilerParams(dimension_semantics=("arbitrary",),
                                             vmem_limit_bytes=VMEM_LIMIT),
        name="mod",
    )(c_rows, w_mod, b_mod)


def _norm_modulate(x, g, sh, sc):
    ms = jnp.mean(x * x, axis=-1, keepdims=True)
    y = x * lax.rsqrt(ms + EPS) * g
    return y * (1.0 + sc) + sh


def _store_attention_kv(k, v, ks_ref, vt_ref):
    rows = k.shape[0]
    low = lax.broadcasted_iota(jnp.int32, k.shape, 1) < ATT_HD
    k_sw = pltpu.roll(k, ATT_HD, 1)
    ks_ref[0] = jnp.where(low, k, k_sw).astype(BF16)
    ks_ref[1] = jnp.where(low, k_sw, k).astype(BF16)
    v_t = v.T
    ones = jnp.ones((BF16_SUBLANES, rows), BF16)
    for j in range(ATT_KV_HEADS):
        vt_ref[j, :ATT_HD, :] = v_t[j * ATT_HD:(j + 1) * ATT_HD].astype(BF16)
        vt_ref[j, ATT_HD:, :] = ones


def _ctx_proj_kernel(x_ref, sh_ref, sc_ref, g_ref, wrk_ref, wrv_ref, wa_ref, gk_ref,
                     ret_ref, ks_ref, vt_ref):
    h = _norm_modulate(x_ref[...], g_ref[...], sh_ref[...], sc_ref[...]).astype(BF16)
    ret_ref[:, :RET_W] = (_dot(h, wrk_ref[...]) * RET_SCALE).astype(BF16)
    ret_ref[:, RET_W:] = _dot(h, wrv_ref[...]).astype(BF16)
    pa = _dot(h, wa_ref[...])
    ak = pa[:, :LANES]
    r = _head_rms_scale(ak, _head_mean_matrix(LANES))
    _store_attention_kv(ak * r * gk_ref[...], pa[:, LANES:], ks_ref, vt_ref)


def _ctx_proj_call(ctx_rows, ctx_row, mod3, norm1_g, w_in, gk2):
    N, D = ctx_rows.shape
    tm = TM_CTX
    const = lambda i: (0, 0)
    kv_w = D_IN - AK_OFF
    assert RK_OFF % RET_W == 0 and RV_OFF % RET_W == 0 and AK_OFF % kv_w == 0
    return pl.pallas_call(
        _ctx_proj_kernel,
        grid=(N // tm,),
        in_specs=[
            pl.BlockSpec((tm, D), lambda i: (i, 0)),
            pl.BlockSpec((None, 1, D), lambda i: (ctx_row, 0, 0)),
            pl.BlockSpec((None, 1, D), lambda i: (ctx_row, 0, 1)),
            pl.BlockSpec((1, D), const),
            pl.BlockSpec((D, RET_W), lambda i: (0, RK_OFF // RET_W)),
            pl.BlockSpec((D, RET_W), lambda i: (0, RV_OFF // RET_W)),
            pl.BlockSpec((D, kv_w), lambda i: (0, AK_OFF // kv_w)),
            pl.BlockSpec((1, LANES), const),
        ],
        out_specs=[
            pl.BlockSpec((tm, 2 * RET_W), lambda i: (i, 0)),
            pl.BlockSpec((ATT_KV_HEADS, tm, LANES), lambda i: (0, i, 0)),
            pl.BlockSpec((ATT_KV_HEADS, ATT_VT_ROWS, tm), lambda i: (0, 0, i)),
        ],
        out_shape=[
            jax.ShapeDtypeStruct((N, 2 * RET_W), BF16),
            jax.ShapeDtypeStruct((ATT_KV_HEADS, N, LANES), BF16),
            jax.ShapeDtypeStruct((ATT_KV_HEADS, ATT_VT_ROWS, N), BF16),
        ],
        compiler_params=pltpu.CompilerParams(dimension_semantics=("arbitrary",),
                                             vmem_limit_bytes=VMEM_LIMIT),
        name="ctx_proj",
    )(ctx_rows, mod3, mod3, norm1_g, w_in, w_in, w_in, gk2)


def _in_proj_kernel(x_ref, sh_ref, sc_ref, g_ref, w_ref, tc_ref, ts_ref, oc_ref, os_ref, ocs_ref, oss_ref,
                    rc_ref, rs_ref, cc_ref, cs_ref, gq_ref, gk_ref, ret_ref, aq_ref, ks_ref, vt_ref):
    tm = x_ref.shape[0]
    tile = pl.program_id(1)
    h = _norm_modulate(x_ref[...], g_ref[...], sh_ref[...], sc_ref[...]).astype(BF16)
    a_c, a_s = tc_ref[pl.ds(tile, 1), :], ts_ref[pl.ds(tile, 1), :]
    cr = a_c * oc_ref[...] - a_s * os_ref[...]
    sr = a_s * ocs_ref[...] + a_c * oss_ref[...]
    lane = lax.broadcasted_iota(jnp.int32, (tm, LANES), 1)
    first_half = (lane & (ATT_HD // 2)) == 0

    grid_rows = tm // GRID_W
    row0 = tile * grid_rows
    col_lane = (lax.broadcasted_iota(jnp.int32, (GRID_W, LANES), 1) & (ATT_HD // 4)) != 0

    def axial(row_ref, col_ref):
        col_t = col_ref[...]
        return jnp.concatenate([jnp.where(col_lane, col_t, row_ref[pl.ds(row0 + r, 1), :])
                                for r in range(grid_rows)], axis=0)

    ca, sa = axial(rc_ref, cc_ref), axial(rs_ref, cs_ref)

    bd = _head_mean_matrix(MXU_DIM)
    gq, gk = gq_ref[...], gk_ref[...]
    pa = _dot(h, w_ref[:, AQ_OFF:AK_OFF])
    for j in range(ATT_W // MXU_DIM):
        blk = pa[:, j * MXU_DIM:(j + 1) * MXU_DIM]
        r = _head_rms_scale(blk, bd) * (ATT_SCALE * LOG2E)
        for i in range(MXU_DIM // LANES):
            sl = slice(i * LANES, (i + 1) * LANES)
            o = _rope_att(blk[:, sl] * gq, ca, sa, first_half) * r[:, sl]
            aq_ref[j * MXU_DIM + i * LANES:j * MXU_DIM + (i + 1) * LANES, :] = o.T.astype(BF16)

    pkv = _dot(h, w_ref[:, AK_OFF:D_IN])
    ak = pkv[:, :LANES]
    rk = _head_rms_scale(ak, bd[:LANES, :LANES])
    _store_attention_kv(_rope_att(ak * gk, ca, sa, first_half) * rk, pkv[:, LANES:], ks_ref, vt_ref)

    pq = _dot(h, w_ref[:, RQ_OFF:RK_OFF])
    for hh in range(RET_HEADS):
        sl = slice(hh * RET_DK, (hh + 1) * RET_DK)
        ret_ref[:, RQ_OFF + hh * RET_DK:RQ_OFF + (hh + 1) * RET_DK] = _rope_ret(pq[:, sl], cr, sr).astype(BF16)
    pk = _dot(h, w_ref[:, RK_OFF:RV_OFF])
    for hh in range(RET_HEADS):
        sl = slice(hh * RET_DK, (hh + 1) * RET_DK)
        ret_ref[:, RK_OFF + hh * RET_DK:RK_OFF + (hh + 1) * RET_DK] = (
            _rope_ret(pk[:, sl], cr, sr) * RET_SCALE).astype(BF16)
    ret_ref[:, RV_OFF:AQ_OFF] = _dot(h, w_ref[:, RV_OFF:AQ_OFF]).astype(BF16)


def _in_proj_call(x, mod3, norm1_g, w_in, ret_tables, axial_tables, gq2, gk2):
    B, L, D = x.shape
    tm = TM_IN
    assert tm % GRID_W == 0
    const = lambda b, i: (0, 0)
    tables = list(ret_tables) + list(axial_tables)
    return pl.pallas_call(
        _in_proj_kernel,
        grid=(B, L // tm),
        in_specs=[
            pl.BlockSpec((None, tm, D), lambda b, i: (b, i, 0)),
            pl.BlockSpec((None, 1, D), lambda b, i: (b, 0, 0)),
            pl.BlockSpec((None, 1, D), lambda b, i: (b, 0, 1)),
            pl.BlockSpec((1, D), const),
            pl.BlockSpec(w_in.shape, const),
            *[pl.BlockSpec(t.shape, const) for t in tables],
            pl.BlockSpec((1, LANES), const),
            pl.BlockSpec((1, LANES), const),
        ],
        out_specs=[
            pl.BlockSpec((None, tm, AQ_OFF), lambda b, i: (b, i, 0)),
            pl.BlockSpec((None, ATT_W, tm), lambda b, i: (b, 0, i)),
            pl.BlockSpec((None, ATT_KV_HEADS, tm, LANES), lambda b, i: (b, 0, i, 0)),
            pl.BlockSpec((None, ATT_KV_HEADS, ATT_VT_ROWS, tm), lambda b, i: (b, 0, 0, i)),
        ],
        out_shape=[
            jax.ShapeDtypeStruct((B, L, AQ_OFF), BF16),
            jax.ShapeDtypeStruct((B, ATT_W, L), BF16),
            jax.ShapeDtypeStruct((B, ATT_KV_HEADS, L, LANES), BF16),
            jax.ShapeDtypeStruct((B, ATT_KV_HEADS, ATT_VT_ROWS, L), BF16),
        ],
        compiler_params=pltpu.CompilerParams(dimension_semantics=("arbitrary", "arbitrary"),
                                             vmem_limit_bytes=VMEM_LIMIT),
        name="in_proj",
    )(x, mod3, mod3, norm1_g, w_in, *tables, gq2, gk2)


def _retention_kernel(q_ref, k_ref, v_ref, g_ref, kc_ref, vc_ref, rate_ref, gn_ref, o_ref,
                      u_ref, s_ref):
    L = q_ref.shape[0]
    Lc = kc_ref.shape[0]
    C = RET_CHUNK
    T = L // C

    lg = jnp.log1p(-jnp.exp(rate_ref[...]))
    lgf, lgb = lg[0], lg[1]
    lgf1, lgb1 = lgf[:, :RET_DK], lgb[:, :RET_DK]

    ri = lax.broadcasted_iota(jnp.int32, (C, C), 0)
    ci = lax.broadcasted_iota(jnp.int32, (C, C), 1)
    rel = (ri - ci).astype(F32)
    decay = (jnp.where(rel >= 0, jnp.exp(lgf * jnp.maximum(rel, 0.0)), 0.0)
             + jnp.where(rel <= 0, jnp.exp(lgb * jnp.maximum(-rel, 0.0)), 0.0))

    pos = lax.broadcasted_iota(jnp.int32, (C, RET_DK), 0).astype(F32)
    q_dec_f = jnp.exp(lgf1 * (pos + 1.0))
    q_dec_b = jnp.exp(lgb1 * (C - pos))
    k_dec_f = jnp.exp(lgf1 * (C - 1.0 - pos))
    k_dec_b = jnp.exp(lgb1 * pos)
    c_dec_f = jnp.exp(lgf1 * C)
    c_dec_b = jnp.exp(lgb1 * C)

    cpos = lax.broadcasted_iota(jnp.int32, (Lc, RET_DK), 0).astype(F32)
    kc = kc_ref[...].astype(F32)
    vc = vc_ref[...]
    s_f0 = _dot_tn((kc * jnp.exp(lgf1 * (Lc - 1.0 - cpos))).astype(BF16), vc)
    s_b0 = _dot_tn((kc * jnp.exp(lgb1 * cpos)).astype(BF16), vc)

    def rows(t):
        return slice(t * C, (t + 1) * C)

    for t in range(T):
        kf = k_ref[rows(t), :].astype(F32)
        kd = jnp.concatenate([(kf * k_dec_f).astype(BF16), (kf * k_dec_b).astype(BF16)], axis=1)
        u_ref[t] = _dot_tn(kd, v_ref[rows(t), :])

    s_f, s_b = s_f0, s_b0
    for t in range(T):
        tb = T - 1 - t
        s_ref[t, :RET_DK, :] = s_f.astype(BF16)
        s_ref[tb, RET_DK:, :] = s_b.astype(BF16)
        s_f = s_f * c_dec_f + u_ref[t, :RET_DK, :]
        s_b = s_b * c_dec_b + u_ref[tb, RET_DK:, :]

    gn = gn_ref[...]

    def qk(t):
        return _dot_nt(q_ref[rows(t), :], k_ref[rows(t), :])

    sc_next = qk(0)
    for t in range(T):
        sc = sc_next
        if t + 1 < T:
            sc_next = qk(t + 1)
        qf = q_ref[rows(t), :].astype(F32)
        lhs = jnp.concatenate([(sc * decay).astype(BF16), (qf * q_dec_f).astype(BF16),
                               (qf * q_dec_b).astype(BF16)], axis=1)
        rhs = jnp.concatenate([v_ref[rows(t), :], s_ref[t]], axis=0)
        o = _dot(lhs, rhs)
        mu = jnp.mean(o, axis=-1, keepdims=True)
        d = o - mu
        var = jnp.mean(d * d, axis=-1, keepdims=True)
        on = d * lax.rsqrt(var + EPS) * gn
        o_ref[rows(t), :] = (on * _silu(g_ref[rows(t), :].astype(F32))).astype(o_ref.dtype)


def _retention_call(ret_in, ctx_ret, rate, gn_g):
    B, L, _ = ret_in.shape
    Lc = ctx_ret.shape[0] // B
    H = RET_HEADS
    T = L // RET_CHUNK

    def col(off):
        return lambda b, h: (b, 0, off + h)

    return pl.pallas_call(
        _retention_kernel,
        grid=(B, H),
        in_specs=[
            pl.BlockSpec((None, L, RET_DK), col(RQ_OFF // RET_DK)),
            pl.BlockSpec((None, L, RET_DK), col(RK_OFF // RET_DK)),
            pl.BlockSpec((None, L, RET_DV), col(RV_OFF // RET_DV)),
            pl.BlockSpec((None, L, RET_DV), col(RG_OFF // RET_DV)),
            pl.BlockSpec((Lc, RET_DK), lambda b, h: (b, h)),
            pl.BlockSpec((Lc, RET_DV), lambda b, h: (b, RET_HEADS + h)),
            pl.BlockSpec((2, None, 1, RET_CHUNK), lambda b, h: (0, h, 0, 0)),
            pl.BlockSpec((1, RET_DV), lambda b, h: (0, h)),
        ],
        out_specs=pl.BlockSpec((None, L, RET_DV), col(0)),
        out_shape=jax.ShapeDtypeStruct((B, L, RET_W), BF16),
        scratch_shapes=[pltpu.VMEM((T, 2 * RET_DK, RET_DV), F32),
                        pltpu.VMEM((T, 2 * RET_DK, RET_DV), BF16)],
        compiler_params=pltpu.CompilerParams(dimension_semantics=("arbitrary", "arbitrary"),
                                             vmem_limit_bytes=VMEM_LIMIT),
        name="retention",
    )(ret_in, ret_in, ret_in, ret_in, ctx_ret, ctx_ret, rate, gn_g)


def _attention_kernel(q_ref, k_ref, vt_ref, kc_ref, vtc_ref, o_ref):
    Lc = kc_ref.shape[0]
    L = k_ref.shape[0]
    tq = q_ref.shape[1]
    ctx_tiles = Lc // ATT_KEYS
    n_tiles = ctx_tiles + L // ATT_KEYS

    def key_tile(c):
        if c < ctx_tiles:
            return kc_ref[c * ATT_KEYS:(c + 1) * ATT_KEYS, :]
        return k_ref[(c - ctx_tiles) * ATT_KEYS:(c - ctx_tiles + 1) * ATT_KEYS, :]

    def value_tile(c):
        if c < ctx_tiles:
            return vtc_ref[:, c * ATT_KEYS:(c + 1) * ATT_KEYS]
        return vt_ref[:, (c - ctx_tiles) * ATT_KEYS:(c - ctx_tiles + 1) * ATT_KEYS]

    streams = [(h, g) for h in range(tq // MXU_DIM) for g in range(ATT_GROUP)]
    pad = jnp.zeros((LANES - ATT_HD, MXU_DIM), BF16)

    def q_weights(h, g):
        return jnp.concatenate(
            [q_ref[g * ATT_HD:(g + 1) * ATT_HD, h * MXU_DIM:(h + 1) * MXU_DIM], pad], axis=0)

    wq = {st: q_weights(*st) for st in streams}

    def scores(c, st):
        s = _dot(key_tile(c), wq[st])
        return s, jnp.max(s, axis=0, keepdims=True)

    m = {}
    acc = {}

    def fold(c, st, s_mt):
        s, mt = s_mt
        v_c = value_tile(c)
        if c == 0:
            m[st] = mt
            acc[st] = _dot(v_c, jnp.exp2(s - mt).astype(BF16))
        else:
            m_new = jnp.maximum(m[st], mt)
            alpha = jnp.exp2(m[st] - m_new)
            acc[st] = acc[st] * alpha + _dot(v_c, jnp.exp2(s - m_new).astype(BF16))
            m[st] = m_new

    units = [(c, st) for c in range(n_tiles) for st in streams]
    pending = []
    for i in range(len(units) + ATT_AHEAD):
        if i < len(units):
            pending.append(scores(*units[i]))
        if i >= ATT_AHEAD:
            fold(*units[i - ATT_AHEAD], pending.pop(0))
    for h, g in streams:
        a = acc[(h, g)]
        o_ref[g * ATT_HD:(g + 1) * ATT_HD, h * MXU_DIM:(h + 1) * MXU_DIM] = (
            a[:ATT_HD] / a[ATT_HD:ATT_HD + 1]).astype(o_ref.dtype)


def _attention_call(aq_t, ks, vt, ctx_ks, ctx_vt):
    B, _, L = aq_t.shape
    Lc = ctx_ks.shape[1] // B
    tq = TQ_ATT
    assert L % ATT_KEYS == 0 and Lc % ATT_KEYS == 0 and tq % MXU_DIM == 0
    per_head = lambda b, kv, i: (b, kv, 0, 0)
    return pl.pallas_call(
        _attention_kernel,
        grid=(B, ATT_KV_HEADS, L // tq),
        in_specs=[
            pl.BlockSpec((None, ATT_GROUP * ATT_HD, tq), lambda b, kv, i: (b, kv, i)),
            pl.BlockSpec((None, None, L, LANES), per_head),
            pl.BlockSpec((None, None, ATT_VT_ROWS, L), per_head),
            pl.BlockSpec((None, Lc, LANES), lambda b, kv, i: (kv, b, 0)),
            pl.BlockSpec((None, ATT_VT_ROWS, Lc), lambda b, kv, i: (kv, 0, b)),
        ],
        out_specs=pl.BlockSpec((None, ATT_GROUP * ATT_HD, tq), lambda b, kv, i: (b, kv, i)),
        out_shape=jax.ShapeDtypeStruct((B, ATT_W, L), BF16),
        compiler_params=pltpu.CompilerParams(
            dimension_semantics=("arbitrary", "arbitrary", "arbitrary"),
            vmem_limit_bytes=VMEM_LIMIT),
        name="attention",
    )(aq_t, ks, vt, ctx_ks, ctx_vt)


def _out_ffn_kernel(x_ref, ret_ref, att_ref, g1_ref, sh2_ref, sc2_ref, g2_ref, n2_ref, nf_ref,
                    wo_r_ref, wo_a_ref, w1_ref, w2_ref, o_ref):
    tm = x_ref.shape[0]
    subs = [slice(s * TM_SUB, (s + 1) * TM_SUB) for s in range(tm // TM_SUB)]
    x1, h2 = [], []
    for r in subs:
        mix = _dot(ret_ref[r, :], wo_r_ref[...]) + _dot_tn(att_ref[:, r], wo_a_ref[...])
        x1.append(x_ref[r, :] + g1_ref[...] * mix)
        h2.append(_norm_modulate(x1[-1], n2_ref[...], sh2_ref[...], sc2_ref[...]).astype(BF16))
    for s, r in enumerate(subs):
        ff = jnp.zeros((TM_SUB, D_MODEL), F32)
        for j in range(D_FF // FF_CHUNK):
            a = jnp.maximum(_dot(h2[s], w1_ref[:, j * FF_CHUNK:(j + 1) * FF_CHUNK]), 0.0)
            ff = ff + _dot((a * a).astype(BF16), w2_ref[j * FF_CHUNK:(j + 1) * FF_CHUNK, :])
        x2 = x1[s] + g2_ref[...] * ff
        ms = jnp.mean(x2 * x2, axis=-1, keepdims=True)
        o_ref[r, :] = x2 * lax.rsqrt(ms + EPS) * nf_ref[...]


def _out_ffn_call(x, ret, att, mod3, norm2_g, final_g, wo, w1, w2):
    B, L, D = x.shape
    assert RET_W == ATT_W
    tm = TM_PROJ
    const = lambda b, i: (0, 0)
    once = pl.Buffered(1)

    def modrow(j):
        return pl.BlockSpec((None, 1, D), lambda b, i: (b, 0, j))

    return pl.pallas_call(
        _out_ffn_kernel,
        grid=(B, L // tm),
        in_specs=[
            pl.BlockSpec((None, tm, D), lambda b, i: (b, i, 0)),
            pl.BlockSpec((None, tm, RET_W), lambda b, i: (b, i, 0)),
            pl.BlockSpec((None, ATT_W, tm), lambda b, i: (b, 0, i)),
            modrow(2), modrow(3), modrow(4), modrow(5),
            pl.BlockSpec((1, D), const),
            pl.BlockSpec((1, D), const),
            pl.BlockSpec((RET_W, D), lambda b, i: (0, 0), pipeline_mode=once),
            pl.BlockSpec((ATT_W, D), lambda b, i: (1, 0), pipeline_mode=once),
            pl.BlockSpec(w1.shape, const, pipeline_mode=once),
            pl.BlockSpec(w2.shape, const, pipeline_mode=once),
        ],
        out_specs=pl.BlockSpec((None, tm, D), lambda b, i: (b, i, 0)),
        out_shape=jax.ShapeDtypeStruct((B, L, D), F32),
        compiler_params=pltpu.CompilerParams(dimension_semantics=("arbitrary", "arbitrary"),
                                             vmem_limit_bytes=VMEM_LIMIT_OUT_FFN),
        name="out_ffn",
    )(x, ret, att, mod3, mod3, mod3, mod3, norm2_g, final_g, wo, wo, w1, w2)


def _freqs(n_pairs):
    return ROPE_BASE ** (-jnp.arange(n_pairs, dtype=F32) / n_pairs)


def _rope_tables(L, tm):
    lane = jnp.arange(LANES)
    fr = jnp.tile(_freqs(RET_DK // 2), 2)[None, :]
    sign = jnp.where(lane < RET_DK // 2, -1.0, 1.0).astype(F32)[None, :]
    base = (jnp.arange(L // tm) * tm).astype(F32)[:, None] * fr
    off = jnp.arange(tm, dtype=F32)[:, None] * fr
    ret = (jnp.cos(base), jnp.sin(base), jnp.cos(off), jnp.sin(off),
           sign * jnp.cos(off), sign * jnp.sin(off))
    freq = jnp.tile(_freqs(ATT_HD // 4), LANES // (ATT_HD // 4))[None, :]
    neg = (lane % ATT_HD < ATT_HD // 2)[None, :]

    def axis_tables(n):
        a = jnp.arange(n, dtype=F32)[:, None] * freq
        return jnp.cos(a), jnp.where(neg, -jnp.sin(a), jnp.sin(a))

    row_c, row_s = axis_tables(L // GRID_W)
    col_c, col_s = axis_tables(GRID_W)
    return ret, (row_c, row_s, col_c, col_s)


def kernel(x, c, ctx, c_ctx, w_mod, b_mod, norm1_g, norm2_g, w_in, w_out, ret_log_rate, ret_gn_g,
           q_norm_g, k_norm_g, w_ff1, w_ff2, final_norm_g):
    B, L, D = x.shape
    assert w_mod.shape[0] == 1, "single-layer configuration"
    assert B + 1 <= MOD_ROWS and L % TM_PROJ == 0 and L % TM_IN == 0 and L % TQ_ATT == 0 and L % RET_CHUNK == 0

    c_rows = jnp.zeros((MOD_ROWS, D), F32).at[:B].set(c).at[B].set(c_ctx)
    mod3 = _mod_call(c_rows, w_mod[0], b_mod[0][None, :])

    w_in_b = w_in[0].astype(BF16)
    wo = w_out[0].astype(BF16)
    w1 = w_ff1[0].astype(BF16)
    w2 = w_ff2[0].astype(BF16)
    n1 = norm1_g[0][None, :]
    n2 = norm2_g[0][None, :]
    nf = final_norm_g[None, :]
    gq2 = jnp.tile(q_norm_g[0], LANES // ATT_HD)[None, :]
    gk2 = jnp.tile(k_norm_g[0], LANES // ATT_HD)[None, :]
    gn = ret_gn_g[0][None, :]
    rate = jnp.broadcast_to(ret_log_rate[0].astype(F32)[:, :, None, None],
                            (2, RET_HEADS, 1, RET_CHUNK))
    ret_tables, axial_tables = _rope_tables(L, TM_IN)

    Lc = ctx.shape[1]
    assert (B * Lc) % TM_CTX == 0
    ctx_ret, ctx_ks, ctx_vt = _ctx_proj_call(ctx.reshape(B * Lc, D), B, mod3, n1, w_in_b, gk2)
    ret_in, aq_t, ks, vt = _in_proj_call(x, mod3, n1, w_in_b, ret_tables, axial_tables, gq2, gk2)
    ret = _retention_call(ret_in, ctx_ret, rate, gn)
    att = _attention_call(aq_t, ks, vt, ctx_ks, ctx_vt)
    return _out_ffn_call(x, ret, att, mod3, n2, nf, wo, w1, w2)
```

```python
import math

import jax
import jax.numpy as jnp
from jax import lax
from jax.experimental import pallas as pl
from jax.experimental.pallas import tpu as pltpu

D_MODEL = 1024
GRID_W = 64

RET_HEADS = 4
RET_DK = 128
RET_DV = 128
RET_W = RET_HEADS * RET_DV
RET_SCALE = RET_DK ** -0.5

ATT_HEADS = 8
ATT_KV_HEADS = 2
ATT_GROUP = ATT_HEADS // ATT_KV_HEADS
ATT_HD = 64
ATT_W = ATT_HEADS * ATT_HD
ATT_SCALE = ATT_HD ** -0.5

MIX_W = RET_W + ATT_W
D_FF = 4 * D_MODEL
ROPE_BASE = 10000.0
EPS = 1e-6
LOG2E = math.log2(math.e)

RQ_OFF = 0
RK_OFF = RQ_OFF + RET_HEADS * RET_DK
RV_OFF = RK_OFF + RET_HEADS * RET_DK
RG_OFF = RV_OFF + RET_W
AQ_OFF = RG_OFF + RET_W
AK_OFF = AQ_OFF + ATT_W
AV_OFF = AK_OFF + ATT_KV_HEADS * ATT_HD
D_IN = AV_OFF + ATT_KV_HEADS * ATT_HD

LANES = 128
BF16_SUBLANES = 16
ATT_VT_ROWS = ATT_HD + BF16_SUBLANES
MXU_DIM = 256
V7X_VMEM_BYTES = 64 * 1024 * 1024
VMEM_LIMIT = V7X_VMEM_BYTES * 7 // 8

MOD_ROWS = 16
MOD_BN = 1024
TM_CTX = 512
TM_IN = 512
TM_PROJ = 1024
TM_SUB = 512
TQ_ATT = 1024
ATT_KEYS = 256
ATT_AHEAD = 6
RET_CHUNK = 256
FF_CHUNK = 1024

F32 = jnp.float32
BF16 = jnp.bfloat16


def _dot(a, b):
    return jnp.dot(a, b, preferred_element_type=F32)


def _dot_nt(a, b):
    return lax.dot_general(a, b, (((1,), (1,)), ((), ())), preferred_element_type=F32)


def _dot_tn(a, b):
    return lax.dot_general(a, b, (((0,), (0,)), ((), ())), preferred_element_type=F32)


def _silu(x):
    return x * jax.nn.sigmoid(x)


def _head_mean_matrix(n):
    r = lax.broadcasted_iota(jnp.int32, (n, n), 0) // ATT_HD
    c = lax.broadcasted_iota(jnp.int32, (n, n), 1) // ATT_HD
    return jnp.where(r == c, 1.0 / ATT_HD, 0.0).astype(BF16)


def _head_rms_scale(blk, bd):
    ms = _dot((blk * blk).astype(BF16), bd)
    return lax.rsqrt(ms + EPS)


def _rope_ret(blk, c2, s2):
    return blk * c2 + pltpu.roll(blk, RET_DK // 2, 1) * s2


def _rope_att(blk, c2, s2, first_half):
    up = pltpu.roll(blk, ATT_HD // 2, 1)
    dn = pltpu.roll(blk, LANES - ATT_HD // 2, 1)
    return blk * c2 + jnp.where(first_half, dn, up) * s2


def _split_bf16(x):
    hi = x.astype(BF16)
    return hi, (x - hi.astype(F32)).astype(BF16)


def _mod_kernel(c_ref, w_ref, b_ref, o_ref):
    a_hi, a_lo = _split_bf16(_silu(c_ref[...]))
    w_hi, w_lo = _split_bf16(w_ref[...])
    by_hi = _dot(jnp.concatenate([a_hi, a_lo], axis=0), w_hi)
    out = by_hi[:MOD_ROWS] + by_hi[MOD_ROWS:] + _dot(a_hi, w_lo) + b_ref[...]
    for r in range(MOD_ROWS):
        o_ref[r] = out[r:r + 1, :]


def _mod_call(c_rows, w_mod, b_mod):
    n = w_mod.shape[1]
    bn = MOD_BN
    return pl.pallas_call(
        _mod_kernel,
        grid=(n // bn,),
        in_specs=[
            pl.BlockSpec((MOD_ROWS, D_MODEL), lambda j: (0, 0)),
            pl.BlockSpec((D_MODEL, bn), lambda j: (0, j)),
            pl.BlockSpec((1, bn), lambda j: (0, j)),
        ],
        out_specs=pl.BlockSpec((MOD_ROWS, 1, bn), lambda j: (0, 0, j)),
        out_shape=jax.ShapeDtypeStruct((MOD_ROWS, 1, n), F32),
        compiler_params=pltpu.CompilerParams(dimension_semantics=("arbitrary",),
                                             vmem_limit_bytes=VMEM_LIMIT),
        name="mod",
    )(c_rows, w_mod, b_mod)


def _norm_modulate(x, g, sh, sc):
    ms = jnp.mean(x * x, axis=-1, keepdims=True)
    y = x * lax.rsqrt(ms + EPS) * g
    return y * (1.0 + sc) + sh


def _store_attention_kv(k, v, ks_ref, vt_ref):
    rows = k.shape[0]
    low = lax.broadcasted_iota(jnp.int32, k.shape, 1) < ATT_HD
    k_sw = pltpu.roll(k, ATT_HD, 1)
    ks_ref[0] = jnp.where(low, k, k_sw).astype(BF16)
    ks_ref[1] = jnp.where(low, k_sw, k).astype(BF16)
    v_t = v.T
    ones = jnp.ones((BF16_SUBLANES, rows), BF16)
    for j in range(ATT_KV_HEADS):
        vt_ref[j, :ATT_HD, :] = v_t[j * ATT_HD:(j + 1) * ATT_HD].astype(BF16)
        vt_ref[j, ATT_HD:, :] = ones


def _ctx_proj_kernel(x_ref, sh_ref, sc_ref, g_ref, wrk_ref, wrv_ref, wa_ref, gk_ref,
                     ret_ref, ks_ref, vt_ref):
    h = _norm_modulate(x_ref[...], g_ref[...], sh_ref[...], sc_ref[...]).astype(BF16)
    ret_ref[:, :RET_W] = (_dot(h, wrk_ref[...]) * RET_SCALE).astype(BF16)
    ret_ref[:, RET_W:] = _dot(h, wrv_ref[...]).astype(BF16)
    pa = _dot(h, wa_ref[...])
    ak = pa[:, :LANES]
    r = _head_rms_scale(ak, _head_mean_matrix(LANES))
    _store_attention_kv(ak * r * gk_ref[...], pa[:, LANES:], ks_ref, vt_ref)


def _ctx_proj_call(ctx_rows, ctx_row, mod3, norm1_g, w_in, gk2):
    N, D = ctx_rows.shape
    tm = TM_CTX
    const = lambda i: (0, 0)
    kv_w = D_IN - AK_OFF
    assert RK_OFF % RET_W == 0 and RV_OFF % RET_W == 0 and AK_OFF % kv_w == 0
    return pl.pallas_call(
        _ctx_proj_kernel,
        grid=(N // tm,),
        in_specs=[
            pl.BlockSpec((tm, D), lambda i: (i, 0)),
            pl.BlockSpec((None, 1, D), lambda i: (ctx_row, 0, 0)),
            pl.BlockSpec((None, 1, D), lambda i: (ctx_row, 0, 1)),
            pl.BlockSpec((1, D), const),
            pl.BlockSpec((D, RET_W), lambda i: (0, RK_OFF // RET_W)),
            pl.BlockSpec((D, RET_W), lambda i: (0, RV_OFF // RET_W)),
            pl.BlockSpec((D, kv_w), lambda i: (0, AK_OFF // kv_w)),
            pl.BlockSpec((1, LANES), const),
        ],
        out_specs=[
            pl.BlockSpec((tm, 2 * RET_W), lambda i: (i, 0)),
            pl.BlockSpec((ATT_KV_HEADS, tm, LANES), lambda i: (0, i, 0)),
            pl.BlockSpec((ATT_KV_HEADS, ATT_VT_ROWS, tm), lambda i: (0, 0, i)),
        ],
        out_shape=[
            jax.ShapeDtypeStruct((N, 2 * RET_W), BF16),
            jax.ShapeDtypeStruct((ATT_KV_HEADS, N, LANES), BF16),
            jax.ShapeDtypeStruct((ATT_KV_HEADS, ATT_VT_ROWS, N), BF16),
        ],
        compiler_params=pltpu.CompilerParams(dimension_semantics=("arbitrary",),
                                             vmem_limit_bytes=VMEM_LIMIT),
        name="ctx_proj",
    )(ctx_rows, mod3, mod3, norm1_g, w_in, w_in, w_in, gk2)


def _in_proj_kernel(x_ref, sh_ref, sc_ref, g_ref, w_ref, tc_ref, ts_ref, oc_ref, os_ref, ocs_ref, oss_ref,
                    rc_ref, rs_ref, cc_ref, cs_ref, gq_ref, gk_ref, ret_ref, aq_ref, ks_ref, vt_ref):
    tm = x_ref.shape[0]
    tile = pl.program_id(1)
    h = _norm_modulate(x_ref[...], g_ref[...], sh_ref[...], sc_ref[...]).astype(BF16)
    a_c, a_s = tc_ref[pl.ds(tile, 1), :], ts_ref[pl.ds(tile, 1), :]
    cr = a_c * oc_ref[...] - a_s * os_ref[...]
    sr = a_s * ocs_ref[...] + a_c * oss_ref[...]
    lane = lax.broadcasted_iota(jnp.int32, (tm, LANES), 1)
    first_half = (lane & (ATT_HD // 2)) == 0

    grid_rows = tm // GRID_W
    row0 = tile * grid_rows
    col_lane = (lax.broadcasted_iota(jnp.int32, (GRID_W, LANES), 1) & (ATT_HD // 4)) != 0

    def axial(row_ref, col_ref):
        col_t = col_ref[...]
        return jnp.concatenate([jnp.where(col_lane, col_t, row_ref[pl.ds(row0 + r, 1), :])
                                for r in range(grid_rows)], axis=0)

    ca, sa = axial(rc_ref, cc_ref), axial(rs_ref, cs_ref)

    bd = _head_mean_matrix(MXU_DIM)
    gq, gk = gq_ref[...], gk_ref[...]
    pa = _dot(h, w_ref[:, AQ_OFF:AK_OFF])
    for j in range(ATT_W // MXU_DIM):
        blk = pa[:, j * MXU_DIM:(j + 1) * MXU_DIM]
        r = _head_rms_scale(blk, bd) * (ATT_SCALE * LOG2E)
        for i in range(MXU_DIM // LANES):
            sl = slice(i * LANES, (i + 1) * LANES)
            o = _rope_att(blk[:, sl] * gq, ca, sa, first_half) * r[:, sl]
            aq_ref[j * MXU_DIM + i * LANES:j * MXU_DIM + (i + 1) * LANES, :] = o.T.astype(BF16)

    pkv = _dot(h, w_ref[:, AK_OFF:D_IN])
    ak = pkv[:, :LANES]
    rk = _head_rms_scale(ak, bd[:LANES, :LANES])
    _store_attention_kv(_rope_att(ak * gk, ca, sa, first_half) * rk, pkv[:, LANES:], ks_ref, vt_ref)

    pq = _dot(h, w_ref[:, RQ_OFF:RK_OFF])
    for hh in range(RET_HEADS):
        sl = slice(hh * RET_DK, (hh + 1) * RET_DK)
        ret_ref[:, RQ_OFF + hh * RET_DK:RQ_OFF + (hh + 1) * RET_DK] = _rope_ret(pq[:, sl], cr, sr).astype(BF16)
    pk = _dot(h, w_ref[:, RK_OFF:RV_OFF])
    for hh in range(RET_HEADS):
        sl = slice(hh * RET_DK, (hh + 1) * RET_DK)
        ret_ref[:, RK_OFF + hh * RET_DK:RK_OFF + (hh + 1) * RET_DK] = (
            _rope_ret(pk[:, sl], cr, sr) * RET_SCALE).astype(BF16)
    ret_ref[:, RV_OFF:AQ_OFF] = _dot(h, w_ref[:, RV_OFF:AQ_OFF]).astype(BF16)


def _in_proj_call(x, mod3, norm1_g, w_in, ret_tables, axial_tables, gq2, gk2):
    B, L, D = x.shape
    tm = TM_IN
    assert tm % GRID_W == 0
    const = lambda b, i: (0, 0)
    tables = list(ret_tables) + list(axial_tables)
    return pl.pallas_call(
        _in_proj_kernel,
        grid=(B, L // tm),
        in_specs=[
            pl.BlockSpec((None, tm, D), lambda b, i: (b, i, 0)),
            pl.BlockSpec((None, 1, D), lambda b, i: (b, 0, 0)),
            pl.BlockSpec((None, 1, D), lambda b, i: (b, 0, 1)),
            pl.BlockSpec((1, D), const),
            pl.BlockSpec(w_in.shape, const),
            *[pl.BlockSpec(t.shape, const) for t in tables],
            pl.BlockSpec((1, LANES), const),
            pl.BlockSpec((1, LANES), const),
        ],
        out_specs=[
            pl.BlockSpec((None, tm, AQ_OFF), lambda b, i: (b, i, 0)),
            pl.BlockSpec((None, ATT_W, tm), lambda b, i: (b, 0, i)),
            pl.BlockSpec((None, ATT_KV_HEADS, tm, LANES), lambda b, i: (b, 0, i, 0)),
            pl.BlockSpec((None, ATT_KV_HEADS, ATT_VT_ROWS, tm), lambda b, i: (b, 0, 0, i)),
        ],
        out_shape=[
            jax.ShapeDtypeStruct((B, L, AQ_OFF), BF16),
            jax.ShapeDtypeStruct((B, ATT_W, L), BF16),
            jax.ShapeDtypeStruct((B, ATT_KV_HEADS, L, LANES), BF16),
            jax.ShapeDtypeStruct((B, ATT_KV_HEADS, ATT_VT_ROWS, L), BF16),
        ],
        compiler_params=pltpu.CompilerParams(dimension_semantics=("arbitrary", "arbitrary"),
                                             vmem_limit_bytes=VMEM_LIMIT),
        name="in_proj",
    )(x, mod3, mod3, norm1_g, w_in, *tables, gq2, gk2)


def _retention_kernel(q_ref, k_ref, v_ref, g_ref, kc_ref, vc_ref, rate_ref, gn_ref, o_ref,
                      u_ref, s_ref):
    L = q_ref.shape[0]
    Lc = kc_ref.shape[0]
    C = RET_CHUNK
    T = L // C

    lg = jnp.log1p(-jnp.exp(rate_ref[...]))
    lgf, lgb = lg[0], lg[1]
    lgf1, lgb1 = lgf[:, :RET_DK], lgb[:, :RET_DK]

    ri = lax.broadcasted_iota(jnp.int32, (C, C), 0)
    ci = lax.broadcasted_iota(jnp.int32, (C, C), 1)
    rel = (ri - ci).astype(F32)
    decay = (jnp.where(rel >= 0, jnp.exp(lgf * jnp.maximum(rel, 0.0)), 0.0)
             + jnp.where(rel <= 0, jnp.exp(lgb * jnp.maximum(-rel, 0.0)), 0.0))

    pos = lax.broadcasted_iota(jnp.int32, (C, RET_DK), 0).astype(F32)
    q_dec_f = jnp.exp(lgf1 * (pos + 1.0))
    q_dec_b = jnp.exp(lgb1 * (C - pos))
    k_dec_f = jnp.exp(lgf1 * (C - 1.0 - pos))
    k_dec_b = jnp.exp(lgb1 * pos)
    c_dec_f = jnp.exp(lgf1 * C)
    c_dec_b = jnp.exp(lgb1 * C)

    cpos = lax.broadcasted_iota(jnp.int32, (Lc, RET_DK), 0).astype(F32)
    kc = kc_ref[...].astype(F32)
    vc = vc_ref[...]
    s_f0 = _dot_tn((kc * jnp.exp(lgf1 * (Lc - 1.0 - cpos))).astype(BF16), vc)
    s_b0 = _dot_tn((kc * jnp.exp(lgb1 * cpos)).astype(BF16), vc)

    def rows(t):
        return slice(t * C, (t + 1) * C)

    for t in range(T):
        kf = k_ref[rows(t), :].astype(F32)
        kd = jnp.concatenate([(kf * k_dec_f).astype(BF16), (kf * k_dec_b).astype(BF16)], axis=1)
        u_ref[t] = _dot_tn(kd, v_ref[rows(t), :])

    s_f, s_b = s_f0, s_b0
    for t in range(T):
        tb = T - 1 - t
        s_ref[t, :RET_DK, :] = s_f.astype(BF16)
        s_ref[tb, RET_DK:, :] = s_b.astype(BF16)
        s_f = s_f * c_dec_f + u_ref[t, :RET_DK, :]
        s_b = s_b * c_dec_b + u_ref[tb, RET_DK:, :]

    gn = gn_ref[...]

    def qk(t):
        return _dot_nt(q_ref[rows(t), :], k_ref[rows(t), :])

    sc_next = qk(0)
    for t in range(T):
        sc = sc_next
        if t + 1 < T:
            sc_next = qk(t + 1)
        qf = q_ref[rows(t), :].astype(F32)
        lhs = jnp.concatenate([(sc * decay).astype(BF16), (qf * q_dec_f).astype(BF16),
                               (qf * q_dec_b).astype(BF16)], axis=1)
        rhs = jnp.concatenate([v_ref[rows(t), :], s_ref[t]], axis=0)
        o = _dot(lhs, rhs)
        mu = jnp.mean(o, axis=-1, keepdims=True)
        d = o - mu
        var = jnp.mean(d * d, axis=-1, keepdims=True)
        on = d * lax.rsqrt(var + EPS) * gn
        o_ref[rows(t), :] = (on * _silu(g_ref[rows(t), :].astype(F32))).astype(o_ref.dtype)


def _retention_call(ret_in, ctx_ret, rate, gn_g):
    B, L, _ = ret_in.shape
    Lc = ctx_ret.shape[0] // B
    H = RET_HEADS
    T = L // RET_CHUNK

    def col(off):
        return lambda b, h: (b, 0, off + h)

    return pl.pallas_call(
        _retention_kernel,
        grid=(B, H),
        in_specs=[
            pl.BlockSpec((None, L, RET_DK), col(RQ_OFF // RET_DK)),
            pl.BlockSpec((None, L, RET_DK), col(RK_OFF // RET_DK)),
            pl.BlockSpec((None, L, RET_DV), col(RV_OFF // RET_DV)),
            pl.BlockSpec((None, L, RET_DV), col(RG_OFF // RET_DV)),
            pl.BlockSpec((Lc, RET_DK), lambda b, h: (b, h)),
            pl.BlockSpec((Lc, RET_DV), lambda b, h: (b, RET_HEADS + h)),
            pl.BlockSpec((2, None, 1, RET_CHUNK), lambda b, h: (0, h, 0, 0)),
            pl.BlockSpec((1, RET_DV), lambda b, h: (0, h)),
        ],
        out_specs=pl.BlockSpec((None, L, RET_DV), col(0)),
        out_shape=jax.ShapeDtypeStruct((B, L, RET_W), BF16),
        scratch_shapes=[pltpu.VMEM((T, 2 * RET_DK, RET_DV), F32),
                        pltpu.VMEM((T, 2 * RET_DK, RET_DV), BF16)],
        compiler_params=pltpu.CompilerParams(dimension_semantics=("arbitrary", "arbitrary"),
                                             vmem_limit_bytes=VMEM_LIMIT),
        name="retention",
    )(ret_in, ret_in, ret_in, ret_in, ctx_ret, ctx_ret, rate, gn_g)


def _attention_kernel(q_ref, k_ref, vt_ref, kc_ref, vtc_ref, o_ref):
    Lc = kc_ref.shape[0]
    L = k_ref.shape[0]
    tq = q_ref.shape[1]
    ctx_tiles = Lc // ATT_KEYS
    n_tiles = ctx_tiles + L // ATT_KEYS

    def key_tile(c):
        if c < ctx_tiles:
            return kc_ref[c * ATT_KEYS:(c + 1) * ATT_KEYS, :]
        return k_ref[(c - ctx_tiles) * ATT_KEYS:(c - ctx_tiles + 1) * ATT_KEYS, :]

    def value_tile(c):
        if c < ctx_tiles:
            return vtc_ref[:, c * ATT_KEYS:(c + 1) * ATT_KEYS]
        return vt_ref[:, (c - ctx_tiles) * ATT_KEYS:(c - ctx_tiles + 1) * ATT_KEYS]

    streams = [(h, g) for h in range(tq // MXU_DIM) for g in range(ATT_GROUP)]
    pad = jnp.zeros((LANES - ATT_HD, MXU_DIM), BF16)

    def q_weights(h, g):
        return jnp.concatenate(
            [q_ref[g * ATT_HD:(g + 1) * ATT_HD, h * MXU_DIM:(h + 1) * MXU_DIM], pad], axis=0)

    wq = {st: q_weights(*st) for st in streams}

    def scores(c, st):
        s = _dot(key_tile(c), wq[st])
        return s, jnp.max(s, axis=0, keepdims=True)

    m = {}
    acc = {}

    def fold(c, st, s_mt):
        s, mt = s_mt
        v_c = value_tile(c)
        if c == 0:
            m[st] = mt
            acc[st] = _dot(v_c, jnp.exp2(s - mt).astype(BF16))
        else:
            m_new = jnp.maximum(m[st], mt)
            alpha = jnp.exp2(m[st] - m_new)
            acc[st] = acc[st] * alpha + _dot(v_c, jnp.exp2(s - m_new).astype(BF16))
            m[st] = m_new

    units = [(c, st) for c in range(n_tiles) for st in streams]
    pending = []
    for i in range(len(units) + ATT_AHEAD):
        if i < len(units):
            pending.append(scores(*units[i]))
        if i >= ATT_AHEAD:
            fold(*units[i - ATT_AHEAD], pending.pop(0))
    for h, g in streams:
        a = acc[(h, g)]
        o_ref[g * ATT_HD:(g + 1) * ATT_HD, h * MXU_DIM:(h + 1) * MXU_DIM] = (
            a[:ATT_HD] / a[ATT_HD:ATT_HD + 1]).astype(o_ref.dtype)


def _attention_call(aq_t, ks, vt, ctx_ks, ctx_vt):
    B, _, L = aq_t.shape
    Lc = ctx_ks.shape[1] // B
    tq = TQ_ATT
    assert L % ATT_KEYS == 0 and Lc % ATT_KEYS == 0 and tq % MXU_DIM == 0
    per_head = lambda b, kv, i: (b, kv, 0, 0)
    return pl.pallas_call(
        _attention_kernel,
        grid=(B, ATT_KV_HEADS, L // tq),
        in_specs=[
            pl.BlockSpec((None, ATT_GROUP * ATT_HD, tq), lambda b, kv, i: (b, kv, i)),
            pl.BlockSpec((None, None, L, LANES), per_head),
            pl.BlockSpec((None, None, ATT_VT_ROWS, L), per_head),
            pl.BlockSpec((None, Lc, LANES), lambda b, kv, i: (kv, b, 0)),
            pl.BlockSpec((None, ATT_VT_ROWS, Lc), lambda b, kv, i: (kv, 0, b)),
        ],
        out_specs=pl.BlockSpec((None, ATT_GROUP * ATT_HD, tq), lambda b, kv, i: (b, kv, i)),
        out_shape=jax.ShapeDtypeStruct((B, ATT_W, L), BF16),
        compiler_params=pltpu.CompilerParams(
            dimension_semantics=("arbitrary", "arbitrary", "arbitrary"),
            vmem_limit_bytes=VMEM_LIMIT),
        name="attention",
    )(aq_t, ks, vt, ctx_ks, ctx_vt)


def _out_ffn_kernel(x_ref, ret_ref, att_ref, g1_ref, sh2_ref, sc2_ref, g2_ref, n2_ref, nf_ref,
                    wo_r_ref, wo_a_ref, w1_ref, w2_ref, o_ref):
    tm = x_ref.shape[0]
    subs = [slice(s * TM_SUB, (s + 1) * TM_SUB) for s in range(tm // TM_SUB)]
    x1, h2 = [], []
    for r in subs:
        mix = _dot(ret_ref[r, :], wo_r_ref[...]) + _dot_tn(att_ref[:, r], wo_a_ref[...])
        x1.append(x_ref[r, :] + g1_ref[...] * mix)
        h2.append(_norm_modulate(x1[-1], n2_ref[...], sh2_ref[...], sc2_ref[...]).astype(BF16))
    for s, r in enumerate(subs):
        ff = jnp.zeros((TM_SUB, D_MODEL), F32)
        for j in range(D_FF // FF_CHUNK):
            a = jnp.maximum(_dot(h2[s], w1_ref[:, j * FF_CHUNK:(j + 1) * FF_CHUNK]), 0.0)
            ff = ff + _dot((a * a).astype(BF16), w2_ref[j * FF_CHUNK:(j + 1) * FF_CHUNK, :])
        x2 = x1[s] + g2_ref[...] * ff
        ms = jnp.mean(x2 * x2, axis=-1, keepdims=True)
        o_ref[r, :] = x2 * lax.rsqrt(ms + EPS) * nf_ref[...]


def _out_ffn_call(x, ret, att, mod3, norm2_g, final_g, wo, w1, w2):
    B, L, D = x.shape
    assert RET_W == ATT_W
    tm = TM_PROJ
    const = lambda b, i: (0, 0)
    once = pl.Buffered(1)

    def modrow(j):
        return pl.BlockSpec((None, 1, D), lambda b, i: (b, 0, j))

    return pl.pallas_call(
        _out_ffn_kernel,
        grid=(B, L // tm),
        in_specs=[
            pl.BlockSpec((None, tm, D), lambda b, i: (b, i, 0)),
            pl.BlockSpec((None, tm, RET_W), lambda b, i: (b, i, 0)),
            pl.BlockSpec((None, ATT_W, tm), lambda b, i: (b, 0, i)),
            modrow(2), modrow(3), modrow(4), modrow(5),
            pl.BlockSpec((1, D), const),
            pl.BlockSpec((1, D), const),
            pl.BlockSpec((RET_W, D), lambda b, i: (0, 0), pipeline_mode=once),
            pl.BlockSpec((ATT_W, D), lambda b, i: (1, 0), pipeline_mode=once),
            pl.BlockSpec(w1.shape, const, pipeline_mode=once),
            pl.BlockSpec(w2.shape, const, pipeline_mode=once),
        ],
        out_specs=pl.BlockSpec((None, tm, D), lambda b, i: (b, i, 0)),
        out_shape=jax.ShapeDtypeStruct((B, L, D), F32),
        compiler_params=pltpu.CompilerParams(dimension_semantics=("arbitrary", "arbitrary"),
                                             vmem_limit_bytes=VMEM_LIMIT),
        name="out_ffn",
    )(x, ret, att, mod3, mod3, mod3, mod3, norm2_g, final_g, wo, wo, w1, w2)


def _freqs(n_pairs):
    return ROPE_BASE ** (-jnp.arange(n_pairs, dtype=F32) / n_pairs)


def _rope_tables(L, tm):
    lane = jnp.arange(LANES)
    fr = jnp.tile(_freqs(RET_DK // 2), 2)[None, :]
    sign = jnp.where(lane < RET_DK // 2, -1.0, 1.0).astype(F32)[None, :]
    base = (jnp.arange(L // tm) * tm).astype(F32)[:, None] * fr
    off = jnp.arange(tm, dtype=F32)[:, None] * fr
    ret = (jnp.cos(base), jnp.sin(base), jnp.cos(off), jnp.sin(off),
           sign * jnp.cos(off), sign * jnp.sin(off))
    freq = jnp.tile(_freqs(ATT_HD // 4), LANES // (ATT_HD // 4))[None, :]
    neg = (lane % ATT_HD < ATT_HD // 2)[None, :]

    def axis_tables(n):
        a = jnp.arange(n, dtype=F32)[:, None] * freq
        return jnp.cos(a), jnp.where(neg, -jnp.sin(a), jnp.sin(a))

    row_c, row_s = axis_tables(L // GRID_W)
    col_c, col_s = axis_tables(GRID_W)
    return ret, (row_c, row_s, col_c, col_s)


def kernel(x, c, ctx, c_ctx, w_mod, b_mod, norm1_g, norm2_g, w_in, w_out, ret_log_rate, ret_gn_g,
           q_norm_g, k_norm_g, w_ff1, w_ff2, final_norm_g):
    B, L, D = x.shape
    assert w_mod.shape[0] == 1, "single-layer configuration"
    assert B + 1 <= MOD_ROWS and L % TM_PROJ == 0 and L % TM_IN == 0 and L % TQ_ATT == 0 and L % RET_CHUNK == 0

    c_rows = jnp.zeros((MOD_ROWS, D), F32).at[:B].set(c).at[B].set(c_ctx)
    mod3 = _mod_call(c_rows, w_mod[0], b_mod[0][None, :])

    w_in_b = w_in[0].astype(BF16)
    wo = w_out[0].astype(BF16)
    w1 = w_ff1[0].astype(BF16)
    w2 = w_ff2[0].astype(BF16)
    n1 = norm1_g[0][None, :]
    n2 = norm2_g[0][None, :]
    nf = final_norm_g[None, :]
    gq2 = jnp.tile(q_norm_g[0], LANES // ATT_HD)[None, :]
    gk2 = jnp.tile(k_norm_g[0], LANES // ATT_HD)[None, :]
    gn = ret_gn_g[0][None, :]
    rate = jnp.broadcast_to(ret_log_rate[0].astype(F32)[:, :, None, None],
                            (2, RET_HEADS, 1, RET_CHUNK))
    ret_tables, axial_tables = _rope_tables(L, TM_IN)

    Lc = ctx.shape[1]
    assert (B * Lc) % TM_CTX == 0
    ctx_ret, ctx_ks, ctx_vt = _ctx_proj_call(ctx.reshape(B * Lc, D), B, mod3, n1, w_in_b, gk2)
    ret_in, aq_t, ks, vt = _in_proj_call(x, mod3, n1, w_in_b, ret_tables, axial_tables, gq2, gk2)
    ret = _retention_call(ret_in, ctx_ret, rate, gn)
    att = _attention_call(aq_t, ks, vt, ctx_ks, ctx_vt)
    return _out_ffn_call(x, ret, att, mod3, n2, nf, wo, w1, w2)
```

```python
import math

import jax
import jax.numpy as jnp
from jax import lax
from jax.experimental import pallas as pl
from jax.experimental.pallas import tpu as pltpu

D_MODEL = 1024
GRID_W = 64

RET_HEADS = 4
RET_DK = 128
RET_DV = 128
RET_W = RET_HEADS * RET_DV
RET_SCALE = RET_DK ** -0.5

ATT_HEADS = 8
ATT_KV_HEADS = 2
ATT_GROUP = ATT_HEADS // ATT_KV_HEADS
ATT_HD = 64
ATT_W = ATT_HEADS * ATT_HD
ATT_SCALE = ATT_HD ** -0.5

MIX_W = RET_W + ATT_W
D_FF = 4 * D_MODEL
ROPE_BASE = 10000.0
EPS = 1e-6
LOG2E = math.log2(math.e)

RQ_OFF = 0
RK_OFF = RQ_OFF + RET_HEADS * RET_DK
RV_OFF = RK_OFF + RET_HEADS * RET_DK
RG_OFF = RV_OFF + RET_W
AQ_OFF = RG_OFF + RET_W
AK_OFF = AQ_OFF + ATT_W
AV_OFF = AK_OFF + ATT_KV_HEADS * ATT_HD
D_IN = AV_OFF + ATT_KV_HEADS * ATT_HD

LANES = 128
BF16_SUBLANES = 16
ATT_VT_ROWS = ATT_HD + BF16_SUBLANES
MXU_DIM = 256
V7X_VMEM_BYTES = 64 * 1024 * 1024
VMEM_LIMIT = V7X_VMEM_BYTES * 7 // 8

MOD_ROWS = 16
MOD_BN = 2048
TM_CTX = 1024
TM_IN = 512
TM_PROJ = 1024
TM_SUB = 256
TQ_ATT = 1024
ATT_KEYS = 256
ATT_AHEAD = 6
RET_CHUNK = 256
FF_CHUNK = 1024

F32 = jnp.float32
BF16 = jnp.bfloat16


def _dot(a, b):
    return jnp.dot(a, b, preferred_element_type=F32)


def _dot_nt(a, b):
    return lax.dot_general(a, b, (((1,), (1,)), ((), ())), preferred_element_type=F32)


def _dot_tn(a, b):
    return lax.dot_general(a, b, (((0,), (0,)), ((), ())), preferred_element_type=F32)


def _silu(x):
    return x * jax.nn.sigmoid(x)


def _head_mean_matrix(n):
    r = lax.broadcasted_iota(jnp.int32, (n, n), 0) // ATT_HD
    c = lax.broadcasted_iota(jnp.int32, (n, n), 1) // ATT_HD
    return jnp.where(r == c, 1.0 / ATT_HD, 0.0).astype(BF16)


def _head_rms_scale(blk, bd):
    ms = _dot((blk * blk).astype(BF16), bd)
    return lax.rsqrt(ms + EPS)


def _rope_ret(blk, c2, s2):
    return blk * c2 + pltpu.roll(blk, RET_DK // 2, 1) * s2


def _rope_att(blk, c2, s2, first_half):
    up = pltpu.roll(blk, ATT_HD // 2, 1)
    dn = pltpu.roll(blk, LANES - ATT_HD // 2, 1)
    return blk * c2 + jnp.where(first_half, dn, up) * s2


def _split_bf16(x):
    hi = x.astype(BF16)
    return hi, (x - hi.astype(F32)).astype(BF16)


def _mod_kernel(c_ref, w_ref, b_ref, o_ref):
    a_hi, a_lo = _split_bf16(_silu(c_ref[...]))
    w_hi, w_lo = _split_bf16(w_ref[...])
    by_hi = _dot(jnp.concatenate([a_hi, a_lo], axis=0), w_hi)
    out = by_hi[:MOD_ROWS] + by_hi[MOD_ROWS:] + _dot(a_hi, w_lo) + b_ref[...]
    for r in range(MOD_ROWS):
        o_ref[r] = out[r:r + 1, :]


def _mod_call(c_rows, w_mod, b_mod):
    n = w_mod.shape[1]
    bn = MOD_BN
    return pl.pallas_call(
        _mod_kernel,
        grid=(n // bn,),
        in_specs=[
            pl.BlockSpec((MOD_ROWS, D_MODEL), lambda j: (0, 0)),
            pl.BlockSpec((D_MODEL, bn), lambda j: (0, j)),
            pl.BlockSpec((1, bn), lambda j: (0, j)),
        ],
        out_specs=pl.BlockSpec((MOD_ROWS, 1, bn), lambda j: (0, 0, j)),
        out_shape=jax.ShapeDtypeStruct((MOD_ROWS, 1, n), F32),
        compiler_params=pltpu.CompilerParams(dimension_semantics=("arbitrary",),
                                             vmem_limit_bytes=VMEM_LIMIT),
        name="mod",
    )(c_rows, w_mod, b_mod)


def _norm_modulate(x, g, sh, sc):
    ms = jnp.mean(x * x, axis=-1, keepdims=True)
    y = x * lax.rsqrt(ms + EPS) * g
    return y * (1.0 + sc) + sh


def _store_attention_kv(k, v, ks_ref, vt_ref):
    rows = k.shape[0]
    low = lax.broadcasted_iota(jnp.int32, k.shape, 1) < ATT_HD
    k_sw = pltpu.roll(k, ATT_HD, 1)
    ks_ref[0] = jnp.where(low, k, k_sw).astype(BF16)
    ks_ref[1] = jnp.where(low, k_sw, k).astype(BF16)
    v_t = v.T
    ones = jnp.ones((BF16_SUBLANES, rows), BF16)
    for j in range(ATT_KV_HEADS):
        vt_ref[j, :ATT_HD, :] = v_t[j * ATT_HD:(j + 1) * ATT_HD].astype(BF16)
        vt_ref[j, ATT_HD:, :] = ones


def _ctx_proj_kernel(x_ref, sh_ref, sc_ref, g_ref, wrk_ref, wrv_ref, wa_ref, gk_ref,
                     ret_ref, ks_ref, vt_ref):
    h = _norm_modulate(x_ref[...], g_ref[...], sh_ref[...], sc_ref[...]).astype(BF16)
    ret_ref[:, :RET_W] = (_dot(h, wrk_ref[...]) * RET_SCALE).astype(BF16)
    ret_ref[:, RET_W:] = _dot(h, wrv_ref[...]).astype(BF16)
    pa = _dot(h, wa_ref[...])
    ak = pa[:, :LANES]
    r = _head_rms_scale(ak, _head_mean_matrix(LANES))
    _store_attention_kv(ak * r * gk_ref[...], pa[:, LANES:], ks_ref, vt_ref)


def _ctx_proj_call(ctx_rows, ctx_row, mod3, norm1_g, w_in, gk2):
    N, D = ctx_rows.shape
    tm = TM_CTX
    const = lambda i: (0, 0)
    kv_w = D_IN - AK_OFF
    assert RK_OFF % RET_W == 0 and RV_OFF % RET_W == 0 and AK_OFF % kv_w == 0
    return pl.pallas_call(
        _ctx_proj_kernel,
        grid=(N // tm,),
        in_specs=[
            pl.BlockSpec((tm, D), lambda i: (i, 0)),
            pl.BlockSpec((None, 1, D), lambda i: (ctx_row, 0, 0)),
            pl.BlockSpec((None, 1, D), lambda i: (ctx_row, 0, 1)),
            pl.BlockSpec((1, D), const),
            pl.BlockSpec((D, RET_W), lambda i: (0, RK_OFF // RET_W)),
            pl.BlockSpec((D, RET_W), lambda i: (0, RV_OFF // RET_W)),
            pl.BlockSpec((D, kv_w), lambda i: (0, AK_OFF // kv_w)),
            pl.BlockSpec((1, LANES), const),
        ],
        out_specs=[
            pl.BlockSpec((tm, 2 * RET_W), lambda i: (i, 0)),
            pl.BlockSpec((ATT_KV_HEADS, tm, LANES), lambda i: (0, i, 0)),
            pl.BlockSpec((ATT_KV_HEADS, ATT_VT_ROWS, tm), lambda i: (0, 0, i)),
        ],
        out_shape=[
            jax.ShapeDtypeStruct((N, 2 * RET_W), BF16),
            jax.ShapeDtypeStruct((ATT_KV_HEADS, N, LANES), BF16),
            jax.ShapeDtypeStruct((ATT_KV_HEADS, ATT_VT_ROWS, N), BF16),
        ],
        compiler_params=pltpu.CompilerParams(dimension_semantics=("arbitrary",),
                                             vmem_limit_bytes=VMEM_LIMIT),
        name="ctx_proj",
    )(ctx_rows, mod3, mod3, norm1_g, w_in, w_in, w_in, gk2)


def _in_proj_kernel(x_ref, sh_ref, sc_ref, g_ref, w_ref, tc_ref, ts_ref, oc_ref, os_ref, ocs_ref, oss_ref,
                    rc_ref, rs_ref, cc_ref, cs_ref, gq_ref, gk_ref, ret_ref, aq_ref, ks_ref, vt_ref):
    tm = x_ref.shape[0]
    tile = pl.program_id(1)
    h = _norm_modulate(x_ref[...], g_ref[...], sh_ref[...], sc_ref[...]).astype(BF16)
    a_c, a_s = tc_ref[pl.ds(tile, 1), :], ts_ref[pl.ds(tile, 1), :]
    cr = a_c * oc_ref[...] - a_s * os_ref[...]
    sr = a_s * ocs_ref[...] + a_c * oss_ref[...]
    lane = lax.broadcasted_iota(jnp.int32, (tm, LANES), 1)
    first_half = (lane & (ATT_HD // 2)) == 0

    grid_rows = tm // GRID_W
    row0 = tile * grid_rows
    col_lane = (lax.broadcasted_iota(jnp.int32, (GRID_W, LANES), 1) & (ATT_HD // 4)) != 0

    def axial(row_ref, col_ref):
        col_t = col_ref[...]
        return jnp.concatenate([jnp.where(col_lane, col_t, row_ref[pl.ds(row0 + r, 1), :])
                                for r in range(grid_rows)], axis=0)

    ca, sa = axial(rc_ref, cc_ref), axial(rs_ref, cs_ref)

    bd = _head_mean_matrix(MXU_DIM)
    gq, gk = gq_ref[...], gk_ref[...]
    pa = _dot(h, w_ref[:, AQ_OFF:AK_OFF])
    for j in range(ATT_W // MXU_DIM):
        blk = pa[:, j * MXU_DIM:(j + 1) * MXU_DIM]
        r = _head_rms_scale(blk, bd) * (ATT_SCALE * LOG2E)
        for i in range(MXU_DIM // LANES):
            sl = slice(i * LANES, (i + 1) * LANES)
            o = _rope_att(blk[:, sl] * gq, ca, sa, first_half) * r[:, sl]
            aq_ref[j * MXU_DIM + i * LANES:j * MXU_DIM + (i + 1) * LANES, :] = o.T.astype(BF16)

    pkv = _dot(h, w_ref[:, AK_OFF:D_IN])
    ak = pkv[:, :LANES]
    rk = _head_rms_scale(ak, bd[:LANES, :LANES])
    _store_attention_kv(_rope_att(ak * gk, ca, sa, first_half) * rk, pkv[:, LANES:], ks_ref, vt_ref)

    pq = _dot(h, w_ref[:, RQ_OFF:RK_OFF])
    for hh in range(RET_HEADS):
        sl = slice(hh * RET_DK, (hh + 1) * RET_DK)
        ret_ref[:, RQ_OFF + hh * RET_DK:RQ_OFF + (hh + 1) * RET_DK] = _rope_ret(pq[:, sl], cr, sr).astype(BF16)
    pk = _dot(h, w_ref[:, RK_OFF:RV_OFF])
    for hh in range(RET_HEADS):
        sl = slice(hh * RET_DK, (hh + 1) * RET_DK)
        ret_ref[:, RK_OFF + hh * RET_DK:RK_OFF + (hh + 1) * RET_DK] = (
            _rope_ret(pk[:, sl], cr, sr) * RET_SCALE).astype(BF16)
    ret_ref[:, RV_OFF:AQ_OFF] = _dot(h, w_ref[:, RV_OFF:AQ_OFF]).astype(BF16)


def _in_proj_call(x, mod3, norm1_g, w_in, ret_tables, axial_tables, gq2, gk2):
    B, L, D = x.shape
    tm = TM_IN
    assert tm % GRID_W == 0
    const = lambda b, i: (0, 0)
    tables = list(ret_tables) + list(axial_tables)
    return pl.pallas_call(
        _in_proj_kernel,
        grid=(B, L // tm),
        in_specs=[
            pl.BlockSpec((None, tm, D), lambda b, i: (b, i, 0)),
            pl.BlockSpec((None, 1, D), lambda b, i: (b, 0, 0)),
            pl.BlockSpec((None, 1, D), lambda b, i: (b, 0, 1)),
            pl.BlockSpec((1, D), const),
            pl.BlockSpec(w_in.shape, const),
            *[pl.BlockSpec(t.shape, const) for t in tables],
            pl.BlockSpec((1, LANES), const),
            pl.BlockSpec((1, LANES), const),
        ],
        out_specs=[
            pl.BlockSpec((None, tm, AQ_OFF), lambda b, i: (b, i, 0)),
            pl.BlockSpec((None, ATT_W, tm), lambda b, i: (b, 0, i)),
            pl.BlockSpec((None, ATT_KV_HEADS, tm, LANES), lambda b, i: (b, 0, i, 0)),
            pl.BlockSpec((None, ATT_KV_HEADS, ATT_VT_ROWS, tm), lambda b, i: (b, 0, 0, i)),
        ],
        out_shape=[
            jax.ShapeDtypeStruct((B, L, AQ_OFF), BF16),
            jax.ShapeDtypeStruct((B, ATT_W, L), BF16),
            jax.ShapeDtypeStruct((B, ATT_KV_HEADS, L, LANES), BF16),
            jax.ShapeDtypeStruct((B, ATT_KV_HEADS, ATT_VT_ROWS, L), BF16),
        ],
        compiler_params=pltpu.CompilerParams(dimension_semantics=("arbitrary", "arbitrary"),
                                             vmem_limit_bytes=VMEM_LIMIT),
        name="in_proj",
    )(x, mod3, mod3, norm1_g, w_in, *tables, gq2, gk2)


def _retention_kernel(q_ref, k_ref, v_ref, g_ref, kc_ref, vc_ref, rate_ref, gn_ref, o_ref,
                      u_ref, s_ref):
    L = q_ref.shape[0]
    Lc = kc_ref.shape[0]
    C = RET_CHUNK
    T = L // C

    lg = jnp.log1p(-jnp.exp(rate_ref[...]))
    lgf, lgb = lg[0], lg[1]
    lgf1, lgb1 = lgf[:, :RET_DK], lgb[:, :RET_DK]

    ri = lax.broadcasted_iota(jnp.int32, (C, C), 0)
    ci = lax.broadcasted_iota(jnp.int32, (C, C), 1)
    rel = (ri - ci).astype(F32)
    decay = (jnp.where(rel >= 0, jnp.exp(lgf * jnp.maximum(rel, 0.0)), 0.0)
             + jnp.where(rel <= 0, jnp.exp(lgb * jnp.maximum(-rel, 0.0)), 0.0))

    pos = lax.broadcasted_iota(jnp.int32, (C, RET_DK), 0).astype(F32)
    q_dec_f = jnp.exp(lgf1 * (pos + 1.0))
    q_dec_b = jnp.exp(lgb1 * (C - pos))
    k_dec_f = jnp.exp(lgf1 * (C - 1.0 - pos))
    k_dec_b = jnp.exp(lgb1 * pos)
    c_dec_f = jnp.exp(lgf1 * C)
    c_dec_b = jnp.exp(lgb1 * C)

    cpos = lax.broadcasted_iota(jnp.int32, (Lc, RET_DK), 0).astype(F32)
    kc = kc_ref[...].astype(F32)
    vc = vc_ref[...]
    s_f0 = _dot_tn((kc * jnp.exp(lgf1 * (Lc - 1.0 - cpos))).astype(BF16), vc)
    s_b0 = _dot_tn((kc * jnp.exp(lgb1 * cpos)).astype(BF16), vc)

    def rows(t):
        return slice(t * C, (t + 1) * C)

    for t in range(T):
        kf = k_ref[rows(t), :].astype(F32)
        kd = jnp.concatenate([(kf * k_dec_f).astype(BF16), (kf * k_dec_b).astype(BF16)], axis=1)
        u_ref[t] = _dot_tn(kd, v_ref[rows(t), :])

    s_f, s_b = s_f0, s_b0
    for t in range(T):
        tb = T - 1 - t
        s_ref[t, :RET_DK, :] = s_f.astype(BF16)
        s_ref[tb, RET_DK:, :] = s_b.astype(BF16)
        s_f = s_f * c_dec_f + u_ref[t, :RET_DK, :]
        s_b = s_b * c_dec_b + u_ref[tb, RET_DK:, :]

    gn = gn_ref[...]

    def qk(t):
        return _dot_nt(q_ref[rows(t), :], k_ref[rows(t), :])

    sc_next = qk(0)
    for t in range(T):
        sc = sc_next
        if t + 1 < T:
            sc_next = qk(t + 1)
        qf = q_ref[rows(t), :].astype(F32)
        lhs = jnp.concatenate([(sc * decay).astype(BF16), (qf * q_dec_f).astype(BF16),
                               (qf * q_dec_b).astype(BF16)], axis=1)
        rhs = jnp.concatenate([v_ref[rows(t), :], s_ref[t]], axis=0)
        o = _dot(lhs, rhs)
        mu = jnp.mean(o, axis=-1, keepdims=True)
        d = o - mu
        var = jnp.mean(d * d, axis=-1, keepdims=True)
        on = d * lax.rsqrt(var + EPS) * gn
        o_ref[rows(t), :] = (on * _silu(g_ref[rows(t), :].astype(F32))).astype(o_ref.dtype)


def _retention_call(ret_in, ctx_ret, rate, gn_g):
    B, L, _ = ret_in.shape
    Lc = ctx_ret.shape[0] // B
    H = RET_HEADS
    T = L // RET_CHUNK

    def col(off):
        return lambda b, h: (b, 0, off + h)

    return pl.pallas_call(
        _retention_kernel,
        grid=(B, H),
        in_specs=[
            pl.BlockSpec((None, L, RET_DK), col(RQ_OFF // RET_DK)),
            pl.BlockSpec((None, L, RET_DK), col(RK_OFF // RET_DK)),
            pl.BlockSpec((None, L, RET_DV), col(RV_OFF // RET_DV)),
            pl.BlockSpec((None, L, RET_DV), col(RG_OFF // RET_DV)),
            pl.BlockSpec((Lc, RET_DK), lambda b, h: (b, h)),
            pl.BlockSpec((Lc, RET_DV), lambda b, h: (b, RET_HEADS + h)),
            pl.BlockSpec((2, None, 1, RET_CHUNK), lambda b, h: (0, h, 0, 0)),
            pl.BlockSpec((1, RET_DV), lambda b, h: (0, h)),
        ],
        out_specs=pl.BlockSpec((None, L, RET_DV), col(0)),
        out_shape=jax.ShapeDtypeStruct((B, L, RET_W), BF16),
        scratch_shapes=[pltpu.VMEM((T, 2 * RET_DK, RET_DV), F32),
                        pltpu.VMEM((T, 2 * RET_DK, RET_DV), BF16)],
        compiler_params=pltpu.CompilerParams(dimension_semantics=("arbitrary", "arbitrary"),
                                             vmem_limit_bytes=VMEM_LIMIT),
        name="retention",
    )(ret_in, ret_in, ret_in, ret_in, ctx_ret, ctx_ret, rate, gn_g)


def _attention_kernel(q_ref, k_ref, vt_ref, kc_ref, vtc_ref, o_ref):
    Lc = kc_ref.shape[0]
    L = k_ref.shape[0]
    tq = q_ref.shape[1]
    ctx_tiles = Lc // ATT_KEYS
    n_tiles = ctx_tiles + L // ATT_KEYS

    def key_tile(c):
        if c < ctx_tiles:
            return kc_ref[c * ATT_KEYS:(c + 1) * ATT_KEYS, :]
        return k_ref[(c - ctx_tiles) * ATT_KEYS:(c - ctx_tiles + 1) * ATT_KEYS, :]

    def value_tile(c):
        if c < ctx_tiles:
            return vtc_ref[:, c * ATT_KEYS:(c + 1) * ATT_KEYS]
        return vt_ref[:, (c - ctx_tiles) * ATT_KEYS:(c - ctx_tiles + 1) * ATT_KEYS]

    streams = [(h, g) for h in range(tq // MXU_DIM) for g in range(ATT_GROUP)]
    pad = jnp.zeros((LANES - ATT_HD, MXU_DIM), BF16)

    def q_weights(h, g):
        return jnp.concatenate(
            [q_ref[g * ATT_HD:(g + 1) * ATT_HD, h * MXU_DIM:(h + 1) * MXU_DIM], pad], axis=0)

    wq = {st: q_weights(*st) for st in streams}

    def scores(c, st):
        s = _dot(key_tile(c), wq[st])
        return s, jnp.max(s, axis=0, keepdims=True)

    m = {}
    acc = {}

    def fold(c, st, s_mt):
        s, mt = s_mt
        v_c = value_tile(c)
        if c == 0:
            m[st] = mt
            acc[st] = _dot(v_c, jnp.exp2(s - mt).astype(BF16))
        else:
            m_new = jnp.maximum(m[st], mt)
            alpha = jnp.exp2(m[st] - m_new)
            acc[st] = acc[st] * alpha + _dot(v_c, jnp.exp2(s - m_new).astype(BF16))
            m[st] = m_new

    units = [(c, st) for c in range(n_tiles) for st in streams]
    pending = []
    for i in range(len(units) + ATT_AHEAD):
        if i < len(units):
            pending.append(scores(*units[i]))
        if i >= ATT_AHEAD:
            fold(*units[i - ATT_AHEAD], pending.pop(0))
    for h, g in streams:
        a = acc[(h, g)]
        o_ref[g * ATT_HD:(g + 1) * ATT_HD, h * MXU_DIM:(h + 1) * MXU_DIM] = (
            a[:ATT_HD] / a[ATT_HD:ATT_HD + 1]).astype(o_ref.dtype)


def _attention_call(aq_t, ks, vt, ctx_ks, ctx_vt):
    B, _, L = aq_t.shape
    Lc = ctx_ks.shape[1] // B
    tq = TQ_ATT
    assert L % ATT_KEYS == 0 and Lc % ATT_KEYS == 0 and tq % MXU_DIM == 0
    per_head = lambda b, kv, i: (b, kv, 0, 0)
    return pl.pallas_call(
        _attention_kernel,
        grid=(B, ATT_KV_HEADS, L // tq),
        in_specs=[
            pl.BlockSpec((None, ATT_GROUP * ATT_HD, tq), lambda b, kv, i: (b, kv, i)),
            pl.BlockSpec((None, None, L, LANES), per_head),
            pl.BlockSpec((None, None, ATT_VT_ROWS, L), per_head),
            pl.BlockSpec((None, Lc, LANES), lambda b, kv, i: (kv, b, 0)),
            pl.BlockSpec((None, ATT_VT_ROWS, Lc), lambda b, kv, i: (kv, 0, b)),
        ],
        out_specs=pl.BlockSpec((None, ATT_GROUP * ATT_HD, tq), lambda b, kv, i: (b, kv, i)),
        out_shape=jax.ShapeDtypeStruct((B, ATT_W, L), BF16),
        compiler_params=pltpu.CompilerParams(
            dimension_semantics=("arbitrary", "arbitrary", "arbitrary"),
            vmem_limit_bytes=VMEM_LIMIT),
        name="attention",
    )(aq_t, ks, vt, ctx_ks, ctx_vt)


def _out_ffn_kernel(x_ref, ret_ref, att_ref, g1_ref, sh2_ref, sc2_ref, g2_ref, n2_ref, nf_ref,
                    wo_r_ref, wo_a_ref, w1_ref, w2_ref, o_ref):
    tm = x_ref.shape[0]
    subs = [slice(s * TM_SUB, (s + 1) * TM_SUB) for s in range(tm // TM_SUB)]
    x1, h2 = [], []
    for r in subs:
        mix = _dot(ret_ref[r, :], wo_r_ref[...]) + _dot_tn(att_ref[:, r], wo_a_ref[...])
        x1.append(x_ref[r, :] + g1_ref[...] * mix)
        h2.append(_norm_modulate(x1[-1], n2_ref[...], sh2_ref[...], sc2_ref[...]).astype(BF16))
    for s, r in enumerate(subs):
        ff = jnp.zeros((TM_SUB, D_MODEL), F32)
        for j in range(D_FF // FF_CHUNK):
            a = jnp.maximum(_dot(h2[s], w1_ref[:, j * FF_CHUNK:(j + 1) * FF_CHUNK]), 0.0)
            ff = ff + _dot((a * a).astype(BF16), w2_ref[j * FF_CHUNK:(j + 1) * FF_CHUNK, :])
        x2 = x1[s] + g2_ref[...] * ff
        ms = jnp.mean(x2 * x2, axis=-1, keepdims=True)
        o_ref[r, :] = x2 * lax.rsqrt(ms + EPS) * nf_ref[...]


def _out_ffn_call(x, ret, att, mod3, norm2_g, final_g, wo, w1, w2):
    B, L, D = x.shape
    assert RET_W == ATT_W
    tm = TM_PROJ
    const = lambda b, i: (0, 0)
    once = pl.Buffered(1)

    def modrow(j):
        return pl.BlockSpec((None, 1, D), lambda b, i: (b, 0, j))

    return pl.pallas_call(
        _out_ffn_kernel,
        grid=(B, L // tm),
        in_specs=[
            pl.BlockSpec((None, tm, D), lambda b, i: (b, i, 0)),
            pl.BlockSpec((None, tm, RET_W), lambda b, i: (b, i, 0)),
            pl.BlockSpec((None, ATT_W, tm), lambda b, i: (b, 0, i)),
            modrow(2), modrow(3), modrow(4), modrow(5),
            pl.BlockSpec((1, D), const),
            pl.BlockSpec((1, D), const),
            pl.BlockSpec((RET_W, D), lambda b, i: (0, 0), pipeline_mode=once),
            pl.BlockSpec((ATT_W, D), lambda b, i: (1, 0), pipeline_mode=once),
            pl.BlockSpec(w1.shape, const, pipeline_mode=once),
            pl.BlockSpec(w2.shape, const, pipeline_mode=once),
        ],
        out_specs=pl.BlockSpec((None, tm, D), lambda b, i: (b, i, 0)),
        out_shape=jax.ShapeDtypeStruct((B, L, D), F32),
        compiler_params=pltpu.CompilerParams(dimension_semantics=("arbitrary", "arbitrary"),
                                             vmem_limit_bytes=VMEM_LIMIT),
        name="out_ffn",
    )(x, ret, att, mod3, mod3, mod3, mod3, norm2_g, final_g, wo, wo, w1, w2)


def _freqs(n_pairs):
    return ROPE_BASE ** (-jnp.arange(n_pairs, dtype=F32) / n_pairs)


def _rope_tables(L, tm):
    lane = jnp.arange(LANES)
    fr = jnp.tile(_freqs(RET_DK // 2), 2)[None, :]
    sign = jnp.where(lane < RET_DK // 2, -1.0, 1.0).astype(F32)[None, :]
    base = (jnp.arange(L // tm) * tm).astype(F32)[:, None] * fr
    off = jnp.arange(tm, dtype=F32)[:, None] * fr
    ret = (jnp.cos(base), jnp.sin(base), jnp.cos(off), jnp.sin(off),
           sign * jnp.cos(off), sign * jnp.sin(off))
    freq = jnp.tile(_freqs(ATT_HD // 4), LANES // (ATT_HD // 4))[None, :]
    neg = (lane % ATT_HD < ATT_HD // 2)[None, :]

    def axis_tables(n):
        a = jnp.arange(n, dtype=F32)[:, None] * freq
        return jnp.cos(a), jnp.where(neg, -jnp.sin(a), jnp.sin(a))

    row_c, row_s = axis_tables(L // GRID_W)
    col_c, col_s = axis_tables(GRID_W)
    return ret, (row_c, row_s, col_c, col_s)


def kernel(x, c, ctx, c_ctx, w_mod, b_mod, norm1_g, norm2_g, w_in, w_out, ret_log_rate, ret_gn_g,
           q_norm_g, k_norm_g, w_ff1, w_ff2, final_norm_g):
    B, L, D = x.shape
    assert w_mod.shape[0] == 1, "single-layer configuration"
    assert B + 1 <= MOD_ROWS and L % TM_PROJ == 0 and L % TM_IN == 0 and L % TQ_ATT == 0 and L % RET_CHUNK == 0

    c_rows = jnp.zeros((MOD_ROWS, D), F32).at[:B].set(c).at[B].set(c_ctx)
    mod3 = _mod_call(c_rows, w_mod[0], b_mod[0][None, :])

    w_in_b = w_in[0].astype(BF16)
    wo = w_out[0].astype(BF16)
    w1 = w_ff1[0].astype(BF16)
    w2 = w_ff2[0].astype(BF16)
    n1 = norm1_g[0][None, :]
    n2 = norm2_g[0][None, :]
    nf = final_norm_g[None, :]
    gq2 = jnp.tile(q_norm_g[0], LANES // ATT_HD)[None, :]
    gk2 = jnp.tile(k_norm_g[0], LANES // ATT_HD)[None, :]
    gn = ret_gn_g[0][None, :]
    rate = jnp.broadcast_to(ret_log_rate[0].astype(F32)[:, :, None, None],
                            (2, RET_HEADS, 1, RET_CHUNK))
    ret_tables, axial_tables = _rope_tables(L, TM_IN)

    Lc = ctx.shape[1]
    assert (B * Lc) % TM_CTX == 0
    ctx_ret, ctx_ks, ctx_vt = _ctx_proj_call(ctx.reshape(B * Lc, D), B, mod3, n1, w_in_b, gk2)
    ret_in, aq_t, ks, vt = _in_proj_call(x, mod3, n1, w_in_b, ret_tables, axial_tables, gq2, gk2)
    ret = _retention_call(ret_in, ctx_ret, rate, gn)
    att = _attention_call(aq_t, ks, vt, ctx_ks, ctx_vt)
    return _out_ffn_call(x, ret, att, mod3, n2, nf, wo, w1, w2)
```

```python
import math

import jax
import jax.numpy as jnp
from jax import lax
from jax.experimental import pallas as pl
from jax.experimental.pallas import tpu as pltpu

D_MODEL = 1024
GRID_W = 64

RET_HEADS = 4
RET_DK = 128
RET_DV = 128
RET_W = RET_HEADS * RET_DV
RET_SCALE = RET_DK ** -0.5

ATT_HEADS = 8
ATT_KV_HEADS = 2
ATT_GROUP = ATT_HEADS // ATT_KV_HEADS
ATT_HD = 64
ATT_W = ATT_HEADS * ATT_HD
ATT_SCALE = ATT_HD ** -0.5

MIX_W = RET_W + ATT_W
D_FF = 4 * D_MODEL
ROPE_BASE = 10000.0
EPS = 1e-6
LOG2E = math.log2(math.e)

RQ_OFF = 0
RK_OFF = RQ_OFF + RET_HEADS * RET_DK
RV_OFF = RK_OFF + RET_HEADS * RET_DK
RG_OFF = RV_OFF + RET_W
AQ_OFF = RG_OFF + RET_W
AK_OFF = AQ_OFF + ATT_W
AV_OFF = AK_OFF + ATT_KV_HEADS * ATT_HD
D_IN = AV_OFF + ATT_KV_HEADS * ATT_HD

LANES = 128
BF16_SUBLANES = 16
ATT_VT_ROWS = ATT_HD + BF16_SUBLANES
MXU_DIM = 256
V7X_VMEM_BYTES = 64 * 1024 * 1024
VMEM_LIMIT = V7X_VMEM_BYTES * 7 // 8

MOD_ROWS = 16
MOD_BN = 1024
TM_CTX = 512
TM_IN = 512
TM_PROJ = 1024
TM_SUB = 256
TQ_ATT = 1024
ATT_KEYS = 256
ATT_AHEAD = 6
RET_CHUNK = 256
FF_CHUNK = 1024

F32 = jnp.float32
BF16 = jnp.bfloat16


def _dot(a, b):
    return jnp.dot(a, b, preferred_element_type=F32)


def _dot_nt(a, b):
    return lax.dot_general(a, b, (((1,), (1,)), ((), ())), preferred_element_type=F32)


def _dot_tn(a, b):
    return lax.dot_general(a, b, (((0,), (0,)), ((), ())), preferred_element_type=F32)


def _silu(x):
    return x * jax.nn.sigmoid(x)


def _head_mean_matrix(n):
    r = lax.broadcasted_iota(jnp.int32, (n, n), 0) // ATT_HD
    c = lax.broadcasted_iota(jnp.int32, (n, n), 1) // ATT_HD
    return jnp.where(r == c, 1.0 / ATT_HD, 0.0).astype(BF16)


def _head_rms_scale(blk, bd):
    ms = _dot((blk * blk).astype(BF16), bd)
    return lax.rsqrt(ms + EPS)


def _rope_ret(blk, c2, s2):
    return blk * c2 + pltpu.roll(blk, RET_DK // 2, 1) * s2


def _rope_att(blk, c2, s2, first_half):
    up = pltpu.roll(blk, ATT_HD // 2, 1)
    dn = pltpu.roll(blk, LANES - ATT_HD // 2, 1)
    return blk * c2 + jnp.where(first_half, dn, up) * s2


def _split_bf16(x):
    hi = x.astype(BF16)
    return hi, (x - hi.astype(F32)).astype(BF16)


def _mod_kernel(c_ref, w_ref, b_ref, o_ref):
    a_hi, a_lo = _split_bf16(_silu(c_ref[...]))
    w_hi, w_lo = _split_bf16(w_ref[...])
    by_hi = _dot(jnp.concatenate([a_hi, a_lo], axis=0), w_hi)
    out = by_hi[:MOD_ROWS] + by_hi[MOD_ROWS:] + _dot(a_hi, w_lo) + b_ref[...]
    for r in range(MOD_ROWS):
        o_ref[r] = out[r:r + 1, :]


def _mod_call(c_rows, w_mod, b_mod):
    n = w_mod.shape[1]
    bn = MOD_BN
    return pl.pallas_call(
        _mod_kernel,
        grid=(n // bn,),
        in_specs=[
            pl.BlockSpec((MOD_ROWS, D_MODEL), lambda j: (0, 0)),
            pl.BlockSpec((D_MODEL, bn), lambda j: (0, j)),
            pl.BlockSpec((1, bn), lambda j: (0, j)),
        ],
        out_specs=pl.BlockSpec((MOD_ROWS, 1, bn), lambda j: (0, 0, j)),
        out_shape=jax.ShapeDtypeStruct((MOD_ROWS, 1, n), F32),
        compiler_params=pltpu.CompilerParams(dimension_semantics=("arbitrary",),
                                             vmem_limit_bytes=VMEM_LIMIT),
        name="mod",
    )(c_rows, w_mod, b_mod)


def _norm_modulate(x, g, sh, sc):
    ms = jnp.mean(x * x, axis=-1, keepdims=True)
    y = x * lax.rsqrt(ms + EPS) * g
    return y * (1.0 + sc) + sh


def _store_attention_kv(k, v, ks_ref, vt_ref):
    rows = k.shape[0]
    low = lax.broadcasted_iota(jnp.int32, k.shape, 1) < ATT_HD
    k_sw = pltpu.roll(k, ATT_HD, 1)
    ks_ref[0] = jnp.where(low, k, k_sw).astype(BF16)
    ks_ref[1] = jnp.where(low, k_sw, k).astype(BF16)
    v_t = v.T
    ones = jnp.ones((BF16_SUBLANES, rows), BF16)
    for j in range(ATT_KV_HEADS):
        vt_ref[j, :ATT_HD, :] = v_t[j * ATT_HD:(j + 1) * ATT_HD].astype(BF16)
        vt_ref[j, ATT_HD:, :] = ones


def _ctx_proj_kernel(x_ref, sh_ref, sc_ref, g_ref, wrk_ref, wrv_ref, wa_ref, gk_ref,
                     ret_ref, ks_ref, vt_ref):
    h = _norm_modulate(x_ref[...], g_ref[...], sh_ref[...], sc_ref[...]).astype(BF16)
    ret_ref[:, :RET_W] = (_dot(h, wrk_ref[...]) * RET_SCALE).astype(BF16)
    ret_ref[:, RET_W:] = _dot(h, wrv_ref[...]).astype(BF16)
    pa = _dot(h, wa_ref[...])
    ak = pa[:, :LANES]
    r = _head_rms_scale(ak, _head_mean_matrix(LANES))
    _store_attention_kv(ak * r * gk_ref[...], pa[:, LANES:], ks_ref, vt_ref)


def _ctx_proj_call(ctx_rows, ctx_row, mod3, norm1_g, w_in, gk2):
    N, D = ctx_rows.shape
    tm = TM_CTX
    const = lambda i: (0, 0)
    kv_w = D_IN - AK_OFF
    assert RK_OFF % RET_W == 0 and RV_OFF % RET_W == 0 and AK_OFF % kv_w == 0
    return pl.pallas_call(
        _ctx_proj_kernel,
        grid=(N // tm,),
        in_specs=[
            pl.BlockSpec((tm, D), lambda i: (i, 0)),
            pl.BlockSpec((None, 1, D), lambda i: (ctx_row, 0, 0)),
            pl.BlockSpec((None, 1, D), lambda i: (ctx_row, 0, 1)),
            pl.BlockSpec((1, D), const),
            pl.BlockSpec((D, RET_W), lambda i: (0, RK_OFF // RET_W)),
            pl.BlockSpec((D, RET_W), lambda i: (0, RV_OFF // RET_W)),
            pl.BlockSpec((D, kv_w), lambda i: (0, AK_OFF // kv_w)),
            pl.BlockSpec((1, LANES), const),
        ],
        out_specs=[
            pl.BlockSpec((tm, 2 * RET_W), lambda i: (i, 0)),
            pl.BlockSpec((ATT_KV_HEADS, tm, LANES), lambda i: (0, i, 0)),
            pl.BlockSpec((ATT_KV_HEADS, ATT_VT_ROWS, tm), lambda i: (0, 0, i)),
        ],
        out_shape=[
            jax.ShapeDtypeStruct((N, 2 * RET_W), BF16),
            jax.ShapeDtypeStruct((ATT_KV_HEADS, N, LANES), BF16),
            jax.ShapeDtypeStruct((ATT_KV_HEADS, ATT_VT_ROWS, N), BF16),
        ],
        compiler_params=pltpu.CompilerParams(dimension_semantics=("arbitrary",),
                                             vmem_limit_bytes=VMEM_LIMIT),
        name="ctx_proj",
    )(ctx_rows, mod3, mod3, norm1_g, w_in, w_in, w_in, gk2)


def _in_proj_kernel(x_ref, sh_ref, sc_ref, g_ref, w_ref, tc_ref, ts_ref, oc_ref, os_ref, ocs_ref, oss_ref,
                    rc_ref, rs_ref, cc_ref, cs_ref, gq_ref, gk_ref, ret_ref, aq_ref, ks_ref, vt_ref):
    tm = x_ref.shape[0]
    tile = pl.program_id(1)
    h = _norm_modulate(x_ref[...], g_ref[...], sh_ref[...], sc_ref[...]).astype(BF16)
    a_c, a_s = tc_ref[pl.ds(tile, 1), :], ts_ref[pl.ds(tile, 1), :]
    cr = a_c * oc_ref[...] - a_s * os_ref[...]
    sr = a_s * ocs_ref[...] + a_c * oss_ref[...]
    lane = lax.broadcasted_iota(jnp.int32, (tm, LANES), 1)
    first_half = (lane & (ATT_HD // 2)) == 0

    grid_rows = tm // GRID_W
    row0 = tile * grid_rows
    col_lane = (lax.broadcasted_iota(jnp.int32, (GRID_W, LANES), 1) & (ATT_HD // 4)) != 0

    def axial(row_ref, col_ref):
        col_t = col_ref[...]
        return jnp.concatenate([jnp.where(col_lane, col_t, row_ref[pl.ds(row0 + r, 1), :])
                                for r in range(grid_rows)], axis=0)

    ca, sa = axial(rc_ref, cc_ref), axial(rs_ref, cs_ref)

    bd = _head_mean_matrix(MXU_DIM)
    gq, gk = gq_ref[...], gk_ref[...]
    pa = _dot(h, w_ref[:, AQ_OFF:AK_OFF])
    for j in range(ATT_W // MXU_DIM):
        blk = pa[:, j * MXU_DIM:(j + 1) * MXU_DIM]
        r = _head_rms_scale(blk, bd) * (ATT_SCALE * LOG2E)
        for i in range(MXU_DIM // LANES):
            sl = slice(i * LANES, (i + 1) * LANES)
            o = _rope_att(blk[:, sl] * gq, ca, sa, first_half) * r[:, sl]
            aq_ref[j * MXU_DIM + i * LANES:j * MXU_DIM + (i + 1) * LANES, :] = o.T.astype(BF16)

    pkv = _dot(h, w_ref[:, AK_OFF:D_IN])
    ak = pkv[:, :LANES]
    rk = _head_rms_scale(ak, bd[:LANES, :LANES])
    _store_attention_kv(_rope_att(ak * gk, ca, sa, first_half) * rk, pkv[:, LANES:], ks_ref, vt_ref)

    pq = _dot(h, w_ref[:, RQ_OFF:RK_OFF])
    for hh in range(RET_HEADS):
        sl = slice(hh * RET_DK, (hh + 1) * RET_DK)
        ret_ref[:, RQ_OFF + hh * RET_DK:RQ_OFF + (hh + 1) * RET_DK] = _rope_ret(pq[:, sl], cr, sr).astype(BF16)
    pk = _dot(h, w_ref[:, RK_OFF:RV_OFF])
    for hh in range(RET_HEADS):
        sl = slice(hh * RET_DK, (hh + 1) * RET_DK)
        ret_ref[:, RK_OFF + hh * RET_DK:RK_OFF + (hh + 1) * RET_DK] = (
            _rope_ret(pk[:, sl], cr, sr) * RET_SCALE).astype(BF16)
    ret_ref[:, RV_OFF:AQ_OFF] = _dot(h, w_ref[:, RV_OFF:AQ_OFF]).astype(BF16)


def _in_proj_call(x, mod3, norm1_g, w_in, ret_tables, axial_tables, gq2, gk2):
    B, L, D = x.shape
    tm = TM_IN
    assert tm % GRID_W == 0
    const = lambda b, i: (0, 0)
    tables = list(ret_tables) + list(axial_tables)
    return pl.pallas_call(
        _in_proj_kernel,
        grid=(B, L // tm),
        in_specs=[
            pl.BlockSpec((None, tm, D), lambda b, i: (b, i, 0)),
            pl.BlockSpec((None, 1, D), lambda b, i: (b, 0, 0)),
            pl.BlockSpec((None, 1, D), lambda b, i: (b, 0, 1)),
            pl.BlockSpec((1, D), const),
            pl.BlockSpec(w_in.shape, const),
            *[pl.BlockSpec(t.shape, const) for t in tables],
            pl.BlockSpec((1, LANES), const),
            pl.BlockSpec((1, LANES), const),
        ],
        out_specs=[
            pl.BlockSpec((None, tm, AQ_OFF), lambda b, i: (b, i, 0)),
            pl.BlockSpec((None, ATT_W, tm), lambda b, i: (b, 0, i)),
            pl.BlockSpec((None, ATT_KV_HEADS, tm, LANES), lambda b, i: (b, 0, i, 0)),
            pl.BlockSpec((None, ATT_KV_HEADS, ATT_VT_ROWS, tm), lambda b, i: (b, 0, 0, i)),
        ],
        out_shape=[
            jax.ShapeDtypeStruct((B, L, AQ_OFF), BF16),
            jax.ShapeDtypeStruct((B, ATT_W, L), BF16),
            jax.ShapeDtypeStruct((B, ATT_KV_HEADS, L, LANES), BF16),
            jax.ShapeDtypeStruct((B, ATT_KV_HEADS, ATT_VT_ROWS, L), BF16),
        ],
        compiler_params=pltpu.CompilerParams(dimension_semantics=("arbitrary", "arbitrary"),
                                             vmem_limit_bytes=VMEM_LIMIT),
        name="in_proj",
    )(x, mod3, mod3, norm1_g, w_in, *tables, gq2, gk2)


def _retention_kernel(q_ref, k_ref, v_ref, g_ref, kc_ref, vc_ref, rate_ref, gn_ref, o_ref,
                      u_ref, s_ref):
    L = q_ref.shape[0]
    Lc = kc_ref.shape[0]
    C = RET_CHUNK
    T = L // C

    lg = jnp.log1p(-jnp.exp(rate_ref[...]))
    lgf, lgb = lg[0], lg[1]
    lgf1, lgb1 = lgf[:, :RET_DK], lgb[:, :RET_DK]

    ri = lax.broadcasted_iota(jnp.int32, (C, C), 0)
    ci = lax.broadcasted_iota(jnp.int32, (C, C), 1)
    rel = (ri - ci).astype(F32)
    decay = (jnp.where(rel >= 0, jnp.exp(lgf * jnp.maximum(rel, 0.0)), 0.0)
             + jnp.where(rel <= 0, jnp.exp(lgb * jnp.maximum(-rel, 0.0)), 0.0))

    pos = lax.broadcasted_iota(jnp.int32, (C, RET_DK), 0).astype(F32)
    q_dec_f = jnp.exp(lgf1 * (pos + 1.0))
    q_dec_b = jnp.exp(lgb1 * (C - pos))
    k_dec_f = jnp.exp(lgf1 * (C - 1.0 - pos))
    k_dec_b = jnp.exp(lgb1 * pos)
    c_dec_f = jnp.exp(lgf1 * C)
    c_dec_b = jnp.exp(lgb1 * C)

    cpos = lax.broadcasted_iota(jnp.int32, (Lc, RET_DK), 0).astype(F32)
    kc = kc_ref[...].astype(F32)
    vc = vc_ref[...]
    s_f0 = _dot_tn((kc * jnp.exp(lgf1 * (Lc - 1.0 - cpos))).astype(BF16), vc)
    s_b0 = _dot_tn((kc * jnp.exp(lgb1 * cpos)).astype(BF16), vc)

    def rows(t):
        return slice(t * C, (t + 1) * C)

    for t in range(T):
        kf = k_ref[rows(t), :].astype(F32)
        kd = jnp.concatenate([(kf * k_dec_f).astype(BF16), (kf * k_dec_b).astype(BF16)], axis=1)
        u_ref[t] = _dot_tn(kd, v_ref[rows(t), :])

    s_f, s_b = s_f0, s_b0
    for t in range(T):
        tb = T - 1 - t
        s_ref[t, :RET_DK, :] = s_f.astype(BF16)
        s_ref[tb, RET_DK:, :] = s_b.astype(BF16)
        s_f = s_f * c_dec_f + u_ref[t, :RET_DK, :]
        s_b = s_b * c_dec_b + u_ref[tb, RET_DK:, :]

    gn = gn_ref[...]

    def qk(t):
        return _dot_nt(q_ref[rows(t), :], k_ref[rows(t), :])

    sc_next = qk(0)
    for t in range(T):
        sc = sc_next
        if t + 1 < T:
            sc_next = qk(t + 1)
        qf = q_ref[rows(t), :].astype(F32)
        lhs = jnp.concatenate([(sc * decay).astype(BF16), (qf * q_dec_f).astype(BF16),
                               (qf * q_dec_b).astype(BF16)], axis=1)
        rhs = jnp.concatenate([v_ref[rows(t), :], s_ref[t]], axis=0)
        o = _dot(lhs, rhs)
        mu = jnp.mean(o, axis=-1, keepdims=True)
        d = o - mu
        var = jnp.mean(d * d, axis=-1, keepdims=True)
        on = d * lax.rsqrt(var + EPS) * gn
        o_ref[rows(t), :] = (on * _silu(g_ref[rows(t), :].astype(F32))).astype(o_ref.dtype)


def _retention_call(ret_in, ctx_ret, rate, gn_g):
    B, L, _ = ret_in.shape
    Lc = ctx_ret.shape[0] // B
    H = RET_HEADS
    T = L // RET_CHUNK

    def col(off):
        return lambda b, h: (b, 0, off + h)

    return pl.pallas_call(
        _retention_kernel,
        grid=(B, H),
        in_specs=[
            pl.BlockSpec((None, L, RET_DK), col(RQ_OFF // RET_DK)),
            pl.BlockSpec((None, L, RET_DK), col(RK_OFF // RET_DK)),
            pl.BlockSpec((None, L, RET_DV), col(RV_OFF // RET_DV)),
            pl.BlockSpec((None, L, RET_DV), col(RG_OFF // RET_DV)),
            pl.BlockSpec((Lc, RET_DK), lambda b, h: (b, h)),
            pl.BlockSpec((Lc, RET_DV), lambda b, h: (b, RET_HEADS + h)),
            pl.BlockSpec((2, None, 1, RET_CHUNK), lambda b, h: (0, h, 0, 0)),
            pl.BlockSpec((1, RET_DV), lambda b, h: (0, h)),
        ],
        out_specs=pl.BlockSpec((None, L, RET_DV), col(0)),
        out_shape=jax.ShapeDtypeStruct((B, L, RET_W), BF16),
        scratch_shapes=[pltpu.VMEM((T, 2 * RET_DK, RET_DV), F32),
                        pltpu.VMEM((T, 2 * RET_DK, RET_DV), BF16)],
        compiler_params=pltpu.CompilerParams(dimension_semantics=("arbitrary", "arbitrary"),
                                             vmem_limit_bytes=VMEM_LIMIT),
        name="retention",
    )(ret_in, ret_in, ret_in, ret_in, ctx_ret, ctx_ret, rate, gn_g)


def _attention_kernel(q_ref, k_ref, vt_ref, kc_ref, vtc_ref, o_ref):
    Lc = kc_ref.shape[0]
    L = k_ref.shape[0]
    tq = q_ref.shape[1]
    ctx_tiles = Lc // ATT_KEYS
    n_tiles = ctx_tiles + L // ATT_KEYS

    def key_tile(c):
        if c < ctx_tiles:
            return kc_ref[c * ATT_KEYS:(c + 1) * ATT_KEYS, :]
        return k_ref[(c - ctx_tiles) * ATT_KEYS:(c - ctx_tiles + 1) * ATT_KEYS, :]

    def value_tile(c):
        if c < ctx_tiles:
            return vtc_ref[:, c * ATT_KEYS:(c + 1) * ATT_KEYS]
        return vt_ref[:, (c - ctx_tiles) * ATT_KEYS:(c - ctx_tiles + 1) * ATT_KEYS]

    streams = [(h, g) for h in range(tq // MXU_DIM) for g in range(ATT_GROUP)]
    pad = jnp.zeros((LANES - ATT_HD, MXU_DIM), BF16)

    def q_weights(h, g):
        return jnp.concatenate(
            [q_ref[g * ATT_HD:(g + 1) * ATT_HD, h * MXU_DIM:(h + 1) * MXU_DIM], pad], axis=0)

    wq = {st: q_weights(*st) for st in streams}

    def scores(c, st):
        s = _dot(key_tile(c), wq[st])
        return s, jnp.max(s, axis=0, keepdims=True)

    m = {}
    acc = {}

    def fold(c, st, s_mt):
        s, mt = s_mt
        v_c = value_tile(c)
        if c == 0:
            m[st] = mt
            acc[st] = _dot(v_c, jnp.exp2(s - mt).astype(BF16))
        else:
            m_new = jnp.maximum(m[st], mt)
            alpha = jnp.exp2(m[st] - m_new)
            acc[st] = acc[st] * alpha + _dot(v_c, jnp.exp2(s - m_new).astype(BF16))
            m[st] = m_new

    units = [(c, st) for c in range(n_tiles) for st in streams]
    pending = []
    for i in range(len(units) + ATT_AHEAD):
        if i < len(units):
            pending.append(scores(*units[i]))
        if i >= ATT_AHEAD:
            fold(*units[i - ATT_AHEAD], pending.pop(0))
    for h, g in streams:
        a = acc[(h, g)]
        o_ref[g * ATT_HD:(g + 1) * ATT_HD, h * MXU_DIM:(h + 1) * MXU_DIM] = (
            a[:ATT_HD] / a[ATT_HD:ATT_HD + 1]).astype(o_ref.dtype)


def _attention_call(aq_t, ks, vt, ctx_ks, ctx_vt):
    B, _, L = aq_t.shape
    Lc = ctx_ks.shape[1] // B
    tq = TQ_ATT
    assert L % ATT_KEYS == 0 and Lc % ATT_KEYS == 0 and tq % MXU_DIM == 0
    per_head = lambda b, kv, i: (b, kv, 0, 0)
    return pl.pallas_call(
        _attention_kernel,
        grid=(B, ATT_KV_HEADS, L // tq),
        in_specs=[
            pl.BlockSpec((None, ATT_GROUP * ATT_HD, tq), lambda b, kv, i: (b, kv, i)),
            pl.BlockSpec((None, None, L, LANES), per_head),
            pl.BlockSpec((None, None, ATT_VT_ROWS, L), per_head),
            pl.BlockSpec((None, Lc, LANES), lambda b, kv, i: (kv, b, 0)),
            pl.BlockSpec((None, ATT_VT_ROWS, Lc), lambda b, kv, i: (kv, 0, b)),
        ],
        out_specs=pl.BlockSpec((None, ATT_GROUP * ATT_HD, tq), lambda b, kv, i: (b, kv, i)),
        out_shape=jax.ShapeDtypeStruct((B, ATT_W, L), BF16),
        compiler_params=pltpu.CompilerParams(
            dimension_semantics=("arbitrary", "arbitrary", "arbitrary"),
            vmem_limit_bytes=VMEM_LIMIT),
        name="attention",
    )(aq_t, ks, vt, ctx_ks, ctx_vt)


def _out_ffn_kernel(x_ref, ret_ref, att_ref, g1_ref, sh2_ref, sc2_ref, g2_ref, n2_ref, nf_ref,
                    wo_r_ref, wo_a_ref, w1_ref, w2_ref, o_ref):
    tm = x_ref.shape[0]
    subs = [slice(s * TM_SUB, (s + 1) * TM_SUB) for s in range(tm // TM_SUB)]
    x1, h2 = [], []
    for r in subs:
        mix = _dot(ret_ref[r, :], wo_r_ref[...]) + _dot_tn(att_ref[:, r], wo_a_ref[...])
        x1.append(x_ref[r, :] + g1_ref[...] * mix)
        h2.append(_norm_modulate(x1[-1], n2_ref[...], sh2_ref[...], sc2_ref[...]).astype(BF16))
    for s, r in enumerate(subs):
        ff = jnp.zeros((TM_SUB, D_MODEL), F32)
        for j in range(D_FF // FF_CHUNK):
            a = jnp.maximum(_dot(h2[s], w1_ref[:, j * FF_CHUNK:(j + 1) * FF_CHUNK]), 0.0)
            ff = ff + _dot((a * a).astype(BF16), w2_ref[j * FF_CHUNK:(j + 1) * FF_CHUNK, :])
        x2 = x1[s] + g2_ref[...] * ff
        ms = jnp.mean(x2 * x2, axis=-1, keepdims=True)
        o_ref[r, :] = x2 * lax.rsqrt(ms + EPS) * nf_ref[...]


def _out_ffn_call(x, ret, att, mod3, norm2_g, final_g, wo, w1, w2):
    B, L, D = x.shape
    assert RET_W == ATT_W
    tm = TM_PROJ
    const = lambda b, i: (0, 0)
    once = pl.Buffered(1)

    def modrow(j):
        return pl.BlockSpec((None, 1, D), lambda b, i: (b, 0, j))

    return pl.pallas_call(
        _out_ffn_kernel,
        grid=(B, L // tm),
        in_specs=[
            pl.BlockSpec((None, tm, D), lambda b, i: (b, i, 0)),
            pl.BlockSpec((None, tm, RET_W), lambda b, i: (b, i, 0)),
            pl.BlockSpec((None, ATT_W, tm), lambda b, i: (b, 0, i)),
            modrow(2), modrow(3), modrow(4), modrow(5),
            pl.BlockSpec((1, D), const),
            pl.BlockSpec((1, D), const),
            pl.BlockSpec((RET_W, D), lambda b, i: (0, 0), pipeline_mode=once),
            pl.BlockSpec((ATT_W, D), lambda b, i: (1, 0), pipeline_mode=once),
            pl.BlockSpec(w1.shape, const, pipeline_mode=once),
            pl.BlockSpec(w2.shape, const, pipeline_mode=once),
        ],
        out_specs=pl.BlockSpec((None, tm, D), lambda b, i: (b, i, 0)),
        out_shape=jax.ShapeDtypeStruct((B, L, D), F32),
        compiler_params=pltpu.CompilerParams(dimension_semantics=("arbitrary", "arbitrary"),
                                             vmem_limit_bytes=VMEM_LIMIT),
        name="out_ffn",
    )(x, ret, att, mod3, mod3, mod3, mod3, norm2_g, final_g, wo, wo, w1, w2)


def _freqs(n_pairs):
    return ROPE_BASE ** (-jnp.arange(n_pairs, dtype=F32) / n_pairs)


def _rope_tables(L, tm):
    lane = jnp.arange(LANES)
    fr = jnp.tile(_freqs(RET_DK // 2), 2)[None, :]
    sign = jnp.where(lane < RET_DK // 2, -1.0, 1.0).astype(F32)[None, :]
    base = (jnp.arange(L // tm) * tm).astype(F32)[:, None] * fr
    off = jnp.arange(tm, dtype=F32)[:, None] * fr
    ret = (jnp.cos(base), jnp.sin(base), jnp.cos(off), jnp.sin(off),
           sign * jnp.cos(off), sign * jnp.sin(off))
    freq = jnp.tile(_freqs(ATT_HD // 4), LANES // (ATT_HD // 4))[None, :]
    neg = (lane % ATT_HD < ATT_HD // 2)[None, :]

    def axis_tables(n):
        a = jnp.arange(n, dtype=F32)[:, None] * freq
        return jnp.cos(a), jnp.where(neg, -jnp.sin(a), jnp.sin(a))

    row_c, row_s = axis_tables(L // GRID_W)
    col_c, col_s = axis_tables(GRID_W)
    return ret, (row_c, row_s, col_c, col_s)


def kernel(x, c, ctx, c_ctx, w_mod, b_mod, norm1_g, norm2_g, w_in, w_out, ret_log_rate, ret_gn_g,
           q_norm_g, k_norm_g, w_ff1, w_ff2, final_norm_g):
    B, L, D = x.shape
    assert w_mod.shape[0] == 1, "single-layer configuration"
    assert B + 1 <= MOD_ROWS and L % TM_PROJ == 0 and L % TM_IN == 0 and L % TQ_ATT == 0 and L % RET_CHUNK == 0

    c_rows = jnp.zeros((MOD_ROWS, D), F32).at[:B].set(c).at[B].set(c_ctx)
    mod3 = _mod_call(c_rows, w_mod[0], b_mod[0][None, :])

    w_in_b = w_in[0].astype(BF16)
    wo = w_out[0].astype(BF16)
    w1 = w_ff1[0].astype(BF16)
    w2 = w_ff2[0].astype(BF16)
    n1 = norm1_g[0][None, :]
    n2 = norm2_g[0][None, :]
    nf = final_norm_g[None, :]
    gq2 = jnp.tile(q_norm_g[0], LANES // ATT_HD)[None, :]
    gk2 = jnp.tile(k_norm_g[0], LANES // ATT_HD)[None, :]
    gn = ret_gn_g[0][None, :]
    rate = jnp.broadcast_to(ret_log_rate[0].astype(F32)[:, :, None, None],
                            (2, RET_HEADS, 1, RET_CHUNK))
    ret_tables, axial_tables = _rope_tables(L, TM_IN)

    Lc = ctx.shape[1]
    assert (B * Lc) % TM_CTX == 0
    ctx_ret, ctx_ks, ctx_vt = _ctx_proj_call(ctx.reshape(B * Lc, D), B, mod3, n1, w_in_b, gk2)
    ret_in, aq_t, ks, vt = _in_proj_call(x, mod3, n1, w_in_b, ret_tables, axial_tables, gq2, gk2)
    ret = _retention_call(ret_in, ctx_ret, rate, gn)
    att = _attention_call(aq_t, ks, vt, ctx_ks, ctx_vt)
    return _out_ffn_call(x, ret, att, mod3, n2, nf, wo, w1, w2)
```
